```python
import jax, jax.numpy as jnp
from jax import lax
import numpy as np

D_MODEL = 1024
BATCH = 16
SEQ = 2048
DEPTH = 2
DEC_BATCH = 16
DEC_SEQ = 16
PAST_LEN = 2048

CHUNK = 64
N_MEM = 256
EPS = 1e-6
NEG_INF = -1e30

A_HEAD_DIM = 64
D_A = D_MODEL // 4
A_HEADS = D_A // A_HEAD_DIM
A_LEFT_CHUNKS = 8
A_REACH = A_LEFT_CHUNKS * CHUNK
A_BAND = (A_LEFT_CHUNKS + 1) * CHUNK
REL_CLIP = 128
A_SCALE = A_HEAD_DIM ** -0.5

B_QK_DIM = 64
B_V_DIM = 128
D_B = D_MODEL // 2
B_HEADS = D_B // B_V_DIM
D_BQK = B_HEADS * B_QK_DIM
RET_GAMMA_EXP0 = 5.0
ROPE_BASE = 10000.0

C_CHANNELS = D_MODEL // 4
CONV_WIDTH = 31

D_MIX = D_A + D_B + C_CHANNELS
D_IN = 3 * D_A + 2 * D_BQK + 2 * D_B + 2 * C_CHANNELS

X_HEADS = 4
X_HEAD_DIM = D_MODEL // X_HEADS

D_FF = 11 * D_MODEL // 4
N_EXPERTS = 8
TOP_K = 2
D_FF_EXPERT = D_FF // 2
N_DENSE = (DEPTH + 1) // 2
N_MOE = DEPTH // 2

kernel_name = 'hybrid_streaming_encoder_step'


def rms_norm(x, g):
    xf = x.astype(jnp.float32)
    y = xf * lax.rsqrt(jnp.mean(xf * xf, axis=-1, keepdims=True) + EPS)
    return (y * g.astype(jnp.float32)).astype(x.dtype)


def layer_norm(x, g, b):
    xf = x.astype(jnp.float32)
    mu = jnp.mean(xf, axis=-1, keepdims=True)
    var = jnp.mean(jnp.square(xf - mu), axis=-1, keepdims=True)
    y = (xf - mu) * lax.rsqrt(var + EPS) * g.astype(jnp.float32) + b.astype(jnp.float32)
    return y.astype(x.dtype)


def rotary(x, pos):
    half = x.shape[-1] // 2
    inv_freq = ROPE_BASE ** (-jnp.arange(half, dtype=jnp.float32) / half)
    ang = pos[:, None] * inv_freq[None, :]
    cos = jnp.cos(ang)[None, :, None, :]
    sin = jnp.sin(ang)[None, :, None, :]
    xf = x.astype(jnp.float32)
    x1, x2 = xf[..., :half], xf[..., half:]
    return jnp.concatenate([x1 * cos - x2 * sin, x2 * cos + x1 * sin], axis=-1).astype(x.dtype)


def band_attention(q, k, v, q_pos, k_pos, k_valid, rel_bias):
    rel = jnp.clip(q_pos[:, :, None] - k_pos[:, None, :], -REL_CLIP, REL_CLIP) + REL_CLIP
    bias = jnp.transpose(rel_bias[:, rel], (1, 0, 2, 3)).astype(jnp.float32)
    s = jnp.einsum('ngqhd,ngkhd->nghqk', q, k).astype(jnp.float32) * A_SCALE + bias[None]
    s = jnp.where(k_valid[None, :, None, None, :], s, NEG_INF)
    p = jax.nn.softmax(s, axis=-1).astype(v.dtype)
    return jnp.einsum('nghqk,ngkhd->ngqhd', p, v)


def band_attention_prompt(qa, ka, va, rel_bias):
    n, t = qa.shape[:2]
    nc = t // CHUNK

    def band(a):
        ac = a.reshape(n, nc, CHUNK, A_HEADS, A_HEAD_DIM)
        ac = jnp.pad(ac, ((0, 0), (A_LEFT_CHUNKS, 0), (0, 0), (0, 0), (0, 0)))
        return jnp.concatenate([ac[:, j:j + nc] for j in range(A_LEFT_CHUNKS + 1)], axis=2)

    c = jnp.arange(nc)[:, None]
    q_pos = c * CHUNK + jnp.arange(CHUNK)[None, :]
    k_pos = (c - A_LEFT_CHUNKS) * CHUNK + jnp.arange(A_BAND)[None, :]
    o = band_attention(qa.reshape(n, nc, CHUNK, A_HEADS, A_HEAD_DIM), band(ka), band(va),
                       q_pos, k_pos, k_pos >= 0, rel_bias)
    return o.reshape(n, t, A_HEADS, A_HEAD_DIM)


def band_attention_sample(qa, ka, va, ck, cv, pos0, rel_bias):
    t = qa.shape[1]
    l = ck.shape[1]
    k_all = jnp.concatenate([ck.astype(ka.dtype), ka], axis=1)[:, None]
    v_all = jnp.concatenate([cv.astype(va.dtype), va], axis=1)[:, None]
    q_pos = (pos0 + jnp.arange(t))[None, :]
    k_pos = (pos0 - l + jnp.arange(l + t))[None, :]
    valid = jnp.ones((1, l + t), dtype=bool)
    return band_attention(qa[:, None], k_all, v_all, q_pos, k_pos, valid, rel_bias)[:, 0]


def retention(q, k, v, s0, gn_g):
    n, t, h, dk = q.shape
    dv = v.shape[-1]
    blk = min(CHUNK, t)
    nb = t // blk
    log_g = jnp.log(1.0 - 2.0 ** (-(RET_GAMMA_EXP0 + jnp.arange(h, dtype=jnp.float32))))
    i = jnp.arange(blk, dtype=jnp.float32)
    diff = i[:, None] - i[None, :]
    dmask = jnp.where(diff[None] >= 0, jnp.exp(jnp.maximum(diff, 0.0)[None] * log_g[:, None, None]), 0.0)
    q_dec = jnp.exp((i[:, None] + 1.0) * log_g[None, :])
    k_dec = jnp.exp((blk - 1.0 - i)[:, None] * log_g[None, :])
    s_dec = jnp.exp(blk * log_g)

    def blocks(a):
        return jnp.moveaxis(a.astype(jnp.float32).reshape(n, nb, blk, h, a.shape[-1]), 1, 0)

    def step(s, inp):
        qi, ki, vi = inp
        att = jnp.einsum('nihd,njhd->nhij', qi, ki) * dmask[None]
        o = (jnp.einsum('nhij,njhe->nihe', att, vi)
             + jnp.einsum('nihd,nhde->nihe', qi * q_dec[None, :, :, None], s))
        s = s * s_dec[None, :, None, None] + jnp.einsum('njhd,njhe->nhde', ki * k_dec[None, :, :, None], vi)
        return s, o

    s_fin, o = lax.scan(step, s0.astype(jnp.float32), (blocks(q), blocks(k), blocks(v)))
    o = jnp.moveaxis(o, 0, 1).reshape(n, t, h, dv)
    mu = jnp.mean(o, axis=-1, keepdims=True)
    var = jnp.mean(jnp.square(o - mu), axis=-1, keepdims=True)
    o = ((o - mu) * lax.rsqrt(var + EPS)).reshape(n, t, h * dv) * gn_g.astype(jnp.float32)
    return o, s_fin


def conv_module(ca, cb, buf, conv_w, conv_b, ln_g, ln_b):
    u = ca * jax.nn.sigmoid(cb)
    up = jnp.concatenate([buf.astype(u.dtype), u], axis=1)
    y = lax.conv_general_dilated(up, conv_w[:, None, :].astype(u.dtype), (1,), 'VALID',
                                 dimension_numbers=('NWC', 'WIO', 'NWC'),
                                 feature_group_count=C_CHANNELS)
    y = jax.nn.silu(layer_norm(y + conv_b.astype(u.dtype), ln_g, ln_b))
    return y, up[:, up.shape[1] - (CONV_WIDTH - 1):]


def token_mixer(h, pos0, attn_cache, ret_state, conv_state,
                w_in, rel_bias, ret_gn_g, conv_w, conv_b, conv_ln_g, conv_ln_b, w_out):
    n, t, _ = h.shape
    splits = [int(s) for s in np.cumsum([D_A, D_A, D_A, D_BQK, D_BQK, D_B, D_B, C_CHANNELS])]
    qa, ka, va, qb, kb, vb, gb, ca, cb = jnp.split(h @ w_in, splits, axis=-1)
    qa = qa.reshape(n, t, A_HEADS, A_HEAD_DIM)
    ka = ka.reshape(n, t, A_HEADS, A_HEAD_DIM)
    va = va.reshape(n, t, A_HEADS, A_HEAD_DIM)
    if attn_cache is None:
        o_a = band_attention_prompt(qa, ka, va, rel_bias)
        keep = min(A_REACH, t)
        new_k, new_v = ka[:, t - keep:], va[:, t - keep:]
    else:
        o_a = band_attention_sample(qa, ka, va, attn_cache[0], attn_cache[1], pos0, rel_bias)
        new_k, new_v = ka, va
    pos = pos0 + jnp.arange(t, dtype=jnp.float32)
    qb = rotary(qb.reshape(n, t, B_HEADS, B_QK_DIM), pos)
    kb = rotary(kb.reshape(n, t, B_HEADS, B_QK_DIM), pos) * (B_QK_DIM ** -0.5)
    vb = vb.reshape(n, t, B_HEADS, B_V_DIM)
    o_b, new_s = retention(qb, kb, vb, ret_state, ret_gn_g)
    o_b = jax.nn.silu(gb) * o_b.astype(h.dtype)
    o_c, new_conv = conv_module(ca, cb, conv_state, conv_w, conv_b, conv_ln_g, conv_ln_b)
    y = jnp.concatenate([o_a.reshape(n, t, D_A), o_b, o_c], axis=-1) @ w_out
    return y, new_k, new_v, new_s, new_conv


def memory_kv(mem, wk, wv):
    n, m, _ = mem.shape
    return ((mem @ wk).reshape(n, m, X_HEADS, X_HEAD_DIM), (mem @ wv).reshape(n, m, X_HEADS, X_HEAD_DIM))


def cross_attention(h, mk, mv, wq, wo):
    n, t, _ = h.shape
    q = (h @ wq).reshape(n, t, X_HEADS, X_HEAD_DIM)
    s = jnp.einsum('nqhd,nkhd->nhqk', q, mk).astype(jnp.float32) * (X_HEAD_DIM ** -0.5)
    p = jax.nn.softmax(s, axis=-1).astype(mv.dtype)
    o = jnp.einsum('nhqk,nkhd->nqhd', p, mv).reshape(n, t, X_HEADS * X_HEAD_DIM)
    return o @ wo


def swiglu(h, wg, wu, wd):
    return (jax.nn.silu(h @ wg) * (h @ wu)) @ wd


def moe_swiglu(h, router, wg, wu, wd):
    probs = jax.nn.softmax((h @ router).astype(jnp.float32), axis=-1)
    top_p, top_i = lax.top_k(probs, TOP_K)
    top_p = top_p / jnp.sum(top_p, axis=-1, keepdims=True)
    combine = jnp.sum(jax.nn.one_hot(top_i, N_EXPERTS, dtype=jnp.float32) * top_p[..., None], axis=-2)
    combine = combine.astype(h.dtype)
    out = jnp.zeros(h.shape, h.dtype)
    for e in range(N_EXPERTS):
        out = out + combine[..., e:e + 1] * swiglu(h, wg[e], wu[e], wd[e])
    return out


def channel_mixer(h, l, ffn_w_gate, ffn_w_up, ffn_w_down, router_w, moe_w_gate, moe_w_up, moe_w_down):
    i = l // 2
    if l % 2 == 0:
        return swiglu(h, ffn_w_gate[i], ffn_w_up[i], ffn_w_down[i])
    return moe_swiglu(h, router_w[i], moe_w_gate[i], moe_w_up[i], moe_w_down[i])


def setup_inputs(seed: int = 0) -> dict:
    key = jax.random.key(seed)
    ks = iter(jax.random.split(key, 40))

    def nrm(shape, scale):
        return jax.random.normal(next(ks), shape, jnp.float32) * scale

    def gain(shape):
        return 1.0 + nrm(shape, 0.01)

    la = min(A_REACH, PAST_LEN)
    return {
        'x_prompt': nrm((BATCH, SEQ, D_MODEL), 1.0),
        'x_sample': nrm((DEC_BATCH, DEC_SEQ, D_MODEL), 1.0),
        'cache_attn_k': nrm((DEPTH, DEC_BATCH, la, A_HEADS, A_HEAD_DIM), 1.0),
        'cache_attn_v': nrm((DEPTH, DEC_BATCH, la, A_HEADS, A_HEAD_DIM), 1.0),
        'state_ret': nrm((DEPTH, DEC_BATCH, B_HEADS, B_QK_DIM, B_V_DIM), 1.0),
        'state_conv': nrm((DEPTH, DEC_BATCH, CONV_WIDTH - 1, C_CHANNELS), 0.5),
        'cache_mem_k': nrm((DEPTH, DEC_BATCH, N_MEM, X_HEADS, X_HEAD_DIM), 1.0),
        'cache_mem_v': nrm((DEPTH, DEC_BATCH, N_MEM, X_HEADS, X_HEAD_DIM), 1.0),
        'mem_prompt': nrm((BATCH, N_MEM, D_MODEL), 1.0),
        'norm_mix_g': gain((DEPTH, D_MODEL)),
        'w_in': nrm((DEPTH, D_MODEL, D_IN), D_MODEL ** -0.5),
        'rel_bias': nrm((DEPTH, A_HEADS, 2 * REL_CLIP + 1), 0.5),
        'ret_gn_g': gain((DEPTH, D_B)),
        'conv_w': nrm((DEPTH, CONV_WIDTH, C_CHANNELS), CONV_WIDTH ** -0.5),
        'conv_b': nrm((DEPTH, C_CHANNELS), 0.01),
        'conv_ln_g': gain((DEPTH, C_CHANNELS)),
        'conv_ln_b': nrm((DEPTH, C_CHANNELS), 0.01),
        'w_out': nrm((DEPTH, D_MIX, D_MODEL), D_MIX ** -0.5),
        'norm_mem_g': gain((DEPTH, D_MODEL)),
        'wx_q': nrm((DEPTH, D_MODEL, X_HEADS * X_HEAD_DIM), D_MODEL ** -0.5),
        'wx_k': nrm((DEPTH, D_MODEL, X_HEADS * X_HEAD_DIM), D_MODEL ** -0.5),
        'wx_v': nrm((DEPTH, D_MODEL, X_HEADS * X_HEAD_DIM), D_MODEL ** -0.5),
        'wx_o': nrm((DEPTH, X_HEADS * X_HEAD_DIM, D_MODEL), (X_HEADS * X_HEAD_DIM) ** -0.5),
        'norm_ffn_g': gain((DEPTH, D_MODEL)),
        'ffn_w_gate': nrm((N_DENSE, D_MODEL, D_FF), D_MODEL ** -0.5),
        'ffn_w_up': nrm((N_DENSE, D_MODEL, D_FF), D_MODEL ** -0.5),
        'ffn_w_down': nrm((N_DENSE, D_FF, D_MODEL), D_FF ** -0.5),
        'router_w': nrm((N_MOE, D_MODEL, N_EXPERTS), D_MODEL ** -0.5),
        'moe_w_gate': nrm((N_MOE, N_EXPERTS, D_MODEL, D_FF_EXPERT), D_MODEL ** -0.5),
        'moe_w_up': nrm((N_MOE, N_EXPERTS, D_MODEL, D_FF_EXPERT), D_MODEL ** -0.5),
        'moe_w_down': nrm((N_MOE, N_EXPERTS, D_FF_EXPERT, D_MODEL), D_FF_EXPERT ** -0.5),
        'final_norm_g': gain((D_MODEL,)),
    }


def reference(x_prompt, x_sample, cache_attn_k, cache_attn_v, state_ret, state_conv, cache_mem_k, cache_mem_v,
              mem_prompt, norm_mix_g, w_in, rel_bias, ret_gn_g, conv_w, conv_b, conv_ln_g, conv_ln_b, w_out,
              norm_mem_g, wx_q, wx_k, wx_v, wx_o, norm_ffn_g, ffn_w_gate, ffn_w_up, ffn_w_down,
              router_w, moe_w_gate, moe_w_up, moe_w_down, final_norm_g):
    xp, xs = x_prompt, x_sample
    n_p = xp.shape[0]
    pk, pv, ps, pc, pmk, pmv = [], [], [], [], [], []
    sk, sv, ss, sc = [], [], [], []
    for l in range(DEPTH):
        mix_w = (w_in[l], rel_bias[l], ret_gn_g[l], conv_w[l], conv_b[l], conv_ln_g[l], conv_ln_b[l], w_out[l])
        ret0 = jnp.zeros((n_p, B_HEADS, B_QK_DIM, B_V_DIM), jnp.float32)
        conv0 = jnp.zeros((n_p, CONV_WIDTH - 1, C_CHANNELS), xp.dtype)
        y, k_new, v_new, s_new, c_new = token_mixer(rms_norm(xp, norm_mix_g[l]), 0, None, ret0, conv0, *mix_w)
        xp = xp + y
        pk.append(k_new)
        pv.append(v_new)
        ps.append(s_new)
        pc.append(c_new)
        y, k_new, v_new, s_new, c_new = token_mixer(rms_norm(xs, norm_mix_g[l]), PAST_LEN,
                                                    (cache_attn_k[l], cache_attn_v[l]),
                                                    state_ret[l], state_conv[l], *mix_w)
        xs = xs + y
        sk.append(k_new)
        sv.append(v_new)
        ss.append(s_new)
        sc.append(c_new)
        mk, mv = memory_kv(mem_prompt, wx_k[l], wx_v[l])
        pmk.append(mk)
        pmv.append(mv)
        xp = xp + cross_attention(rms_norm(xp, norm_mem_g[l]), mk, mv, wx_q[l], wx_o[l])
        xs = xs + cross_attention(rms_norm(xs, norm_mem_g[l]), cache_mem_k[l].astype(xs.dtype),
                                  cache_mem_v[l].astype(xs.dtype), wx_q[l], wx_o[l])
        xp = xp + channel_mixer(rms_norm(xp, norm_ffn_g[l]), l, ffn_w_gate, ffn_w_up, ffn_w_down,
                                router_w, moe_w_gate, moe_w_up, moe_w_down)
        xs = xs + channel_mixer(rms_norm(xs, norm_ffn_g[l]), l, ffn_w_gate, ffn_w_up, ffn_w_down,
                                router_w, moe_w_gate, moe_w_up, moe_w_down)
    y_prompt = rms_norm(xp, final_norm_g)
    y_sample = rms_norm(xs, final_norm_g)
    new_attn_k_prompt = jnp.stack(pk)
    new_attn_v_prompt = jnp.stack(pv)
    new_ret_prompt = jnp.stack(ps)
    new_conv_prompt = jnp.stack(pc)
    new_mem_k_prompt = jnp.stack(pmk)
    new_mem_v_prompt = jnp.stack(pmv)
    new_attn_k_sample = jnp.stack(sk)
    new_attn_v_sample = jnp.stack(sv)
    new_ret_sample = jnp.stack(ss)
    new_conv_sample = jnp.stack(sc)
    return (y_prompt, y_sample, new_attn_k_prompt, new_attn_v_prompt, new_ret_prompt, new_conv_prompt,
            new_mem_k_prompt, new_mem_v_prompt, new_attn_k_sample, new_attn_v_sample, new_ret_sample,
            new_conv_sample)
```

```python
import functools

import jax
import jax.numpy as jnp
from jax import lax
from jax.experimental import pallas as pl
from jax.experimental.pallas import tpu as pltpu

F32 = jnp.float32
BF16 = jnp.bfloat16

D_MODEL = 1024
PAST_LEN = 2048
CHUNK = 64
EPS = 1e-6
NEG_INF = -1e30
LANES = 128

A_HEADS = 4
A_HEAD_DIM = 64
D_A = A_HEADS * A_HEAD_DIM
A_LEFT_CHUNKS = 8
A_REACH = A_LEFT_CHUNKS * CHUNK
REL_CLIP = 128
A_SCALE = A_HEAD_DIM ** -0.5

B_HEADS = 4
B_QK_DIM = 64
B_V_DIM = 128
D_BQK = B_HEADS * B_QK_DIM
D_B = B_HEADS * B_V_DIM
RET_GAMMA_EXP0 = 5.0
ROPE_BASE = 10000.0

C_CHANNELS = 256
CONV_WIDTH = 31
CONV_PAD = 32

X_HEADS = 4
X_HEAD_DIM = D_MODEL // X_HEADS

D_FF = 11 * D_MODEL // 4
FF_CHUNK = 256
N_EXPERTS = 8
TOP_K = 2
D_FF_EXPERT = D_FF // 2

COL_QA, COL_KVA, COL_QKB, COL_VB, COL_GB, COL_CAB, D_IN = 0, 256, 768, 1280, 1792, 2304, 2816

VMEM_LIMIT = 56 * 1024 * 1024


def _params(sem):
    return pltpu.CompilerParams(dimension_semantics=sem, vmem_limit_bytes=VMEM_LIMIT)


def _rms(x, g):
    return x * lax.rsqrt(jnp.mean(x * x, axis=-1, keepdims=True) + EPS) * g


def _dot(a, b):
    return jnp.dot(a, b, preferred_element_type=F32)


def _dot_nt(a, b):
    return lax.dot_general(a, b, (((1,), (1,)), ((), ())), preferred_element_type=F32)


def _dot_tn(a, b):
    return lax.dot_general(a, b, (((0,), (0,)), ((), ())), preferred_element_type=F32)


def _const_spec(shape):
    return pl.BlockSpec(shape, lambda *_: (0,) * len(shape))


def _in_proj_kernel(x_ref, g_ref, w_ref, aq_ref, akv_ref, bqk_ref, bv_ref, bg_ref, cab_ref):
    h = _rms(x_ref[...], g_ref[...]).astype(BF16)
    aq_ref[...] = _dot(h, w_ref[:, COL_QA:COL_KVA]).astype(BF16)
    akv_ref[...] = _dot(h, w_ref[:, COL_KVA:COL_QKB])
    bqk_ref[...] = _dot(h, w_ref[:, COL_QKB:COL_VB]).astype(BF16)
    bv_ref[...] = _dot(h, w_ref[:, COL_VB:COL_GB]).astype(BF16)
    bg_ref[...] = _dot(h, w_ref[:, COL_GB:COL_CAB]).astype(BF16)
    cab_ref[...] = _dot(h, w_ref[:, COL_CAB:D_IN])


def _in_proj(x2d, g, w_in):
    m = x2d.shape[0]
    tm = min(m, 512)
    widths = ((256, BF16), (512, F32), (512, BF16), (512, BF16), (512, BF16), (512, F32))
    return pl.pallas_call(
        _in_proj_kernel,
        grid=(m // tm,),
        in_specs=[pl.BlockSpec((tm, D_MODEL), lambda i: (i, 0)),
                  _const_spec((1, D_MODEL)),
                  _const_spec((D_MODEL, D_IN))],
        out_specs=[pl.BlockSpec((tm, w), lambda i: (i, 0)) for w, _ in widths],
        out_shape=[jax.ShapeDtypeStruct((m, w), dt) for w, dt in widths],
        compiler_params=_params(("parallel",)),
        name="in_proj",
    )(x2d, g, w_in)


def _band_attn_kernel(*refs, t, tq, has_hist):
    if has_hist:
        aq_ref, akv_ref, hk_ref, hv_ref, bias_ref, o_ref, kc, vc = refs
    else:
        aq_ref, akv_ref, bias_ref, o_ref, kc, vc = refs
    j = pl.program_id(1)
    span = A_REACH + tq

    @pl.when(j == 0)
    def _():
        if has_hist:
            kc[0:A_REACH, :] = hk_ref[...].astype(BF16)
            vc[0:A_REACH, :] = hv_ref[...].astype(BF16)
        else:
            kc[0:A_REACH, :] = jnp.zeros((A_REACH, D_A), BF16)
            vc[0:A_REACH, :] = jnp.zeros((A_REACH, D_A), BF16)
        kc[A_REACH:A_REACH + t, :] = akv_ref[:, 0:D_A].astype(BF16)
        vc[A_REACH:A_REACH + t, :] = akv_ref[:, D_A:2 * D_A].astype(BF16)

    t0 = pl.multiple_of(j * tq, tq)
    q = aq_ref[...]
    lane = lax.broadcasted_iota(jnp.int32, (tq, LANES), 1)
    if not has_hist:
        col = lax.broadcasted_iota(jnp.int32, (tq, span), 1)
        k_ok = col >= A_REACH - t0
    outs = []
    for p in range(A_HEADS // 2):
        qp = q[:, p * LANES:(p + 1) * LANES]
        kp = kc[pl.ds(t0, span), p * LANES:(p + 1) * LANES]
        vp = vc[pl.ds(t0, span), p * LANES:(p + 1) * LANES]
        o_pair = None
        for hh in range(2):
            own = (lane < A_HEAD_DIM) if hh == 0 else (lane >= A_HEAD_DIM)
            qm = jnp.where(own, qp.astype(F32), 0.0).astype(BF16)
            s = _dot_nt(qm, kp) * A_SCALE + bias_ref[2 * p + hh]
            if not has_hist:
                s = jnp.where(k_ok, s, NEG_INF)
            m = jnp.max(s, axis=-1, keepdims=True)
            e = jnp.exp(s - m)
            l = jnp.sum(e, axis=-1, keepdims=True)
            o = _dot(e.astype(BF16), vp) / l
            o_pair = o if hh == 0 else jnp.where(own, o, o_pair)
        outs.append(o_pair)
    o_ref[...] = jnp.concatenate(outs, axis=1).astype(BF16)


def _band_bias(rel_bias_l, tq):
    i = jnp.arange(tq)[:, None]
    j = jnp.arange(A_REACH + tq)[None, :]
    rel = jnp.clip(A_REACH + i - j, -REL_CLIP, REL_CLIP) + REL_CLIP
    in_band = (j // CHUNK >= i // CHUNK) & (j // CHUNK <= i // CHUNK + A_LEFT_CHUNKS)
    return jnp.where(in_band[None], rel_bias_l[:, rel].astype(F32), NEG_INF)


def _band_attn(aq, akv, hist, rel_bias_l):
    n, t, _ = aq.shape
    tq = min(t, 128)
    span = A_REACH + tq
    bias = _band_bias(rel_bias_l, tq)
    has_hist = hist is not None
    in_specs = [pl.BlockSpec((None, tq, D_A), lambda b, j: (b, j, 0)),
                pl.BlockSpec((None, t, 2 * D_A), lambda b, j: (b, 0, 0))]
    args = [aq, akv]
    if has_hist:
        in_specs += [pl.BlockSpec((None, A_REACH, D_A), lambda b, j: (b, 0, 0))] * 2
        args += list(hist)
    in_specs.append(_const_spec((A_HEADS, tq, span)))
    args.append(bias)
    return pl.pallas_call(
        functools.partial(_band_attn_kernel, t=t, tq=tq, has_hist=has_hist),
        grid=(n, t // tq),
        in_specs=in_specs,
        out_specs=pl.BlockSpec((None, tq, D_A), lambda b, j: (b, j, 0)),
        out_shape=jax.ShapeDtypeStruct((n, t, D_A), BF16),
        scratch_shapes=[pltpu.VMEM((A_REACH + t, D_A), BF16)] * 2,
        compiler_params=_params(("parallel", "arbitrary")),
        name="band_attn",
    )(*args)


def _swap_halves(x):
    lane = lax.broadcasted_iota(jnp.int32, x.shape, 1)
    first = (lane % B_QK_DIM) < (B_QK_DIM // 2)
    return jnp.where(first, pltpu.roll(x, LANES - B_QK_DIM // 2, axis=1), pltpu.roll(x, B_QK_DIM // 2, axis=1))


def _retention_kernel(bqk_ref, bv_ref, bg_ref, cos_ref, sin_ref, dmask_ref, qdec_ref, kdec_ref, sdec_ref,
                      s0_ref, gn_ref, ob_ref, sfin_ref, st, *, t, bc):
    for h in range(B_HEADS):
        off = (h % 2) * B_QK_DIM
        st[h] = jnp.zeros((LANES, B_V_DIM), F32)
        st[h, off:off + B_QK_DIM, :] = s0_ref[h]

    lane = lax.broadcasted_iota(jnp.int32, (bc, LANES), 1)

    def chunk(c, carry):
        r0 = pl.multiple_of(c * bc, bc)
        rows = pl.ds(r0, bc)
        cs = cos_ref[rows, :]
        sn = sin_ref[rows, :]
        for p in range(B_HEADS // 2):
            qx = bqk_ref[rows, p * LANES:(p + 1) * LANES].astype(F32)
            kx = bqk_ref[rows, D_BQK + p * LANES:D_BQK + (p + 1) * LANES].astype(F32)
            qr = qx * cs + _swap_halves(qx) * sn
            kr = (kx * cs + _swap_halves(kx) * sn) * (B_QK_DIM ** -0.5)
            for hh in range(2):
                h = 2 * p + hh
                own = (lane < B_QK_DIM) if hh == 0 else (lane >= B_QK_DIM)
                qh = jnp.where(own, qr, 0.0)
                kh = jnp.where(own, kr, 0.0)
                v = bv_ref[rows, h * B_V_DIM:(h + 1) * B_V_DIM]
                att = _dot_nt(qh.astype(BF16), kh.astype(BF16)) * dmask_ref[h]
                o = (_dot(att.astype(BF16), v)
                     + _dot((qh * qdec_ref[h]).astype(BF16), st[h].astype(BF16)))
                st[h] = st[h] * sdec_ref[h] + _dot_tn((kh * kdec_ref[h]).astype(BF16), v)
                mu = jnp.mean(o, axis=-1, keepdims=True)
                d = o - mu
                var = jnp.mean(d * d, axis=-1, keepdims=True)
                on = d * lax.rsqrt(var + EPS) * gn_ref[:, h * B_V_DIM:(h + 1) * B_V_DIM]
                g = bg_ref[rows, h * B_V_DIM:(h + 1) * B_V_DIM].astype(F32)
                ob_ref[rows, h * B_V_DIM:(h + 1) * B_V_DIM] = (g * jax.nn.sigmoid(g) * on).astype(BF16)
        return carry

    lax.fori_loop(0, t // bc, chunk, 0)
    for h in range(B_HEADS):
        off = (h % 2) * B_QK_DIM
        sfin_ref[h] = st[h, off:off + B_QK_DIM, :]


def _retention_tables(t, bc, pos0):
    log_g = jnp.log(1.0 - 2.0 ** (-(RET_GAMMA_EXP0 + jnp.arange(B_HEADS, dtype=F32))))
    i = jnp.arange(bc, dtype=F32)
    diff = i[:, None] - i[None, :]
    dmask = jnp.where(diff[None] >= 0, jnp.exp(jnp.maximum(diff, 0.0)[None] * log_g[:, None, None]), 0.0)
    qdec = jnp.exp((i[None, :] + 1.0) * log_g[:, None])
    kdec = jnp.exp((bc - 1.0 - i)[None, :] * log_g[:, None])
    sdec = jnp.exp(bc * log_g)
    qdec = jnp.broadcast_to(qdec[:, :, None], (B_HEADS, bc, LANES))
    kdec = jnp.broadcast_to(kdec[:, :, None], (B_HEADS, bc, LANES))
    sdec = jnp.broadcast_to(sdec[:, None, None], (B_HEADS, 1, B_V_DIM))
    half = B_QK_DIM // 2
    pos = pos0 + jnp.arange(t, dtype=F32)
    inv_freq = ROPE_BASE ** (-jnp.arange(half, dtype=F32) / half)
    ang = pos[:, None] * inv_freq[None, :]
    cos = jnp.tile(jnp.cos(ang), (1, LANES // half))
    sin = jnp.tile(jnp.concatenate([-jnp.sin(ang), jnp.sin(ang)], axis=1), (1, LANES // B_QK_DIM))
    return cos, sin, dmask, qdec, kdec, sdec


def _retention(bqk, bv, bg, s0, gn_g, pos0):
    n, t, _ = bqk.shape
    bc = min(t, 256)
    tables = _retention_tables(t, bc, pos0)
    seq = lambda w: pl.BlockSpec((None, t, w), lambda b: (b, 0, 0))
    state = pl.BlockSpec((None, B_HEADS, B_QK_DIM, B_V_DIM), lambda b: (b, 0, 0, 0))
    return pl.pallas_call(
        functools.partial(_retention_kernel, t=t, bc=bc),
        grid=(n,),
        in_specs=[seq(2 * D_BQK), seq(D_B), seq(D_B)] + [_const_spec(tb.shape) for tb in tables]
                 + [state, _const_spec((1, D_B))],
        out_specs=[seq(D_B), state],
        out_shape=[jax.ShapeDtypeStruct((n, t, D_B), BF16),
                   jax.ShapeDtypeStruct((n, B_HEADS, B_QK_DIM, B_V_DIM), F32)],
        scratch_shapes=[pltpu.VMEM((B_HEADS, LANES, B_V_DIM), F32)],
        compiler_params=_params(("parallel",)),
        name="retention",
    )(bqk, bv, bg, *tables, s0, gn_g)


def _conv_kernel(*refs, t, tt, has_buf):
    if has_buf:
        cab_ref, buf_ref, w_ref, b_ref, lng_ref, lnb_ref, oc_ref, nc_ref, up = refs
    else:
        cab_ref, w_ref, b_ref, lng_ref, lnb_ref, oc_ref, nc_ref, up = refs
    hist = CONV_WIDTH - 1
    up[0:CONV_PAD, :] = jnp.zeros((CONV_PAD, C_CHANNELS), F32)
    if has_buf:
        up[CONV_PAD - hist:CONV_PAD, :] = buf_ref[...]

    def glu(i, carry):
        rows = pl.ds(pl.multiple_of(i * tt, tt), tt)
        ca = cab_ref[rows, 0:C_CHANNELS]
        cb = cab_ref[rows, C_CHANNELS:2 * C_CHANNELS]
        up[pl.ds(pl.multiple_of(CONV_PAD + i * tt, 8), tt), :] = ca * jax.nn.sigmoid(cb)
        return carry

    lax.fori_loop(0, t // tt, glu, 0)

    def tile(i, carry):
        t0 = pl.multiple_of(i * tt, tt)
        ext = up[pl.ds(t0, tt + CONV_PAD), :]
        acc = jnp.zeros((tt, C_CHANNELS), F32)
        for b in range(8):
            rb = ext if b == 0 else pltpu.roll(ext, b, axis=0)
            for a in range(CONV_PAD // 8):
                d = 8 * a + b
                if d > hist:
                    continue
                k = hist - d
                acc = acc + w_ref[k:k + 1, :] * rb[CONV_PAD - 8 * a:CONV_PAD - 8 * a + tt, :]
        y = acc + b_ref[...]
        mu = jnp.mean(y, axis=-1, keepdims=True)
        d0 = y - mu
        var = jnp.mean(d0 * d0, axis=-1, keepdims=True)
        yn = d0 * lax.rsqrt(var + EPS) * lng_ref[...] + lnb_ref[...]
        oc_ref[pl.ds(t0, tt), :] = (yn * jax.nn.sigmoid(yn)).astype(BF16)
        return carry

    lax.fori_loop(0, t // tt, tile, 0)
    nc_ref[...] = up[CONV_PAD + t - hist:CONV_PAD + t, :]


def _conv_module(cab, buf, conv_w, conv_b, ln_g, ln_b):
    n, t, _ = cab.shape
    tt = min(t, 128)
    has_buf = buf is not None
    hist = CONV_WIDTH - 1
    in_specs = [pl.BlockSpec((None, t, 2 * C_CHANNELS), lambda b: (b, 0, 0))]
    args = [cab]
    if has_buf:
        in_specs.append(pl.BlockSpec((None, hist, C_CHANNELS), lambda b: (b, 0, 0)))
        args.append(buf)
    in_specs += [_const_spec((CONV_WIDTH, C_CHANNELS))] + [_const_spec((1, C_CHANNELS))] * 3
    args += [conv_w, conv_b, ln_g, ln_b]
    return pl.pallas_call(
        functools.partial(_conv_kernel, t=t, tt=tt, has_buf=has_buf),
        grid=(n,),
        in_specs=in_specs,
        out_specs=[pl.BlockSpec((None, t, C_CHANNELS), lambda b: (b, 0, 0)),
                   pl.BlockSpec((None, hist, C_CHANNELS), lambda b: (b, 0, 0))],
        out_shape=[jax.ShapeDtypeStruct((n, t, C_CHANNELS), BF16),
                   jax.ShapeDtypeStruct((n, hist, C_CHANNELS), F32)],
        scratch_shapes=[pltpu.VMEM((CONV_PAD + t, C_CHANNELS), F32)],
        compiler_params=_params(("parallel",)),
        name="conv_module",
    )(*args)


def _mem_kv_kernel(mem_ref, wk_ref, wv_ref, mk_ref, mv_ref):
    m = mem_ref[...].astype(BF16)
    mk_ref[...] = _dot(m, wk_ref[...])
    mv_ref[...] = _dot(m, wv_ref[...])


def _mem_kv(mem2d, wk, wv):
    m = mem2d.shape[0]
    tm = 512
    row = pl.BlockSpec((tm, D_MODEL), lambda i: (i, 0))
    return pl.pallas_call(
        _mem_kv_kernel,
        grid=(m // tm,),
        in_specs=[row, _const_spec((D_MODEL, D_MODEL)), _const_spec((D_MODEL, D_MODEL))],
        out_specs=[row, row],
        out_shape=[jax.ShapeDtypeStruct((m, D_MODEL), F32)] * 2,
        compiler_params=_params(("parallel",)),
        name="mem_kv",
    )(mem2d, wk, wv)


def _route(logits):
    lane = lax.broadcasted_iota(jnp.int32, logits.shape, 1).astype(F32)
    valid = lane < N_EXPERTS
    lg = jnp.where(valid, logits, NEG_INF)
    e = jnp.exp(lg - jnp.max(lg, axis=-1, keepdims=True))
    probs = e / jnp.sum(e, axis=-1, keepdims=True)
    p1 = jnp.max(probs, axis=-1, keepdims=True)
    i1 = jnp.min(jnp.where(probs == p1, lane, float(LANES)), axis=-1, keepdims=True)
    rest = jnp.where(lane == i1, -1.0, probs)
    rest = jnp.where(valid, rest, -1.0)
    p2 = jnp.max(rest, axis=-1, keepdims=True)
    i2 = jnp.min(jnp.where(rest == p2, lane, float(LANES)), axis=-1, keepdims=True)
    tot = p1 + p2
    return jnp.where(lane == i1, p1 / tot, jnp.where(lane == i2, p2 / tot, 0.0))


def _post_mix_kernel(*refs, moe):
    if moe:
        (x_ref, oa_ref, ob_ref, oc_ref, wout_ref, gmem_ref, wq_ref, mk_ref, mv_ref, wo_ref, gffn_ref,
         router_ref, x2_ref, h_ref, comb_ref) = refs
    else:
        (x_ref, oa_ref, ob_ref, oc_ref, wout_ref, gmem_ref, wq_ref, mk_ref, mv_ref, wo_ref, gffn_ref,
         x2_ref, h_ref) = refs
    y = (_dot(oa_ref[...], wout_ref[0:D_A, :])
         + _dot(ob_ref[...], wout_ref[D_A:D_A + D_B, :])
         + _dot(oc_ref[...], wout_ref[D_A + D_B:D_MODEL, :]))
    x1 = x_ref[...] + y
    q = _dot(_rms(x1, gmem_ref[...]).astype(BF16), wq_ref[...]).astype(BF16)
    outs = []
    for h in range(X_HEADS):
        cols = slice(h * X_HEAD_DIM, (h + 1) * X_HEAD_DIM)
        s = _dot_nt(q[:, cols], mk_ref[:, cols].astype(BF16)) * (X_HEAD_DIM ** -0.5)
        e = jnp.exp(s - jnp.max(s, axis=-1, keepdims=True))
        l = jnp.sum(e, axis=-1, keepdims=True)
        outs.append((_dot(e.astype(BF16), mv_ref[:, cols].astype(BF16)) / l).astype(BF16))
    x2 = x1 + _dot(jnp.concatenate(outs, axis=1), wo_ref[...])
    x2_ref[...] = x2
    hf = _rms(x2, gffn_ref[...])
    h_ref[...] = hf.astype(BF16)
    if moe:
        logits = jnp.dot(hf, router_ref[...], preferred_element_type=F32, precision=lax.Precision.HIGHEST)
        comb_ref[...] = _route(logits)


def _post_mix(x, oa, ob, oc, w_out, g_mem, wq, mk, mv, wo, g_ffn, router):
    n, t, _ = x.shape
    tm = min(t, 512)
    moe = router is not None
    tile = lambda w: pl.BlockSpec((None, tm, w), lambda b, i: (b, i, 0))
    mem = pl.BlockSpec((None,) + mk.shape[1:], lambda b, i: (b, 0, 0))
    sq = _const_spec((D_MODEL, D_MODEL))
    vec = _const_spec((1, D_MODEL))
    in_specs = [tile(D_MODEL), tile(D_A), tile(D_B), tile(C_CHANNELS), sq, vec, sq, mem, mem, sq, vec]
    args = [x, oa, ob, oc, w_out, g_mem, wq, mk, mv, wo, g_ffn]
    out_specs = [tile(D_MODEL), tile(D_MODEL)]
    out_shape = [jax.ShapeDtypeStruct((n, t, D_MODEL), F32), jax.ShapeDtypeStruct((n, t, D_MODEL), BF16)]
    if moe:
        in_specs.append(_const_spec((D_MODEL, LANES)))
        args.append(router)
        out_specs.append(tile(LANES))
        out_shape.append(jax.ShapeDtypeStruct((n, t, LANES), F32))
    return pl.pallas_call(
        functools.partial(_post_mix_kernel, moe=moe),
        grid=(n, t // tm),
        in_specs=in_specs,
        out_specs=out_specs,
        out_shape=out_shape,
        compiler_params=_params(("parallel", "parallel")),
        name="post_mix",
    )(*args)


def _ffn_kernel(x_ref, h_ref, wg_ref, wu_ref, wd_ref, o_ref):
    h = h_ref[...]
    acc = x_ref[...]
    for c in range(D_FF // FF_CHUNK):
        cols = slice(c * FF_CHUNK, (c + 1) * FF_CHUNK)
        g = _dot(h, wg_ref[:, cols])
        u = _dot(h, wu_ref[:, cols])
        acc = acc + _dot((g * jax.nn.sigmoid(g) * u).astype(BF16), wd_ref[cols, :])
    o_ref[...] = acc


def _ffn(x2d, h2d, wg, wu, wd):
    m = x2d.shape[0]
    tm = min(m, 512)
    row = pl.BlockSpec((tm, D_MODEL), lambda i: (i, 0))
    return pl.pallas_call(
        _ffn_kernel,
        grid=(m // tm,),
        in_specs=[row, row, _const_spec((D_MODEL, D_FF)), _const_spec((D_MODEL, D_FF)),
                  _const_spec((D_FF, D_MODEL))],
        out_specs=row,
        out_shape=jax.ShapeDtypeStruct((m, D_MODEL), F32),
        compiler_params=_params(("parallel",)),
        name="ffn_dense",
    )(x2d, h2d, wg, wu, wd)


def _moe_kernel(x_ref, h_ref, comb_ref, wg_ref, wu_ref, wd_ref, gfin_ref, o_ref, acc):
    e = pl.program_id(1)

    @pl.when(e == 0)
    def _():
        acc[...] = x_ref[...]

    h = h_ref[...]
    g = _dot(h, wg_ref[...])
    u = _dot(h, wu_ref[...])
    y = _dot((g * jax.nn.sigmoid(g) * u).astype(BF16), wd_ref[...])
    comb = comb_ref[...]
    lane = lax.broadcasted_iota(jnp.int32, comb.shape, 1)
    ce = jnp.sum(jnp.where(lane == e, comb, 0.0), axis=-1, keepdims=True)
    acc[...] += ce * y

    @pl.when(e == N_EXPERTS - 1)
    def _():
        o_ref[...] = _rms(acc[...], gfin_ref[...])


def _moe(x2d, h2d, comb2d, wg, wu, wd, g_fin):
    m = x2d.shape[0]
    tm = min(m, 512)
    row = lambda w: pl.BlockSpec((tm, w), lambda i, e: (i, 0))
    return pl.pallas_call(
        _moe_kernel,
        grid=(m // tm, N_EXPERTS),
        in_specs=[row(D_MODEL), row(D_MODEL), row(LANES),
                  pl.BlockSpec((None, D_MODEL, D_FF_EXPERT), lambda i, e: (e, 0, 0)),
                  pl.BlockSpec((None, D_MODEL, D_FF_EXPERT), lambda i, e: (e, 0, 0)),
                  pl.BlockSpec((None, D_FF_EXPERT, D_MODEL), lambda i, e: (e, 0, 0)),
                  _const_spec((1, D_MODEL))],
        out_specs=row(D_MODEL),
        out_shape=jax.ShapeDtypeStruct((m, D_MODEL), F32),
        scratch_shapes=[pltpu.VMEM((tm, D_MODEL), F32)],
        compiler_params=_params(("parallel", "arbitrary")),
        name="moe_experts",
    )(x2d, h2d, comb2d, wg, wu, wd, g_fin)


def _layer(x, pos0, attn_hist, ret_state, conv_state, mk, mv, w):
    n, t, _ = x.shape
    aq, akv, bqk, bv, bg, cab = _in_proj(x.reshape(n * t, D_MODEL), w["norm_mix_g"], w["w_in"])
    r3 = lambda a: a.reshape(n, t, a.shape[-1])
    aq, akv, bqk, bv, bg, cab = map(r3, (aq, akv, bqk, bv, bg, cab))
    oa = _band_attn(aq, akv, attn_hist, w["rel_bias"])
    ob, new_s = _retention(bqk, bv, bg, ret_state, w["ret_gn_g"], pos0)
    oc, new_conv = _conv_module(cab, conv_state, w["conv_w"], w["conv_b"], w["conv_ln_g"], w["conv_ln_b"])
    post = _post_mix(x, oa, ob, oc, w["w_out"], w["norm_mem_g"], w["wx_q"], mk, mv, w["wx_o"],
                     w["norm_ffn_g"], w.get("router"))
    x2, h = post[0].reshape(n * t, D_MODEL), post[1].reshape(n * t, D_MODEL)
    if "router" in w:
        x3 = _moe(x2, h, post[2].reshape(n * t, LANES), w["moe_g"], w["moe_u"], w["moe_d"], w["final_g"])
    else:
        x3 = _ffn(x2, h, w["ffn_g"], w["ffn_u"], w["ffn_d"])
    keep = min(A_REACH, t)
    new_kv = akv[:, t - keep:, :]
    new_k = new_kv[..., :D_A].reshape(n, keep, A_HEADS, A_HEAD_DIM)
    new_v = new_kv[..., D_A:].reshape(n, keep, A_HEADS, A_HEAD_DIM)
    return x3.reshape(n, t, D_MODEL), new_k, new_v, new_s, new_conv


def kernel(x_prompt, x_sample, cache_attn_k, cache_attn_v, state_ret, state_conv, cache_mem_k, cache_mem_v,
           mem_prompt, norm_mix_g, w_in, rel_bias, ret_gn_g, conv_w, conv_b, conv_ln_g, conv_ln_b, w_out,
           norm_mem_g, wx_q, wx_k, wx_v, wx_o, norm_ffn_g, ffn_w_gate, ffn_w_up, ffn_w_down,
           router_w, moe_w_gate, moe_w_up, moe_w_down, final_norm_g):
    depth = w_in.shape[0]
    assert depth == 2, "layer 0 dense FFN, layer 1 experts + closing norm"
    n_p, _, _ = x_prompt.shape
    n_s, t_s, _ = x_sample.shape
    n_mem = mem_prompt.shape[1]
    row = lambda a: a.reshape(1, -1).astype(F32)
    xp, xs = x_prompt, x_sample
    mem2d = mem_prompt.reshape(n_p * n_mem, D_MODEL)
    outs_p = [[] for _ in range(6)]
    outs_s = [[] for _ in range(4)]
    for l in range(depth):
        w = dict(norm_mix_g=row(norm_mix_g[l]), w_in=w_in[l].astype(BF16), rel_bias=rel_bias[l],
                 ret_gn_g=row(ret_gn_g[l]), conv_w=conv_w[l].astype(F32), conv_b=row(conv_b[l]),
                 conv_ln_g=row(conv_ln_g[l]), conv_ln_b=row(conv_ln_b[l]), w_out=w_out[l].astype(BF16),
                 norm_mem_g=row(norm_mem_g[l]), wx_q=wx_q[l].astype(BF16), wx_o=wx_o[l].astype(BF16),
                 norm_ffn_g=row(norm_ffn_g[l]))
        if l % 2 == 0:
            i = l // 2
            w.update(ffn_g=ffn_w_gate[i].astype(BF16), ffn_u=ffn_w_up[i].astype(BF16),
                     ffn_d=ffn_w_down[i].astype(BF16))
        else:
            i = l // 2
            w.update(router=jnp.pad(router_w[i].astype(F32), ((0, 0), (0, LANES - N_EXPERTS))),
                     moe_g=moe_w_gate[i].astype(BF16), moe_u=moe_w_up[i].astype(BF16),
                     moe_d=moe_w_down[i].astype(BF16),
                     final_g=row(final_norm_g) if l == depth - 1 else None)
        mk, mv = _mem_kv(mem2d, wx_k[l].astype(BF16), wx_v[l].astype(BF16))
        mk = mk.reshape(n_p, n_mem, D_MODEL)
        mv = mv.reshape(n_p, n_mem, D_MODEL)
        xp, k_new, v_new, s_new, c_new = _layer(
            xp, 0.0, None, jnp.zeros((n_p, B_HEADS, B_QK_DIM, B_V_DIM), F32), None, mk, mv, w)
        for lst, a in zip(outs_p, (k_new, v_new, s_new, c_new,
                                   mk.reshape(n_p, n_mem, X_HEADS, X_HEAD_DIM),
                                   mv.reshape(n_p, n_mem, X_HEADS, X_HEAD_DIM))):
            lst.append(a)
        hist = (cache_attn_k[l].reshape(n_s, A_REACH, D_A), cache_attn_v[l].reshape(n_s, A_REACH, D_A))
        xs, k_new, v_new, s_new, c_new = _layer(
            xs, float(PAST_LEN), hist, state_ret[l], state_conv[l],
            cache_mem_k[l].reshape(n_s, n_mem, D_MODEL), cache_mem_v[l].reshape(n_s, n_mem, D_MODEL), w)
        for lst, a in zip(outs_s, (k_new, v_new, s_new, c_new)):
            lst.append(a)
    st = lambda lst: jnp.stack(lst)
    return (xp, xs, st(outs_p[0]), st(outs_p[1]), st(outs_p[2]), st(outs_p[3]), st(outs_p[4]), st(outs_p[5]),
            st(outs_s[0]), st(outs_s[1]), st(outs_s[2]), st(outs_s[3]))
```

```python
import functools

import numpy as np
import jax
import jax.numpy as jnp
from jax import lax
from jax.experimental import pallas as pl
from jax.experimental.pallas import tpu as pltpu

F32 = jnp.float32
BF16 = jnp.bfloat16

D_MODEL = 1024
PAST_LEN = 2048
CHUNK = 64
EPS = 1e-6
NEG_INF = -1e30
LANES = 128

A_HEADS = 4
A_HEAD_DIM = 64
D_A = A_HEADS * A_HEAD_DIM
A_LEFT_CHUNKS = 8
A_REACH = A_LEFT_CHUNKS * CHUNK
REL_CLIP = 128
A_SCALE = A_HEAD_DIM ** -0.5

B_HEADS = 4
B_QK_DIM = 64
B_V_DIM = 128
D_BQK = B_HEADS * B_QK_DIM
D_B = B_HEADS * B_V_DIM
RET_GAMMA_EXP0 = 5.0
ROPE_BASE = 10000.0

C_CHANNELS = 256
CONV_WIDTH = 31
CONV_PAD = 32

X_HEADS = 4
X_HEAD_DIM = D_MODEL // X_HEADS

D_FF = 11 * D_MODEL // 4
FF_CHUNK = 256
N_EXPERTS = 8
TOP_K = 2
D_FF_EXPERT = D_FF // 2

COL_QA, COL_KVA, COL_QKB, COL_VB, COL_GB, COL_CAB, D_IN = 0, 256, 768, 1280, 1792, 2304, 2816

VMEM_LIMIT = 56 * 1024 * 1024


def _params(sem):
    return pltpu.CompilerParams(dimension_semantics=sem, vmem_limit_bytes=VMEM_LIMIT)


def _rms(x, g):
    return x * lax.rsqrt(jnp.mean(x * x, axis=-1, keepdims=True) + EPS) * g


def _dot(a, b):
    return jnp.dot(a, b, preferred_element_type=F32)


def _dot_nt(a, b):
    return lax.dot_general(a, b, (((1,), (1,)), ((), ())), preferred_element_type=F32)


def _dot_tn(a, b):
    return lax.dot_general(a, b, (((0,), (0,)), ((), ())), preferred_element_type=F32)


def _const_spec(shape):
    return pl.BlockSpec(shape, lambda *_: (0,) * len(shape))


def _in_proj_kernel(x_ref, g_ref, w_ref, aq_ref, akv_ref, bqk_ref, bv_ref, bg_ref, cab_ref):
    h = _rms(x_ref[...], g_ref[...]).astype(BF16)
    aq_ref[...] = _dot(h, w_ref[:, COL_QA:COL_KVA]).astype(BF16)
    akv_ref[...] = _dot(h, w_ref[:, COL_KVA:COL_QKB])
    bqk_ref[...] = _dot(h, w_ref[:, COL_QKB:COL_VB]).astype(BF16)
    bv_ref[...] = _dot(h, w_ref[:, COL_VB:COL_GB]).astype(BF16)
    bg_ref[...] = _dot(h, w_ref[:, COL_GB:COL_CAB]).astype(BF16)
    cab_ref[...] = _dot(h, w_ref[:, COL_CAB:D_IN])


def _in_proj(x2d, g, w_in):
    m = x2d.shape[0]
    tm = min(m, 512)
    widths = ((256, BF16), (512, F32), (512, BF16), (512, BF16), (512, BF16), (512, F32))
    return pl.pallas_call(
        _in_proj_kernel,
        grid=(m // tm,),
        in_specs=[pl.BlockSpec((tm, D_MODEL), lambda i: (i, 0)),
                  _const_spec((1, D_MODEL)),
                  _const_spec((D_MODEL, D_IN))],
        out_specs=[pl.BlockSpec((tm, w), lambda i: (i, 0)) for w, _ in widths],
        out_shape=[jax.ShapeDtypeStruct((m, w), dt) for w, dt in widths],
        compiler_params=_params(("parallel",)),
        name="in_proj",
    )(x2d, g, w_in)


def _band_attn_kernel(*refs, t, tq, has_hist):
    if has_hist:
        aq_ref, akv_ref, hk_ref, hv_ref, bias_ref, o_ref, kc, vc = refs
    else:
        aq_ref, akv_ref, bias_ref, o_ref, kc, vc = refs
    j = pl.program_id(1)
    span = A_REACH + tq

    @pl.when(j == 0)
    def _():
        if has_hist:
            kc[0:A_REACH, :] = hk_ref[...].astype(BF16)
            vc[0:A_REACH, :] = hv_ref[...].astype(BF16)
        else:
            kc[0:A_REACH, :] = jnp.zeros((A_REACH, D_A), BF16)
            vc[0:A_REACH, :] = jnp.zeros((A_REACH, D_A), BF16)
        kc[A_REACH:A_REACH + t, :] = akv_ref[:, 0:D_A].astype(BF16)
        vc[A_REACH:A_REACH + t, :] = akv_ref[:, D_A:2 * D_A].astype(BF16)

    t0 = pl.multiple_of(j * tq, tq)
    q = aq_ref[...]
    lane = lax.broadcasted_iota(jnp.int32, (tq, LANES), 1)
    if not has_hist:
        col = lax.broadcasted_iota(jnp.int32, (tq, span), 1)
        k_ok = col >= A_REACH - t0
    outs = []
    for p in range(A_HEADS // 2):
        qp = q[:, p * LANES:(p + 1) * LANES]
        kp = kc[pl.ds(t0, span), p * LANES:(p + 1) * LANES]
        vp = vc[pl.ds(t0, span), p * LANES:(p + 1) * LANES]
        o_pair = None
        for hh in range(2):
            own = (lane < A_HEAD_DIM) if hh == 0 else (lane >= A_HEAD_DIM)
            qm = jnp.where(own, qp.astype(F32) * A_SCALE, 0.0).astype(BF16)
            s = _dot_nt(qm, kp) + bias_ref[2 * p + hh]
            if not has_hist:
                s = jnp.where(k_ok, s, NEG_INF)
            m = jnp.max(s, axis=-1, keepdims=True)
            e = jnp.exp(s - m)
            l = jnp.sum(e, axis=-1, keepdims=True)
            o = _dot(e.astype(BF16), vp) / l
            o_pair = o if hh == 0 else jnp.where(own, o, o_pair)
        outs.append(o_pair)
    o_ref[...] = jnp.concatenate(outs, axis=1).astype(BF16)


def _band_bias(rel_bias_l, tq):
    span = A_REACH + tq
    period = span + tq
    n_far = A_REACH - REL_CLIP + 1
    far = rel_bias_l[:, 2 * REL_CLIP:]
    n_near = min(2 * REL_CLIP, span - n_far)
    near = jnp.flip(rel_bias_l[:, :2 * REL_CLIP], axis=1)[:, :n_near]
    beyond = jnp.broadcast_to(rel_bias_l[:, :1], (A_HEADS, span - n_far - n_near))
    u = jnp.concatenate([jnp.broadcast_to(far, (A_HEADS, n_far)), near, beyond,
                         jnp.broadcast_to(far, (A_HEADS, tq))], axis=1).astype(F32)
    skew = jnp.broadcast_to(u[:, None, :], (A_HEADS, tq, period)).reshape(A_HEADS, tq * period)
    toeplitz = skew[:, :tq * (period - 1)].reshape(A_HEADS, tq, period - 1)[:, :, :span]
    i = np.arange(tq)[:, None]
    j = np.arange(span)[None, :]
    in_band = (j // CHUNK >= i // CHUNK) & (j // CHUNK <= i // CHUNK + A_LEFT_CHUNKS)
    return jnp.where(in_band[None], toeplitz, NEG_INF)


def _band_attn(aq, akv, hist, rel_bias_l):
    n, t, _ = aq.shape
    tq = min(t, 256)
    span = A_REACH + tq
    bias = _band_bias(rel_bias_l, tq)
    has_hist = hist is not None
    in_specs = [pl.BlockSpec((None, tq, D_A), lambda b, j: (b, j, 0)),
                pl.BlockSpec((None, t, 2 * D_A), lambda b, j: (b, 0, 0))]
    args = [aq, akv]
    if has_hist:
        in_specs += [pl.BlockSpec((None, A_REACH, D_A), lambda b, j: (b, 0, 0))] * 2
        args += list(hist)
    in_specs.append(_const_spec((A_HEADS, tq, span)))
    args.append(bias)
    return pl.pallas_call(
        functools.partial(_band_attn_kernel, t=t, tq=tq, has_hist=has_hist),
        grid=(n, t // tq),
        in_specs=in_specs,
        out_specs=pl.BlockSpec((None, tq, D_A), lambda b, j: (b, j, 0)),
        out_shape=jax.ShapeDtypeStruct((n, t, D_A), BF16),
        scratch_shapes=[pltpu.VMEM((A_REACH + t, D_A), BF16)] * 2,
        compiler_params=_params(("parallel", "arbitrary")),
        name="band_attn",
    )(*args)


def _swap_halves(x):
    lane = lax.broadcasted_iota(jnp.int32, x.shape, 1)
    first = (lane % B_QK_DIM) < (B_QK_DIM // 2)
    return jnp.where(first, pltpu.roll(x, LANES - B_QK_DIM // 2, axis=1), pltpu.roll(x, B_QK_DIM // 2, axis=1))


def _retention_kernel(bqk_ref, bv_ref, bg_ref, cos_ref, sin_ref, dmask_ref, qdec_ref, kdec_ref, sdec_ref,
                      s0_ref, gn_ref, ob_ref, sfin_ref, st, *, t, bc):
    for h in range(B_HEADS):
        off = (h % 2) * B_QK_DIM
        st[h] = jnp.zeros((LANES, B_V_DIM), F32)
        st[h, off:off + B_QK_DIM, :] = s0_ref[h]

    lane = lax.broadcasted_iota(jnp.int32, (bc, LANES), 1)

    def chunk(c, carry):
        r0 = pl.multiple_of(c * bc, bc)
        rows = pl.ds(r0, bc)
        cs = cos_ref[rows, :]
        sn = sin_ref[rows, :]
        for p in range(B_HEADS // 2):
            qx = bqk_ref[rows, p * LANES:(p + 1) * LANES].astype(F32)
            kx = bqk_ref[rows, D_BQK + p * LANES:D_BQK + (p + 1) * LANES].astype(F32)
            qr = qx * cs + _swap_halves(qx) * sn
            kr = (kx * cs + _swap_halves(kx) * sn) * (B_QK_DIM ** -0.5)
            for hh in range(2):
                h = 2 * p + hh
                own = (lane < B_QK_DIM) if hh == 0 else (lane >= B_QK_DIM)
                qh = jnp.where(own, qr, 0.0)
                kh = jnp.where(own, kr, 0.0)
                v = bv_ref[rows, h * B_V_DIM:(h + 1) * B_V_DIM]
                att = _dot_nt(qh.astype(BF16), kh.astype(BF16)) * dmask_ref[h]
                o = (_dot(att.astype(BF16), v)
                     + _dot((qh * qdec_ref[h]).astype(BF16), st[h].astype(BF16)))
                st[h] = st[h] * sdec_ref[h] + _dot_tn((kh * kdec_ref[h]).astype(BF16), v)
                mu = jnp.mean(o, axis=-1, keepdims=True)
                d = o - mu
                var = jnp.mean(d * d, axis=-1, keepdims=True)
                on = d * lax.rsqrt(var + EPS) * gn_ref[:, h * B_V_DIM:(h + 1) * B_V_DIM]
                g = bg_ref[rows, h * B_V_DIM:(h + 1) * B_V_DIM].astype(F32)
                ob_ref[rows, h * B_V_DIM:(h + 1) * B_V_DIM] = (g * jax.nn.sigmoid(g) * on).astype(BF16)
        return carry

    lax.fori_loop(0, t // bc, chunk, 0)
    for h in range(B_HEADS):
        off = (h % 2) * B_QK_DIM
        sfin_ref[h] = st[h, off:off + B_QK_DIM, :]


def _retention_tables(t, bc, pos0):
    log_g = jnp.log(1.0 - 2.0 ** (-(RET_GAMMA_EXP0 + jnp.arange(B_HEADS, dtype=F32))))
    i = jnp.arange(bc, dtype=F32)
    diff = i[:, None] - i[None, :]
    dmask = jnp.where(diff[None] >= 0, jnp.exp(jnp.maximum(diff, 0.0)[None] * log_g[:, None, None]), 0.0)
    qdec = jnp.exp((i[None, :] + 1.0) * log_g[:, None])
    kdec = jnp.exp((bc - 1.0 - i)[None, :] * log_g[:, None])
    sdec = jnp.exp(bc * log_g)
    qdec = jnp.broadcast_to(qdec[:, :, None], (B_HEADS, bc, LANES))
    kdec = jnp.broadcast_to(kdec[:, :, None], (B_HEADS, bc, LANES))
    sdec = jnp.broadcast_to(sdec[:, None, None], (B_HEADS, 1, B_V_DIM))
    half = B_QK_DIM // 2
    pos = pos0 + jnp.arange(t, dtype=F32)
    inv_freq = ROPE_BASE ** (-jnp.arange(half, dtype=F32) / half)
    ang = pos[:, None] * inv_freq[None, :]
    cos = jnp.tile(jnp.cos(ang), (1, LANES // half))
    sin = jnp.tile(jnp.concatenate([-jnp.sin(ang), jnp.sin(ang)], axis=1), (1, LANES // B_QK_DIM))
    return cos, sin, dmask, qdec, kdec, sdec


def _retention(bqk, bv, bg, s0, gn_g, pos0):
    n, t, _ = bqk.shape
    bc = min(t, 256)
    tables = _retention_tables(t, bc, pos0)
    seq = lambda w: pl.BlockSpec((None, t, w), lambda b: (b, 0, 0))
    state = pl.BlockSpec((None, B_HEADS, B_QK_DIM, B_V_DIM), lambda b: (b, 0, 0, 0))
    return pl.pallas_call(
        functools.partial(_retention_kernel, t=t, bc=bc),
        grid=(n,),
        in_specs=[seq(2 * D_BQK), seq(D_B), seq(D_B)] + [_const_spec(tb.shape) for tb in tables]
                 + [state, _const_spec((1, D_B))],
        out_specs=[seq(D_B), state],
        out_shape=[jax.ShapeDtypeStruct((n, t, D_B), BF16),
                   jax.ShapeDtypeStruct((n, B_HEADS, B_QK_DIM, B_V_DIM), F32)],
        scratch_shapes=[pltpu.VMEM((B_HEADS, LANES, B_V_DIM), F32)],
        compiler_params=_params(("parallel",)),
        name="retention",
    )(bqk, bv, bg, *tables, s0, gn_g)


def _conv_kernel(*refs, t, tt, has_buf):
    if has_buf:
        cab_ref, buf_ref, w_ref, b_ref, lng_ref, lnb_ref, oc_ref, nc_ref, up = refs
    else:
        cab_ref, w_ref, b_ref, lng_ref, lnb_ref, oc_ref, nc_ref, up = refs
    hist = CONV_WIDTH - 1
    up[0:CONV_PAD, :] = jnp.zeros((CONV_PAD, C_CHANNELS), F32)
    if has_buf:
        up[CONV_PAD - hist:CONV_PAD, :] = buf_ref[...]

    def glu(i, carry):
        rows = pl.ds(pl.multiple_of(i * tt, tt), tt)
        ca = cab_ref[rows, 0:C_CHANNELS]
        cb = cab_ref[rows, C_CHANNELS:2 * C_CHANNELS]
        up[pl.ds(pl.multiple_of(CONV_PAD + i * tt, 8), tt), :] = ca * jax.nn.sigmoid(cb)
        return carry

    lax.fori_loop(0, t // tt, glu, 0)

    def tile(i, carry):
        t0 = pl.multiple_of(i * tt, tt)
        ext = up[pl.ds(t0, tt + CONV_PAD), :]
        acc = jnp.zeros((tt, C_CHANNELS), F32)
        for b in range(8):
            rb = ext if b == 0 else pltpu.roll(ext, b, axis=0)
            for a in range(CONV_PAD // 8):
                d = 8 * a + b
                if d > hist:
                    continue
                k = hist - d
                acc = acc + w_ref[k:k + 1, :] * rb[CONV_PAD - 8 * a:CONV_PAD - 8 * a + tt, :]
        y = acc + b_ref[...]
        mu = jnp.mean(y, axis=-1, keepdims=True)
        d0 = y - mu
        var = jnp.mean(d0 * d0, axis=-1, keepdims=True)
        yn = d0 * lax.rsqrt(var + EPS) * lng_ref[...] + lnb_ref[...]
        oc_ref[pl.ds(t0, tt), :] = (yn * jax.nn.sigmoid(yn)).astype(BF16)
        return carry

    lax.fori_loop(0, t // tt, tile, 0)
    nc_ref[...] = up[CONV_PAD + t - hist:CONV_PAD + t, :]


def _conv_module(cab, buf, conv_w, conv_b, ln_g, ln_b):
    n, t, _ = cab.shape
    tt = min(t, 128)
    has_buf = buf is not None
    hist = CONV_WIDTH - 1
    in_specs = [pl.BlockSpec((None, t, 2 * C_CHANNELS), lambda b: (b, 0, 0))]
    args = [cab]
    if has_buf:
        in_specs.append(pl.BlockSpec((None, hist, C_CHANNELS), lambda b: (b, 0, 0)))
        args.append(buf)
    in_specs += [_const_spec((CONV_WIDTH, C_CHANNELS))] + [_const_spec((1, C_CHANNELS))] * 3
    args += [conv_w, conv_b, ln_g, ln_b]
    return pl.pallas_call(
        functools.partial(_conv_kernel, t=t, tt=tt, has_buf=has_buf),
        grid=(n,),
        in_specs=in_specs,
        out_specs=[pl.BlockSpec((None, t, C_CHANNELS), lambda b: (b, 0, 0)),
                   pl.BlockSpec((None, hist, C_CHANNELS), lambda b: (b, 0, 0))],
        out_shape=[jax.ShapeDtypeStruct((n, t, C_CHANNELS), BF16),
                   jax.ShapeDtypeStruct((n, hist, C_CHANNELS), F32)],
        scratch_shapes=[pltpu.VMEM((CONV_PAD + t, C_CHANNELS), F32)],
        compiler_params=_params(("parallel",)),
        name="conv_module",
    )(*args)


def _mem_kv_kernel(mem_ref, wk_ref, wv_ref, mk_ref, mv_ref):
    m = mem_ref[...].astype(BF16)
    mk_ref[...] = _dot(m, wk_ref[...])
    mv_ref[...] = _dot(m, wv_ref[...])


def _mem_kv(mem2d, wk, wv):
    m = mem2d.shape[0]
    tm = 512
    row = pl.BlockSpec((tm, D_MODEL), lambda i: (i, 0))
    return pl.pallas_call(
        _mem_kv_kernel,
        grid=(m // tm,),
        in_specs=[row, _const_spec((D_MODEL, D_MODEL)), _const_spec((D_MODEL, D_MODEL))],
        out_specs=[row, row],
        out_shape=[jax.ShapeDtypeStruct((m, D_MODEL), F32)] * 2,
        compiler_params=_params(("parallel",)),
        name="mem_kv",
    )(mem2d, wk, wv)


def _route(logits):
    lane = lax.broadcasted_iota(jnp.int32, logits.shape, 1).astype(F32)
    valid = lane < N_EXPERTS
    lg = jnp.where(valid, logits, NEG_INF)
    e = jnp.exp(lg - jnp.max(lg, axis=-1, keepdims=True))
    probs = e / jnp.sum(e, axis=-1, keepdims=True)
    p1 = jnp.max(probs, axis=-1, keepdims=True)
    i1 = jnp.min(jnp.where(probs == p1, lane, float(LANES)), axis=-1, keepdims=True)
    rest = jnp.where(lane == i1, -1.0, probs)
    rest = jnp.where(valid, rest, -1.0)
    p2 = jnp.max(rest, axis=-1, keepdims=True)
    i2 = jnp.min(jnp.where(rest == p2, lane, float(LANES)), axis=-1, keepdims=True)
    tot = p1 + p2
    return jnp.where(lane == i1, p1 / tot, jnp.where(lane == i2, p2 / tot, 0.0))


def _post_mix_kernel(*refs, moe):
    if moe:
        (x_ref, oa_ref, ob_ref, oc_ref, wout_ref, gmem_ref, wq_ref, mk_ref, mv_ref, wo_ref, gffn_ref,
         router_ref, x2_ref, h_ref, comb_ref) = refs
    else:
        (x_ref, oa_ref, ob_ref, oc_ref, wout_ref, gmem_ref, wq_ref, mk_ref, mv_ref, wo_ref, gffn_ref,
         x2_ref, h_ref) = refs
    y = (_dot(oa_ref[...], wout_ref[0:D_A, :])
         + _dot(ob_ref[...], wout_ref[D_A:D_A + D_B, :])
         + _dot(oc_ref[...], wout_ref[D_A + D_B:D_MODEL, :]))
    x1 = x_ref[...] + y
    q = _dot(_rms(x1, gmem_ref[...]).astype(BF16), wq_ref[...]).astype(BF16)
    outs = []
    for h in range(X_HEADS):
        cols = slice(h * X_HEAD_DIM, (h + 1) * X_HEAD_DIM)
        s = _dot_nt(q[:, cols], mk_ref[:, cols].astype(BF16)) * (X_HEAD_DIM ** -0.5)
        e = jnp.exp(s - jnp.max(s, axis=-1, keepdims=True))
        l = jnp.sum(e, axis=-1, keepdims=True)
        outs.append((_dot(e.astype(BF16), mv_ref[:, cols].astype(BF16)) / l).astype(BF16))
    x2 = x1 + _dot(jnp.concatenate(outs, axis=1), wo_ref[...])
    x2_ref[...] = x2
    hf = _rms(x2, gffn_ref[...])
    h_hi = hf.astype(BF16)
    h_ref[...] = h_hi
    if moe:
        h_lo = (hf - h_hi.astype(F32)).astype(BF16)
        logits = (_dot(h_hi, router_ref[0]) + _dot(h_lo, router_ref[0])) + _dot(h_hi, router_ref[1])
        comb_ref[...] = _route(logits)


def _post_mix(x, oa, ob, oc, w_out, g_mem, wq, mk, mv, wo, g_ffn, router):
    n, t, _ = x.shape
    tm = min(t, 512)
    moe = router is not None
    tile = lambda w: pl.BlockSpec((None, tm, w), lambda b, i: (b, i, 0))
    mem = pl.BlockSpec((None,) + mk.shape[1:], lambda b, i: (b, 0, 0))
    sq = _const_spec((D_MODEL, D_MODEL))
    vec = _const_spec((1, D_MODEL))
    in_specs = [tile(D_MODEL), tile(D_A), tile(D_B), tile(C_CHANNELS), sq, vec, sq, mem, mem, sq, vec]
    args = [x, oa, ob, oc, w_out, g_mem, wq, mk, mv, wo, g_ffn]
    out_specs = [tile(D_MODEL), tile(D_MODEL)]
    out_shape = [jax.ShapeDtypeStruct((n, t, D_MODEL), F32), jax.ShapeDtypeStruct((n, t, D_MODEL), BF16)]
    if moe:
        in_specs.append(_const_spec((2, D_MODEL, LANES)))
        args.append(router)
        out_specs.append(tile(LANES))
        out_shape.append(jax.ShapeDtypeStruct((n, t, LANES), F32))
    return pl.pallas_call(
        functools.partial(_post_mix_kernel, moe=moe),
        grid=(n, t // tm),
        in_specs=in_specs,
        out_specs=out_specs,
        out_shape=out_shape,
        compiler_params=_params(("parallel", "parallel")),
        name="post_mix",
    )(*args)


def _ffn_kernel(x_ref, h_ref, wg_ref, wu_ref, wd_ref, o_ref):
    h = h_ref[...]
    acc = x_ref[...]
    for c in range(D_FF // FF_CHUNK):
        cols = slice(c * FF_CHUNK, (c + 1) * FF_CHUNK)
        g = _dot(h, wg_ref[:, cols])
        u = _dot(h, wu_ref[:, cols])
        acc = acc + _dot((g * jax.nn.sigmoid(g) * u).astype(BF16), wd_ref[cols, :])
    o_ref[...] = acc


def _ffn(x2d, h2d, wg, wu, wd):
    m = x2d.shape[0]
    tm = min(m, 512)
    row = pl.BlockSpec((tm, D_MODEL), lambda i: (i, 0))
    return pl.pallas_call(
        _ffn_kernel,
        grid=(m // tm,),
        in_specs=[row, row, _const_spec((D_MODEL, D_FF)), _const_spec((D_MODEL, D_FF)),
                  _const_spec((D_FF, D_MODEL))],
        out_specs=row,
        out_shape=jax.ShapeDtypeStruct((m, D_MODEL), F32),
        compiler_params=_params(("parallel",)),
        name="ffn_dense",
    )(x2d, h2d, wg, wu, wd)


def _moe_kernel(x_ref, h_ref, comb_ref, wg_ref, wu_ref, wd_ref, gfin_ref, o_ref, acc):
    e = pl.program_id(1)

    @pl.when(e == 0)
    def _():
        acc[...] = x_ref[...]

    h = h_ref[...]
    g = _dot(h, wg_ref[...])
    u = _dot(h, wu_ref[...])
    y = _dot((g * jax.nn.sigmoid(g) * u).astype(BF16), wd_ref[...])
    comb = comb_ref[...]
    lane = lax.broadcasted_iota(jnp.int32, comb.shape, 1)
    ce = jnp.sum(jnp.where(lane == e, comb, 0.0), axis=-1, keepdims=True)
    acc[...] += ce * y

    @pl.when(e == N_EXPERTS - 1)
    def _():
        o_ref[...] = _rms(acc[...], gfin_ref[...])


def _moe(x2d, h2d, comb2d, wg, wu, wd, g_fin):
    m = x2d.shape[0]
    tm = min(m, 512)
    row = lambda w: pl.BlockSpec((tm, w), lambda i, e: (i, 0))
    return pl.pallas_call(
        _moe_kernel,
        grid=(m // tm, N_EXPERTS),
        in_specs=[row(D_MODEL), row(D_MODEL), row(LANES),
                  pl.BlockSpec((None, D_MODEL, D_FF_EXPERT), lambda i, e: (e, 0, 0)),
                  pl.BlockSpec((None, D_MODEL, D_FF_EXPERT), lambda i, e: (e, 0, 0)),
                  pl.BlockSpec((None, D_FF_EXPERT, D_MODEL), lambda i, e: (e, 0, 0)),
                  _const_spec((1, D_MODEL))],
        out_specs=row(D_MODEL),
        out_shape=jax.ShapeDtypeStruct((m, D_MODEL), F32),
        scratch_shapes=[pltpu.VMEM((tm, D_MODEL), F32)],
        compiler_params=_params(("parallel", "arbitrary")),
        name="moe_experts",
    )(x2d, h2d, comb2d, wg, wu, wd, g_fin)


def _layer(x, pos0, attn_hist, ret_state, conv_state, mk, mv, w):
    n, t, _ = x.shape
    aq, akv, bqk, bv, bg, cab = _in_proj(x.reshape(n * t, D_MODEL), w["norm_mix_g"], w["w_in"])
    r3 = lambda a: a.reshape(n, t, a.shape[-1])
    aq, akv, bqk, bv, bg, cab = map(r3, (aq, akv, bqk, bv, bg, cab))
    oa = _band_attn(aq, akv, attn_hist, w["rel_bias"])
    ob, new_s = _retention(bqk, bv, bg, ret_state, w["ret_gn_g"], pos0)
    oc, new_conv = _conv_module(cab, conv_state, w["conv_w"], w["conv_b"], w["conv_ln_g"], w["conv_ln_b"])
    post = _post_mix(x, oa, ob, oc, w["w_out"], w["norm_mem_g"], w["wx_q"], mk, mv, w["wx_o"],
                     w["norm_ffn_g"], w.get("router"))
    x2, h = post[0].reshape(n * t, D_MODEL), post[1].reshape(n * t, D_MODEL)
    if "router" in w:
        x3 = _moe(x2, h, post[2].reshape(n * t, LANES), w["moe_g"], w["moe_u"], w["moe_d"], w["final_g"])
    else:
        x3 = _ffn(x2, h, w["ffn_g"], w["ffn_u"], w["ffn_d"])
    keep = min(A_REACH, t)
    new_kv = akv[:, t - keep:, :]
    new_k = new_kv[..., :D_A].reshape(n, keep, A_HEADS, A_HEAD_DIM)
    new_v = new_kv[..., D_A:].reshape(n, keep, A_HEADS, A_HEAD_DIM)
    return x3.reshape(n, t, D_MODEL), new_k, new_v, new_s, new_conv


def kernel(x_prompt, x_sample, cache_attn_k, cache_attn_v, state_ret, state_conv, cache_mem_k, cache_mem_v,
           mem_prompt, norm_mix_g, w_in, rel_bias, ret_gn_g, conv_w, conv_b, conv_ln_g, conv_ln_b, w_out,
           norm_mem_g, wx_q, wx_k, wx_v, wx_o, norm_ffn_g, ffn_w_gate, ffn_w_up, ffn_w_down,
           router_w, moe_w_gate, moe_w_up, moe_w_down, final_norm_g):
    depth = w_in.shape[0]
    assert depth == 2, "layer 0 dense FFN, layer 1 experts + closing norm"
    n_p, _, _ = x_prompt.shape
    n_s, t_s, _ = x_sample.shape
    n_mem = mem_prompt.shape[1]
    row = lambda a: a.reshape(1, -1).astype(F32)
    xp, xs = x_prompt, x_sample
    mem2d = mem_prompt.reshape(n_p * n_mem, D_MODEL)
    outs_p = [[] for _ in range(6)]
    outs_s = [[] for _ in range(4)]
    for l in range(depth):
        w = dict(norm_mix_g=row(norm_mix_g[l]), w_in=w_in[l].astype(BF16), rel_bias=rel_bias[l],
                 ret_gn_g=row(ret_gn_g[l]), conv_w=conv_w[l].astype(F32), conv_b=row(conv_b[l]),
                 conv_ln_g=row(conv_ln_g[l]), conv_ln_b=row(conv_ln_b[l]), w_out=w_out[l].astype(BF16),
                 norm_mem_g=row(norm_mem_g[l]), wx_q=wx_q[l].astype(BF16), wx_o=wx_o[l].astype(BF16),
                 norm_ffn_g=row(norm_ffn_g[l]))
        if l % 2 == 0:
            i = l // 2
            w.update(ffn_g=ffn_w_gate[i].astype(BF16), ffn_u=ffn_w_up[i].astype(BF16),
                     ffn_d=ffn_w_down[i].astype(BF16))
        else:
            i = l // 2
            r = jnp.pad(router_w[i].astype(F32), ((0, 0), (0, LANES - N_EXPERTS)))
            r_hi = r.astype(BF16)
            r_lo = (r - r_hi.astype(F32)).astype(BF16)
            w.update(router=jnp.stack([r_hi, r_lo]),
                     moe_g=moe_w_gate[i].astype(BF16), moe_u=moe_w_up[i].astype(BF16),
                     moe_d=moe_w_down[i].astype(BF16),
                     final_g=row(final_norm_g) if l == depth - 1 else None)
        mk, mv = _mem_kv(mem2d, wx_k[l].astype(BF16), wx_v[l].astype(BF16))
        mk = mk.reshape(n_p, n_mem, D_MODEL)
        mv = mv.reshape(n_p, n_mem, D_MODEL)
        xp, k_new, v_new, s_new, c_new = _layer(
            xp, 0.0, None, jnp.zeros((n_p, B_HEADS, B_QK_DIM, B_V_DIM), F32), None, mk, mv, w)
        for lst, a in zip(outs_p, (k_new, v_new, s_new, c_new,
                                   mk.reshape(n_p, n_mem, X_HEADS, X_HEAD_DIM),
                                   mv.reshape(n_p, n_mem, X_HEADS, X_HEAD_DIM))):
            lst.append(a)
        hist = (cache_attn_k[l].reshape(n_s, A_REACH, D_A), cache_attn_v[l].reshape(n_s, A_REACH, D_A))
        xs, k_new, v_new, s_new, c_new = _layer(
            xs, float(PAST_LEN), hist, state_ret[l], state_conv[l],
            cache_mem_k[l].reshape(n_s, n_mem, D_MODEL), cache_mem_v[l].reshape(n_s, n_mem, D_MODEL), w)
        for lst, a in zip(outs_s, (k_new, v_new, s_new, c_new)):
            lst.append(a)
    st = lambda lst: jnp.stack(lst)
    return (xp, xs, st(outs_p[0]), st(outs_p[1]), st(outs_p[2]), st(outs_p[3]), st(outs_p[4]), st(outs_p[5]),
            st(outs_s[0]), st(outs_s[1]), st(outs_s[2]), st(outs_s[3]))
```

```python
import functools

import numpy as np
import jax
import jax.numpy as jnp
from jax import lax
from jax.experimental import pallas as pl
from jax.experimental.pallas import tpu as pltpu

F32 = jnp.float32
BF16 = jnp.bfloat16

D_MODEL = 1024
PAST_LEN = 2048
CHUNK = 64
EPS = 1e-6
NEG_INF = -1e30
LANES = 128

A_HEADS = 4
A_HEAD_DIM = 64
D_A = A_HEADS * A_HEAD_DIM
A_LEFT_CHUNKS = 8
A_REACH = A_LEFT_CHUNKS * CHUNK
REL_CLIP = 128
A_SCALE = A_HEAD_DIM ** -0.5

B_HEADS = 4
B_QK_DIM = 64
B_V_DIM = 128
D_BQK = B_HEADS * B_QK_DIM
D_B = B_HEADS * B_V_DIM
RET_GAMMA_EXP0 = 5.0
ROPE_BASE = 10000.0

C_CHANNELS = 256
CONV_WIDTH = 31
CONV_PAD = 32

X_HEADS = 4
X_HEAD_DIM = D_MODEL // X_HEADS

D_FF = 11 * D_MODEL // 4
FF_CHUNK = 256
N_EXPERTS = 8
TOP_K = 2
D_FF_EXPERT = D_FF // 2
TOKEN_TILE_ROWS = D_MODEL // LANES
MOE_BLOCK = 512
MOE_COMBINE_TILE = 512

COL_QA, COL_KVA, COL_QKB, COL_VB, COL_GB, COL_CAB, D_IN = 0, 256, 768, 1280, 1792, 2304, 2816

VMEM_LIMIT = 56 * 1024 * 1024


def _params(sem):
    return pltpu.CompilerParams(dimension_semantics=sem, vmem_limit_bytes=VMEM_LIMIT)


def _rms(x, g):
    return x * lax.rsqrt(jnp.mean(x * x, axis=-1, keepdims=True) + EPS) * g


def _dot(a, b):
    return jnp.dot(a, b, preferred_element_type=F32)


def _dot_nt(a, b):
    return lax.dot_general(a, b, (((1,), (1,)), ((), ())), preferred_element_type=F32)


def _dot_tn(a, b):
    return lax.dot_general(a, b, (((0,), (0,)), ((), ())), preferred_element_type=F32)


def _const_spec(shape):
    return pl.BlockSpec(shape, lambda *_: (0,) * len(shape))


def _in_proj_kernel(x_ref, g_ref, w_ref, aq_ref, akv_ref, bqk_ref, bv_ref, bg_ref, cab_ref):
    h = _rms(x_ref[...], g_ref[...]).astype(BF16)
    aq_ref[...] = _dot(h, w_ref[:, COL_QA:COL_KVA]).astype(BF16)
    akv_ref[...] = _dot(h, w_ref[:, COL_KVA:COL_QKB])
    bqk_ref[...] = _dot(h, w_ref[:, COL_QKB:COL_VB]).astype(BF16)
    bv_ref[...] = _dot(h, w_ref[:, COL_VB:COL_GB]).astype(BF16)
    bg_ref[...] = _dot(h, w_ref[:, COL_GB:COL_CAB]).astype(BF16)
    cab_ref[...] = _dot(h, w_ref[:, COL_CAB:D_IN])


def _in_proj(x2d, g, w_in):
    m = x2d.shape[0]
    tm = min(m, 512)
    widths = ((256, BF16), (512, F32), (512, BF16), (512, BF16), (512, BF16), (512, F32))
    return pl.pallas_call(
        _in_proj_kernel,
        grid=(m // tm,),
        in_specs=[pl.BlockSpec((tm, D_MODEL), lambda i: (i, 0)),
                  _const_spec((1, D_MODEL)),
                  _const_spec((D_MODEL, D_IN))],
        out_specs=[pl.BlockSpec((tm, w), lambda i: (i, 0)) for w, _ in widths],
        out_shape=[jax.ShapeDtypeStruct((m, w), dt) for w, dt in widths],
        compiler_params=_params(("parallel",)),
        name="in_proj",
    )(x2d, g, w_in)


def _band_attn_kernel(*refs, t, tq, has_hist):
    if has_hist:
        aq_ref, akv_ref, hk_ref, hv_ref, bias_ref, o_ref, kc, vc = refs
    else:
        aq_ref, akv_ref, bias_ref, o_ref, kc, vc = refs
    j = pl.program_id(1)
    span = A_REACH + tq

    @pl.when(j == 0)
    def _():
        if has_hist:
            kc[0:A_REACH, :] = hk_ref[...].astype(BF16)
            vc[0:A_REACH, :] = hv_ref[...].astype(BF16)
        else:
            kc[0:A_REACH, :] = jnp.zeros((A_REACH, D_A), BF16)
            vc[0:A_REACH, :] = jnp.zeros((A_REACH, D_A), BF16)
        kc[A_REACH:A_REACH + t, :] = akv_ref[:, 0:D_A].astype(BF16)
        vc[A_REACH:A_REACH + t, :] = akv_ref[:, D_A:2 * D_A].astype(BF16)

    t0 = pl.multiple_of(j * tq, tq)
    q = aq_ref[...]
    lane = lax.broadcasted_iota(jnp.int32, (tq, LANES), 1)
    if not has_hist:
        col = lax.broadcasted_iota(jnp.int32, (tq, span), 1)
        k_ok = col >= A_REACH - t0
    outs = []
    for p in range(A_HEADS // 2):
        qp = q[:, p * LANES:(p + 1) * LANES]
        kp = kc[pl.ds(t0, span), p * LANES:(p + 1) * LANES]
        vp = vc[pl.ds(t0, span), p * LANES:(p + 1) * LANES]
        o_pair = None
        for hh in range(2):
            own = (lane < A_HEAD_DIM) if hh == 0 else (lane >= A_HEAD_DIM)
            qm = jnp.where(own, qp.astype(F32) * A_SCALE, 0.0).astype(BF16)
            s = _dot_nt(qm, kp) + bias_ref[2 * p + hh]
            if not has_hist:
                s = jnp.where(k_ok, s, NEG_INF)
            m = jnp.max(s, axis=-1, keepdims=True)
            e = jnp.exp(s - m)
            l = jnp.sum(e, axis=-1, keepdims=True)
            o = _dot(e.astype(BF16), vp) / l
            o_pair = o if hh == 0 else jnp.where(own, o, o_pair)
        outs.append(o_pair)
    o_ref[...] = jnp.concatenate(outs, axis=1).astype(BF16)


def _band_bias(rel_bias_l, tq):
    span = A_REACH + tq
    period = span + tq
    n_far = A_REACH - REL_CLIP + 1
    far = rel_bias_l[:, 2 * REL_CLIP:]
    n_near = min(2 * REL_CLIP, span - n_far)
    near = jnp.flip(rel_bias_l[:, :2 * REL_CLIP], axis=1)[:, :n_near]
    beyond = jnp.broadcast_to(rel_bias_l[:, :1], (A_HEADS, span - n_far - n_near))
    u = jnp.concatenate([jnp.broadcast_to(far, (A_HEADS, n_far)), near, beyond,
                         jnp.broadcast_to(far, (A_HEADS, tq))], axis=1).astype(F32)
    skew = jnp.broadcast_to(u[:, None, :], (A_HEADS, tq, period)).reshape(A_HEADS, tq * period)
    toeplitz = skew[:, :tq * (period - 1)].reshape(A_HEADS, tq, period - 1)[:, :, :span]
    i = np.arange(tq)[:, None]
    j = np.arange(span)[None, :]
    in_band = (j // CHUNK >= i // CHUNK) & (j // CHUNK <= i // CHUNK + A_LEFT_CHUNKS)
    return jnp.where(in_band[None], toeplitz, NEG_INF)


def _band_attn(aq, akv, hist, rel_bias_l):
    n, t, _ = aq.shape
    tq = min(t, 256)
    span = A_REACH + tq
    bias = _band_bias(rel_bias_l, tq)
    has_hist = hist is not None
    in_specs = [pl.BlockSpec((None, tq, D_A), lambda b, j: (b, j, 0)),
                pl.BlockSpec((None, t, 2 * D_A), lambda b, j: (b, 0, 0))]
    args = [aq, akv]
    if has_hist:
        in_specs += [pl.BlockSpec((None, A_REACH, D_A), lambda b, j: (b, 0, 0))] * 2
        args += list(hist)
    in_specs.append(_const_spec((A_HEADS, tq, span)))
    args.append(bias)
    return pl.pallas_call(
        functools.partial(_band_attn_kernel, t=t, tq=tq, has_hist=has_hist),
        grid=(n, t // tq),
        in_specs=in_specs,
        out_specs=pl.BlockSpec((None, tq, D_A), lambda b, j: (b, j, 0)),
        out_shape=jax.ShapeDtypeStruct((n, t, D_A), BF16),
        scratch_shapes=[pltpu.VMEM((A_REACH + t, D_A), BF16)] * 2,
        compiler_params=_params(("parallel", "arbitrary")),
        name="band_attn",
    )(*args)


def _swap_halves(x):
    lane = lax.broadcasted_iota(jnp.int32, x.shape, 1)
    first = (lane % B_QK_DIM) < (B_QK_DIM // 2)
    return jnp.where(first, pltpu.roll(x, LANES - B_QK_DIM // 2, axis=1), pltpu.roll(x, B_QK_DIM // 2, axis=1))


def _retention_kernel(bqk_ref, bv_ref, bg_ref, cos_ref, sin_ref, dmask_ref, qdec_ref, kdec_ref, sdec_ref,
                      s0_ref, gn_ref, ob_ref, sfin_ref, st, *, t, bc):
    for h in range(B_HEADS):
        off = (h % 2) * B_QK_DIM
        st[h] = jnp.zeros((LANES, B_V_DIM), F32)
        st[h, off:off + B_QK_DIM, :] = s0_ref[h]

    lane = lax.broadcasted_iota(jnp.int32, (bc, LANES), 1)

    def chunk(c, carry):
        r0 = pl.multiple_of(c * bc, bc)
        rows = pl.ds(r0, bc)
        cs = cos_ref[rows, :]
        sn = sin_ref[rows, :]
        for p in range(B_HEADS // 2):
            qx = bqk_ref[rows, p * LANES:(p + 1) * LANES].astype(F32)
            kx = bqk_ref[rows, D_BQK + p * LANES:D_BQK + (p + 1) * LANES].astype(F32)
            qr = qx * cs + _swap_halves(qx) * sn
            kr = (kx * cs + _swap_halves(kx) * sn) * (B_QK_DIM ** -0.5)
            for hh in range(2):
                h = 2 * p + hh
                own = (lane < B_QK_DIM) if hh == 0 else (lane >= B_QK_DIM)
                qh = jnp.where(own, qr, 0.0)
                kh = jnp.where(own, kr, 0.0)
                v = bv_ref[rows, h * B_V_DIM:(h + 1) * B_V_DIM]
                att = _dot_nt(qh.astype(BF16), kh.astype(BF16)) * dmask_ref[h]
                o = (_dot(att.astype(BF16), v)
                     + _dot((qh * qdec_ref[h]).astype(BF16), st[h].astype(BF16)))
                st[h] = st[h] * sdec_ref[h] + _dot_tn((kh * kdec_ref[h]).astype(BF16), v)
                mu = jnp.mean(o, axis=-1, keepdims=True)
                d = o - mu
                var = jnp.mean(d * d, axis=-1, keepdims=True)
                on = d * lax.rsqrt(var + EPS) * gn_ref[:, h * B_V_DIM:(h + 1) * B_V_DIM]
                g = bg_ref[rows, h * B_V_DIM:(h + 1) * B_V_DIM].astype(F32)
                ob_ref[rows, h * B_V_DIM:(h + 1) * B_V_DIM] = (g * jax.nn.sigmoid(g) * on).astype(BF16)
        return carry

    lax.fori_loop(0, t // bc, chunk, 0)
    for h in range(B_HEADS):
        off = (h % 2) * B_QK_DIM
        sfin_ref[h] = st[h, off:off + B_QK_DIM, :]


def _retention_tables(t, bc, pos0):
    log_g = jnp.log(1.0 - 2.0 ** (-(RET_GAMMA_EXP0 + jnp.arange(B_HEADS, dtype=F32))))
    i = jnp.arange(bc, dtype=F32)
    diff = i[:, None] - i[None, :]
    dmask = jnp.where(diff[None] >= 0, jnp.exp(jnp.maximum(diff, 0.0)[None] * log_g[:, None, None]), 0.0)
    qdec = jnp.exp((i[None, :] + 1.0) * log_g[:, None])
    kdec = jnp.exp((bc - 1.0 - i)[None, :] * log_g[:, None])
    sdec = jnp.exp(bc * log_g)
    qdec = jnp.broadcast_to(qdec[:, :, None], (B_HEADS, bc, LANES))
    kdec = jnp.broadcast_to(kdec[:, :, None], (B_HEADS, bc, LANES))
    sdec = jnp.broadcast_to(sdec[:, None, None], (B_HEADS, 1, B_V_DIM))
    half = B_QK_DIM // 2
    pos = pos0 + jnp.arange(t, dtype=F32)
    inv_freq = ROPE_BASE ** (-jnp.arange(half, dtype=F32) / half)
    ang = pos[:, None] * inv_freq[None, :]
    cos = jnp.tile(jnp.cos(ang), (1, LANES // half))
    sin = jnp.tile(jnp.concatenate([-jnp.sin(ang), jnp.sin(ang)], axis=1), (1, LANES // B_QK_DIM))
    return cos, sin, dmask, qdec, kdec, sdec


def _retention(bqk, bv, bg, s0, gn_g, pos0):
    n, t, _ = bqk.shape
    bc = min(t, 256)
    tables = _retention_tables(t, bc, pos0)
    seq = lambda w: pl.BlockSpec((None, t, w), lambda b: (b, 0, 0))
    state = pl.BlockSpec((None, B_HEADS, B_QK_DIM, B_V_DIM), lambda b: (b, 0, 0, 0))
    return pl.pallas_call(
        functools.partial(_retention_kernel, t=t, bc=bc),
        grid=(n,),
        in_specs=[seq(2 * D_BQK), seq(D_B), seq(D_B)] + [_const_spec(tb.shape) for tb in tables]
                 + [state, _const_spec((1, D_B))],
        out_specs=[seq(D_B), state],
        out_shape=[jax.ShapeDtypeStruct((n, t, D_B), BF16),
                   jax.ShapeDtypeStruct((n, B_HEADS, B_QK_DIM, B_V_DIM), F32)],
        scratch_shapes=[pltpu.VMEM((B_HEADS, LANES, B_V_DIM), F32)],
        compiler_params=_params(("parallel",)),
        name="retention",
    )(bqk, bv, bg, *tables, s0, gn_g)


def _conv_kernel(*refs, t, tt, has_buf):
    if has_buf:
        cab_ref, buf_ref, w_ref, b_ref, lng_ref, lnb_ref, oc_ref, nc_ref, up = refs
    else:
        cab_ref, w_ref, b_ref, lng_ref, lnb_ref, oc_ref, nc_ref, up = refs
    hist = CONV_WIDTH - 1
    up[0:CONV_PAD, :] = jnp.zeros((CONV_PAD, C_CHANNELS), F32)
    if has_buf:
        up[CONV_PAD - hist:CONV_PAD, :] = buf_ref[...]

    def glu(i, carry):
        rows = pl.ds(pl.multiple_of(i * tt, tt), tt)
        ca = cab_ref[rows, 0:C_CHANNELS]
        cb = cab_ref[rows, C_CHANNELS:2 * C_CHANNELS]
        up[pl.ds(pl.multiple_of(CONV_PAD + i * tt, 8), tt), :] = ca * jax.nn.sigmoid(cb)
        return carry

    lax.fori_loop(0, t // tt, glu, 0)

    def tile(i, carry):
        t0 = pl.multiple_of(i * tt, tt)
        ext = up[pl.ds(t0, tt + CONV_PAD), :]
        acc = jnp.zeros((tt, C_CHANNELS), F32)
        for b in range(8):
            rb = ext if b == 0 else pltpu.roll(ext, b, axis=0)
            for a in range(CONV_PAD // 8):
                d = 8 * a + b
                if d > hist:
                    continue
                k = hist - d
                acc = acc + w_ref[k:k + 1, :] * rb[CONV_PAD - 8 * a:CONV_PAD - 8 * a + tt, :]
        y = acc + b_ref[...]
        mu = jnp.mean(y, axis=-1, keepdims=True)
        d0 = y - mu
        var = jnp.mean(d0 * d0, axis=-1, keepdims=True)
        yn = d0 * lax.rsqrt(var + EPS) * lng_ref[...] + lnb_ref[...]
        oc_ref[pl.ds(t0, tt), :] = (yn * jax.nn.sigmoid(yn)).astype(BF16)
        return carry

    lax.fori_loop(0, t // tt, tile, 0)
    nc_ref[...] = up[CONV_PAD + t - hist:CONV_PAD + t, :]


def _conv_module(cab, buf, conv_w, conv_b, ln_g, ln_b):
    n, t, _ = cab.shape
    tt = min(t, 128)
    has_buf = buf is not None
    hist = CONV_WIDTH - 1
    in_specs = [pl.BlockSpec((None, t, 2 * C_CHANNELS), lambda b: (b, 0, 0))]
    args = [cab]
    if has_buf:
        in_specs.append(pl.BlockSpec((None, hist, C_CHANNELS), lambda b: (b, 0, 0)))
        args.append(buf)
    in_specs += [_const_spec((CONV_WIDTH, C_CHANNELS))] + [_const_spec((1, C_CHANNELS))] * 3
    args += [conv_w, conv_b, ln_g, ln_b]
    return pl.pallas_call(
        functools.partial(_conv_kernel, t=t, tt=tt, has_buf=has_buf),
        grid=(n,),
        in_specs=in_specs,
        out_specs=[pl.BlockSpec((None, t, C_CHANNELS), lambda b: (b, 0, 0)),
                   pl.BlockSpec((None, hist, C_CHANNELS), lambda b: (b, 0, 0))],
        out_shape=[jax.ShapeDtypeStruct((n, t, C_CHANNELS), BF16),
                   jax.ShapeDtypeStruct((n, hist, C_CHANNELS), F32)],
        scratch_shapes=[pltpu.VMEM((CONV_PAD + t, C_CHANNELS), F32)],
        compiler_params=_params(("parallel",)),
        name="conv_module",
    )(*args)


def _mem_kv_kernel(mem_ref, wk_ref, wv_ref, mk_ref, mv_ref):
    m = mem_ref[...].astype(BF16)
    mk_ref[...] = _dot(m, wk_ref[...])
    mv_ref[...] = _dot(m, wv_ref[...])


def _mem_kv(mem2d, wk, wv):
    m = mem2d.shape[0]
    tm = 512
    row = pl.BlockSpec((tm, D_MODEL), lambda i: (i, 0))
    return pl.pallas_call(
        _mem_kv_kernel,
        grid=(m // tm,),
        in_specs=[row, _const_spec((D_MODEL, D_MODEL)), _const_spec((D_MODEL, D_MODEL))],
        out_specs=[row, row],
        out_shape=[jax.ShapeDtypeStruct((m, D_MODEL), F32)] * 2,
        compiler_params=_params(("parallel",)),
        name="mem_kv",
    )(mem2d, wk, wv)


def _route(logits):
    lane = lax.broadcasted_iota(jnp.int32, logits.shape, 1).astype(F32)
    valid = lane < N_EXPERTS
    lg = jnp.where(valid, logits, NEG_INF)
    e = jnp.exp(lg - jnp.max(lg, axis=-1, keepdims=True))
    probs = e / jnp.sum(e, axis=-1, keepdims=True)
    p1 = jnp.max(probs, axis=-1, keepdims=True)
    i1 = jnp.min(jnp.where(probs == p1, lane, float(LANES)), axis=-1, keepdims=True)
    rest = jnp.where(lane == i1, -1.0, probs)
    rest = jnp.where(valid, rest, -1.0)
    p2 = jnp.max(rest, axis=-1, keepdims=True)
    i2 = jnp.min(jnp.where(rest == p2, lane, float(LANES)), axis=-1, keepdims=True)
    tot = p1 + p2
    return lane, i1, i2, p1 / tot, p2 / tot


def _post_mix_kernel(*refs, mode):
    (x_ref, oa_ref, ob_ref, oc_ref, wout_ref, gmem_ref, wq_ref, mk_ref, mv_ref, wo_ref, gffn_ref) = refs[:11]
    if mode == "dense":
        x2_ref, h_ref = refs[11:]
    else:
        router_ref, x2_ref, h_ref, route_ref = refs[11:]
    y = (_dot(oa_ref[...], wout_ref[0:D_A, :])
         + _dot(ob_ref[...], wout_ref[D_A:D_A + D_B, :])
         + _dot(oc_ref[...], wout_ref[D_A + D_B:D_MODEL, :]))
    x1 = x_ref[...] + y
    q = _dot(_rms(x1, gmem_ref[...]).astype(BF16), wq_ref[...]).astype(BF16)
    outs = []
    for h in range(X_HEADS):
        cols = slice(h * X_HEAD_DIM, (h + 1) * X_HEAD_DIM)
        s = _dot_nt(q[:, cols], mk_ref[:, cols].astype(BF16)) * (X_HEAD_DIM ** -0.5)
        e = jnp.exp(s - jnp.max(s, axis=-1, keepdims=True))
        l = jnp.sum(e, axis=-1, keepdims=True)
        outs.append((_dot(e.astype(BF16), mv_ref[:, cols].astype(BF16)) / l).astype(BF16))
    x2 = x1 + _dot(jnp.concatenate(outs, axis=1), wo_ref[...])
    x2_ref[...] = x2
    hf = _rms(x2, gffn_ref[...])
    h_hi = hf.astype(BF16)
    if mode == "grouped":
        tm = hf.shape[0]
        for c in range(TOKEN_TILE_ROWS):
            h_ref[pl.ds(c, tm, stride=TOKEN_TILE_ROWS), :] = hf[:, c * LANES:(c + 1) * LANES]
    else:
        h_ref[...] = h_hi
    if mode != "dense":
        h_lo = (hf - h_hi.astype(F32)).astype(BF16)
        logits = (_dot(h_hi, router_ref[0]) + _dot(h_lo, router_ref[0])) + _dot(h_hi, router_ref[1])
        lane, i1, i2, w1, w2 = _route(logits)
        if mode == "experts":
            route_ref[...] = jnp.where(lane == i1, w1, jnp.where(lane == i2, w2, 0.0))
        else:
            route_ref[...] = jnp.where(lane == 0.0, i1, jnp.where(lane == 1.0, i2,
                                       jnp.where(lane == 2.0, w1, jnp.where(lane == 3.0, w2, 0.0))))


def _post_mix(x, oa, ob, oc, w_out, g_mem, wq, mk, mv, wo, g_ffn, router, mode):
    n, t, _ = x.shape
    tm = min(t, 512)
    tile = lambda w: pl.BlockSpec((None, tm, w), lambda b, i: (b, i, 0))
    mem = pl.BlockSpec((None,) + mk.shape[1:], lambda b, i: (b, 0, 0))
    sq = _const_spec((D_MODEL, D_MODEL))
    vec = _const_spec((1, D_MODEL))
    in_specs = [tile(D_MODEL), tile(D_A), tile(D_B), tile(C_CHANNELS), sq, vec, sq, mem, mem, sq, vec]
    args = [x, oa, ob, oc, w_out, g_mem, wq, mk, mv, wo, g_ffn]
    out_specs = [tile(D_MODEL)]
    out_shape = [jax.ShapeDtypeStruct((n, t, D_MODEL), F32)]
    if mode == "grouped":
        out_specs.append(pl.BlockSpec((None, tm * TOKEN_TILE_ROWS, LANES), lambda b, i: (b, i, 0)))
        out_shape.append(jax.ShapeDtypeStruct((n, t * TOKEN_TILE_ROWS, LANES), F32))
    else:
        out_specs.append(tile(D_MODEL))
        out_shape.append(jax.ShapeDtypeStruct((n, t, D_MODEL), BF16))
    if mode != "dense":
        in_specs.append(_const_spec((2, D_MODEL, LANES)))
        args.append(router)
        out_specs.append(tile(LANES))
        out_shape.append(jax.ShapeDtypeStruct((n, t, LANES), F32))
    return pl.pallas_call(
        functools.partial(_post_mix_kernel, mode=mode),
        grid=(n, t // tm),
        in_specs=in_specs,
        out_specs=out_specs,
        out_shape=out_shape,
        compiler_params=_params(("parallel", "parallel")),
        name="post_mix",
    )(*args)


def _ffn_kernel(x_ref, h_ref, wg_ref, wu_ref, wd_ref, o_ref):
    h = h_ref[...]
    acc = x_ref[...]
    for c in range(D_FF // FF_CHUNK):
        cols = slice(c * FF_CHUNK, (c + 1) * FF_CHUNK)
        g = _dot(h, wg_ref[:, cols])
        u = _dot(h, wu_ref[:, cols])
        acc = acc + _dot((g * jax.nn.sigmoid(g) * u).astype(BF16), wd_ref[cols, :])
    o_ref[...] = acc


def _ffn(x2d, h2d, wg, wu, wd):
    m = x2d.shape[0]
    tm = min(m, 512)
    row = pl.BlockSpec((tm, D_MODEL), lambda i: (i, 0))
    return pl.pallas_call(
        _ffn_kernel,
        grid=(m // tm,),
        in_specs=[row, row, _const_spec((D_MODEL, D_FF)), _const_spec((D_MODEL, D_FF)),
                  _const_spec((D_FF, D_MODEL))],
        out_specs=row,
        out_shape=jax.ShapeDtypeStruct((m, D_MODEL), F32),
        compiler_params=_params(("parallel",)),
        name="ffn_dense",
    )(x2d, h2d, wg, wu, wd)


def _moe_kernel(x_ref, h_ref, comb_ref, wg_ref, wu_ref, wd_ref, gfin_ref, o_ref, acc):
    e = pl.program_id(1)

    @pl.when(e == 0)
    def _():
        acc[...] = x_ref[...]

    h = h_ref[...]
    g = _dot(h, wg_ref[...])
    u = _dot(h, wu_ref[...])
    y = _dot((g * jax.nn.sigmoid(g) * u).astype(BF16), wd_ref[...])
    comb = comb_ref[...]
    lane = lax.broadcasted_iota(jnp.int32, comb.shape, 1)
    ce = jnp.sum(jnp.where(lane == e, comb, 0.0), axis=-1, keepdims=True)
    acc[...] += ce * y

    @pl.when(e == N_EXPERTS - 1)
    def _():
        o_ref[...] = _rms(acc[...], gfin_ref[...])


def _moe(x2d, h2d, comb2d, wg, wu, wd, g_fin):
    m = x2d.shape[0]
    tm = min(m, 512)
    row = lambda w: pl.BlockSpec((tm, w), lambda i, e: (i, 0))
    return pl.pallas_call(
        _moe_kernel,
        grid=(m // tm, N_EXPERTS),
        in_specs=[row(D_MODEL), row(D_MODEL), row(LANES),
                  pl.BlockSpec((None, D_MODEL, D_FF_EXPERT), lambda i, e: (e, 0, 0)),
                  pl.BlockSpec((None, D_MODEL, D_FF_EXPERT), lambda i, e: (e, 0, 0)),
                  pl.BlockSpec((None, D_FF_EXPERT, D_MODEL), lambda i, e: (e, 0, 0)),
                  _const_spec((1, D_MODEL))],
        out_specs=row(D_MODEL),
        out_shape=jax.ShapeDtypeStruct((m, D_MODEL), F32),
        scratch_shapes=[pltpu.VMEM((tm, D_MODEL), F32)],
        compiler_params=_params(("parallel", "arbitrary")),
        name="moe_experts",
    )(x2d, h2d, comb2d, wg, wu, wd, g_fin)


def _moe_plan(route2d, m, tb):
    nb = 2 * m // tb + N_EXPERTS
    eid = jnp.concatenate([route2d[:, 0], route2d[:, 1]]).astype(jnp.int32)
    onehot = (eid[:, None] == jnp.arange(N_EXPERTS, dtype=jnp.int32)[None, :]).astype(jnp.int32)
    csum = jnp.cumsum(onehot, axis=0)
    cnt = csum[-1]
    rank = jnp.sum((csum - onehot) * onehot, axis=1)
    gsz = (cnt + tb - 1) // tb * tb
    gend = jnp.cumsum(gsz)
    dest = jnp.sum(onehot * (gend - gsz)[None, :], axis=1) + rank
    inv = jnp.full((nb * tb,), -1, jnp.int32).at[dest].set(jnp.arange(2 * m, dtype=jnp.int32),
                                                          unique_indices=True)
    valid = inv >= 0
    spare = 2 * m + jnp.cumsum(jnp.logical_not(valid).astype(jnp.int32)) - 1
    gidx = jnp.where(valid, jnp.where(inv >= m, inv - m, inv), 0)
    sidx = jnp.where(valid, inv, spare)
    blk_expert = jnp.sum((jnp.arange(nb, dtype=jnp.int32) * tb)[:, None] >= gend[None, :], axis=1)
    blk_expert = jnp.minimum(blk_expert, N_EXPERTS - 1).astype(jnp.int32)
    return blk_expert, gidx.reshape(nb, 1, tb), sidx.reshape(nb, 1, tb)


def _moe_grouped_kernel(be_ref, gcur_ref, gnext_ref, scur_ref, h_hbm, wg_ref, wu_ref, wd_ref, y_hbm,
                        xbuf, ybuf, sem_in, sem_out, *, tb):
    b = pl.program_id(0)
    nb = pl.num_programs(0)
    slot = b % 2
    rows = tb * TOKEN_TILE_ROWS

    def token_rows(i):
        return pl.ds(pl.multiple_of(i * TOKEN_TILE_ROWS, TOKEN_TILE_ROWS), TOKEN_TILE_ROWS)

    def gather(idx_ref, s):
        def body(r, carry):
            pltpu.make_async_copy(h_hbm.at[token_rows(idx_ref[0, r]), :], xbuf.at[s, token_rows(r), :],
                                  sem_in.at[s]).start()
            return carry
        lax.fori_loop(0, tb, body, 0, unroll=8)

    def wait_gather(s):
        pltpu.make_async_copy(h_hbm.at[pl.ds(0, rows), :], xbuf.at[s], sem_in.at[s]).wait()

    def wait_scatter(s):
        pltpu.make_async_copy(ybuf.at[s], y_hbm.at[pl.ds(0, rows), :], sem_out.at[s]).wait()

    @pl.when(b == 0)
    def _():
        gather(gcur_ref, 0)

    @pl.when(b + 1 < nb)
    def _():
        gather(gnext_ref, 1 - slot)

    wait_gather(slot)

    @pl.when(b >= 2)
    def _():
        wait_scatter(slot)

    xs = xbuf.at[slot]
    x = jnp.concatenate([xs[pl.ds(c, tb, stride=TOKEN_TILE_ROWS), :] for c in range(TOKEN_TILE_ROWS)],
                        axis=1).astype(BF16)
    g = _dot(x, wg_ref[...])
    u = _dot(x, wu_ref[...])
    y = _dot((g * jax.nn.sigmoid(g) * u).astype(BF16), wd_ref[...])
    ys = ybuf.at[slot]
    for c in range(TOKEN_TILE_ROWS):
        ys[pl.ds(c, tb, stride=TOKEN_TILE_ROWS), :] = y[:, c * LANES:(c + 1) * LANES]

    def scatter(r, carry):
        pltpu.make_async_copy(ybuf.at[slot, token_rows(r), :], y_hbm.at[token_rows(scur_ref[0, r]), :],
                              sem_out.at[slot]).start()
        return carry
    lax.fori_loop(0, tb, scatter, 0, unroll=8)

    @pl.when(b == nb - 1)
    def _():
        wait_scatter(slot)
        wait_scatter(1 - slot)


def _moe_grouped(h_tiles, blk_expert, gidx, sidx, wg, wu, wd, tb):
    nb = gidx.shape[0]
    assert nb >= 2
    rows = tb * TOKEN_TILE_ROWS
    idx_spec = lambda f: pl.BlockSpec((None, 1, tb), f, memory_space=pltpu.SMEM)
    wspec = lambda shape: pl.BlockSpec((None,) + shape, lambda b, be: (be[b], 0, 0))
    return pl.pallas_call(
        functools.partial(_moe_grouped_kernel, tb=tb),
        grid_spec=pltpu.PrefetchScalarGridSpec(
            num_scalar_prefetch=1,
            grid=(nb,),
            in_specs=[idx_spec(lambda b, be: (b, 0, 0)),
                      idx_spec(lambda b, be: (jnp.minimum(b + 1, nb - 1), 0, 0)),
                      idx_spec(lambda b, be: (b, 0, 0)),
                      pl.BlockSpec(memory_space=pl.ANY),
                      wspec((D_MODEL, D_FF_EXPERT)), wspec((D_MODEL, D_FF_EXPERT)),
                      wspec((D_FF_EXPERT, D_MODEL))],
            out_specs=pl.BlockSpec(memory_space=pl.ANY),
            scratch_shapes=[pltpu.VMEM((2, rows, LANES), F32), pltpu.VMEM((2, rows, LANES), F32),
                            pltpu.SemaphoreType.DMA((2,)), pltpu.SemaphoreType.DMA((2,))]),
        out_shape=jax.ShapeDtypeStruct((nb * rows, LANES), F32),
        compiler_params=_params(("arbitrary",)),
        name="moe_grouped",
    )(blk_expert, gidx, gidx, sidx, h_tiles, wg, wu, wd)


def _moe_combine_kernel(x_ref, route_ref, y0_ref, y1_ref, gfin_ref, o_ref):
    tm = x_ref.shape[0]
    untile = lambda ref: jnp.concatenate(
        [ref[pl.ds(c, tm, stride=TOKEN_TILE_ROWS), :] for c in range(TOKEN_TILE_ROWS)], axis=1)
    route = route_ref[...]
    acc = x_ref[...] + route[:, 2:3] * untile(y0_ref) + route[:, 3:4] * untile(y1_ref)
    o_ref[...] = _rms(acc, gfin_ref[...])


def _moe_combine(x2d, route2d, y_tiles, g_fin):
    m = x2d.shape[0]
    tm = MOE_COMBINE_TILE
    rows = tm * TOKEN_TILE_ROWS
    return pl.pallas_call(
        _moe_combine_kernel,
        grid=(m // tm,),
        in_specs=[pl.BlockSpec((tm, D_MODEL), lambda i: (i, 0)),
                  pl.BlockSpec((tm, LANES), lambda i: (i, 0)),
                  pl.BlockSpec((rows, LANES), lambda i: (i, 0)),
                  pl.BlockSpec((rows, LANES), lambda i: (i + m // tm, 0)),
                  _const_spec((1, D_MODEL))],
        out_specs=pl.BlockSpec((tm, D_MODEL), lambda i: (i, 0)),
        out_shape=jax.ShapeDtypeStruct((m, D_MODEL), F32),
        compiler_params=_params(("parallel",)),
        name="moe_combine",
    )(x2d, route2d, y_tiles, y_tiles, g_fin)


def _layer(x, pos0, attn_hist, ret_state, conv_state, mk, mv, w):
    n, t, _ = x.shape
    aq, akv, bqk, bv, bg, cab = _in_proj(x.reshape(n * t, D_MODEL), w["norm_mix_g"], w["w_in"])
    r3 = lambda a: a.reshape(n, t, a.shape[-1])
    aq, akv, bqk, bv, bg, cab = map(r3, (aq, akv, bqk, bv, bg, cab))
    oa = _band_attn(aq, akv, attn_hist, w["rel_bias"])
    ob, new_s = _retention(bqk, bv, bg, ret_state, w["ret_gn_g"], pos0)
    oc, new_conv = _conv_module(cab, conv_state, w["conv_w"], w["conv_b"], w["conv_ln_g"], w["conv_ln_b"])
    m = n * t
    if "router" not in w:
        mode = "dense"
    elif m >= N_EXPERTS * MOE_BLOCK and m % MOE_COMBINE_TILE == 0:
        mode = "grouped"
    else:
        mode = "experts"
    post = _post_mix(x, oa, ob, oc, w["w_out"], w["norm_mem_g"], w["wx_q"], mk, mv, w["wx_o"],
                     w["norm_ffn_g"], w.get("router"), mode)
    x2 = post[0].reshape(m, D_MODEL)
    if mode == "dense":
        x3 = _ffn(x2, post[1].reshape(m, D_MODEL), w["ffn_g"], w["ffn_u"], w["ffn_d"])
    elif mode == "experts":
        x3 = _moe(x2, post[1].reshape(m, D_MODEL), post[2].reshape(m, LANES),
                  w["moe_g"], w["moe_u"], w["moe_d"], w["final_g"])
    else:
        route = post[2].reshape(m, LANES)
        blk_expert, gidx, sidx = _moe_plan(route, m, MOE_BLOCK)
        y_tiles = _moe_grouped(post[1].reshape(m * TOKEN_TILE_ROWS, LANES), blk_expert, gidx, sidx,
                               w["moe_g"], w["moe_u"], w["moe_d"], MOE_BLOCK)
        x3 = _moe_combine(x2, route, y_tiles, w["final_g"])
    keep = min(A_REACH, t)
    new_kv = akv[:, t - keep:, :]
    new_k = new_kv[..., :D_A].reshape(n, keep, A_HEADS, A_HEAD_DIM)
    new_v = new_kv[..., D_A:].reshape(n, keep, A_HEADS, A_HEAD_DIM)
    return x3.reshape(n, t, D_MODEL), new_k, new_v, new_s, new_conv


def kernel(x_prompt, x_sample, cache_attn_k, cache_attn_v, state_ret, state_conv, cache_mem_k, cache_mem_v,
           mem_prompt, norm_mix_g, w_in, rel_bias, ret_gn_g, conv_w, conv_b, conv_ln_g, conv_ln_b, w_out,
           norm_mem_g, wx_q, wx_k, wx_v, wx_o, norm_ffn_g, ffn_w_gate, ffn_w_up, ffn_w_down,
           router_w, moe_w_gate, moe_w_up, moe_w_down, final_norm_g):
    depth = w_in.shape[0]
    assert depth == 2, "layer 0 dense FFN, layer 1 experts + closing norm"
    n_p, _, _ = x_prompt.shape
    n_s, t_s, _ = x_sample.shape
    n_mem = mem_prompt.shape[1]
    row = lambda a: a.reshape(1, -1).astype(F32)
    xp, xs = x_prompt, x_sample
    mem2d = mem_prompt.reshape(n_p * n_mem, D_MODEL)
    outs_p = [[] for _ in range(6)]
    outs_s = [[] for _ in range(4)]
    for l in range(depth):
        w = dict(norm_mix_g=row(norm_mix_g[l]), w_in=w_in[l].astype(BF16), rel_bias=rel_bias[l],
                 ret_gn_g=row(ret_gn_g[l]), conv_w=conv_w[l].astype(F32), conv_b=row(conv_b[l]),
                 conv_ln_g=row(conv_ln_g[l]), conv_ln_b=row(conv_ln_b[l]), w_out=w_out[l].astype(BF16),
                 norm_mem_g=row(norm_mem_g[l]), wx_q=wx_q[l].astype(BF16), wx_o=wx_o[l].astype(BF16),
                 norm_ffn_g=row(norm_ffn_g[l]))
        if l % 2 == 0:
            i = l // 2
            w.update(ffn_g=ffn_w_gate[i].astype(BF16), ffn_u=ffn_w_up[i].astype(BF16),
                     ffn_d=ffn_w_down[i].astype(BF16))
        else:
            i = l // 2
            r = jnp.pad(router_w[i].astype(F32), ((0, 0), (0, LANES - N_EXPERTS)))
            r_hi = r.astype(BF16)
            r_lo = (r - r_hi.astype(F32)).astype(BF16)
            w.update(router=jnp.stack([r_hi, r_lo]),
                     moe_g=moe_w_gate[i].astype(BF16), moe_u=moe_w_up[i].astype(BF16),
                     moe_d=moe_w_down[i].astype(BF16),
                     final_g=row(final_norm_g) if l == depth - 1 else None)
        mk, mv = _mem_kv(mem2d, wx_k[l].astype(BF16), wx_v[l].astype(BF16))
        mk = mk.reshape(n_p, n_mem, D_MODEL)
        mv = mv.reshape(n_p, n_mem, D_MODEL)
        xp, k_new, v_new, s_new, c_new = _layer(
            xp, 0.0, None, jnp.zeros((n_p, B_HEADS, B_QK_DIM, B_V_DIM), F32), None, mk, mv, w)
        for lst, a in zip(outs_p, (k_new, v_new, s_new, c_new,
                                   mk.reshape(n_p, n_mem, X_HEADS, X_HEAD_DIM),
                                   mv.reshape(n_p, n_mem, X_HEADS, X_HEAD_DIM))):
            lst.append(a)
        hist = (cache_attn_k[l].reshape(n_s, A_REACH, D_A), cache_attn_v[l].reshape(n_s, A_REACH, D_A))
        xs, k_new, v_new, s_new, c_new = _layer(
            xs, float(PAST_LEN), hist, state_ret[l], state_conv[l],
            cache_mem_k[l].reshape(n_s, n_mem, D_MODEL), cache_mem_v[l].reshape(n_s, n_mem, D_MODEL), w)
        for lst, a in zip(outs_s, (k_new, v_new, s_new, c_new)):
            lst.append(a)
    st = lambda lst: jnp.stack(lst)
    return (xp, xs, st(outs_p[0]), st(outs_p[1]), st(outs_p[2]), st(outs_p[3]), st(outs_p[4]), st(outs_p[5]),
            st(outs_s[0]), st(outs_s[1]), st(outs_s[2]), st(outs_s[3]))
```

```python
import functools

import numpy as np
import jax
import jax.numpy as jnp
from jax import lax
from jax.experimental import pallas as pl
from jax.experimental.pallas import tpu as pltpu

F32 = jnp.float32
BF16 = jnp.bfloat16

D_MODEL = 1024
PAST_LEN = 2048
CHUNK = 64
EPS = 1e-6
NEG_INF = -1e30
LANES = 128

A_HEADS = 4
A_HEAD_DIM = 64
D_A = A_HEADS * A_HEAD_DIM
A_LEFT_CHUNKS = 8
A_REACH = A_LEFT_CHUNKS * CHUNK
REL_CLIP = 128
A_SCALE = A_HEAD_DIM ** -0.5

B_HEADS = 4
B_QK_DIM = 64
B_V_DIM = 128
D_BQK = B_HEADS * B_QK_DIM
D_B = B_HEADS * B_V_DIM
RET_GAMMA_EXP0 = 5.0
ROPE_BASE = 10000.0

C_CHANNELS = 256
CONV_WIDTH = 31
CONV_PAD = 32

X_HEADS = 4
X_HEAD_DIM = D_MODEL // X_HEADS

D_FF = 11 * D_MODEL // 4
FF_CHUNK = 256
N_EXPERTS = 8
TOP_K = 2
D_FF_EXPERT = D_FF // 2
TOKEN_TILE_ROWS = D_MODEL // LANES
MOE_BLOCK = 512
MOE_COMBINE_TILE = 512

COL_QA, COL_KVA, COL_QKB, COL_VB, COL_GB, COL_CAB, D_IN = 0, 256, 768, 1280, 1792, 2304, 2816

VMEM_LIMIT = 56 * 1024 * 1024


def _params(sem):
    return pltpu.CompilerParams(dimension_semantics=sem, vmem_limit_bytes=VMEM_LIMIT)


def _rms(x, g):
    return x * lax.rsqrt(jnp.mean(x * x, axis=-1, keepdims=True) + EPS) * g


def _dot(a, b):
    return jnp.dot(a, b, preferred_element_type=F32)


def _dot_nt(a, b):
    return lax.dot_general(a, b, (((1,), (1,)), ((), ())), preferred_element_type=F32)


def _dot_tn(a, b):
    return lax.dot_general(a, b, (((0,), (0,)), ((), ())), preferred_element_type=F32)


def _const_spec(shape):
    return pl.BlockSpec(shape, lambda *_: (0,) * len(shape))


def _in_proj_kernel(x_ref, g_ref, w_ref, aq_ref, akv_ref, bqk_ref, bv_ref, bg_ref, cab_ref):
    h = _rms(x_ref[...], g_ref[...]).astype(BF16)
    aq_ref[...] = _dot(h, w_ref[:, COL_QA:COL_KVA]).astype(BF16)
    akv_ref[...] = _dot(h, w_ref[:, COL_KVA:COL_QKB])
    bqk_ref[...] = _dot(h, w_ref[:, COL_QKB:COL_VB]).astype(BF16)
    bv_ref[...] = _dot(h, w_ref[:, COL_VB:COL_GB]).astype(BF16)
    bg_ref[...] = _dot(h, w_ref[:, COL_GB:COL_CAB]).astype(BF16)
    cab_ref[...] = _dot(h, w_ref[:, COL_CAB:D_IN])


def _in_proj(x2d, g, w_in):
    m = x2d.shape[0]
    tm = min(m, 512)
    widths = ((256, BF16), (512, F32), (512, BF16), (512, BF16), (512, BF16), (512, F32))
    return pl.pallas_call(
        _in_proj_kernel,
        grid=(m // tm,),
        in_specs=[pl.BlockSpec((tm, D_MODEL), lambda i: (i, 0)),
                  _const_spec((1, D_MODEL)),
                  _const_spec((D_MODEL, D_IN))],
        out_specs=[pl.BlockSpec((tm, w), lambda i: (i, 0)) for w, _ in widths],
        out_shape=[jax.ShapeDtypeStruct((m, w), dt) for w, dt in widths],
        compiler_params=_params(("parallel",)),
        name="in_proj",
    )(x2d, g, w_in)


def _band_attn_kernel(*refs, t, tq, has_hist):
    if has_hist:
        aq_ref, akv_ref, hk_ref, hv_ref, bias_ref, o_ref, kc, vc = refs
    else:
        aq_ref, akv_ref, bias_ref, o_ref, kc, vc = refs
    j = pl.program_id(1)
    span = A_REACH + tq

    @pl.when(j == 0)
    def _():
        if has_hist:
            kc[0:A_REACH, :] = hk_ref[...].astype(BF16)
            vc[0:A_REACH, :] = hv_ref[...].astype(BF16)
        else:
            kc[0:A_REACH, :] = jnp.zeros((A_REACH, D_A), BF16)
            vc[0:A_REACH, :] = jnp.zeros((A_REACH, D_A), BF16)
        kc[A_REACH:A_REACH + t, :] = akv_ref[:, 0:D_A].astype(BF16)
        vc[A_REACH:A_REACH + t, :] = akv_ref[:, D_A:2 * D_A].astype(BF16)

    t0 = pl.multiple_of(j * tq, tq)
    q = aq_ref[...]
    lane = lax.broadcasted_iota(jnp.int32, (tq, LANES), 1)
    if not has_hist:
        col = lax.broadcasted_iota(jnp.int32, (tq, span), 1)
        k_ok = col >= A_REACH - t0
    outs = []
    for p in range(A_HEADS // 2):
        qp = q[:, p * LANES:(p + 1) * LANES]
        kp = kc[pl.ds(t0, span), p * LANES:(p + 1) * LANES]
        vp = vc[pl.ds(t0, span), p * LANES:(p + 1) * LANES]
        o_pair = None
        for hh in range(2):
            own = (lane < A_HEAD_DIM) if hh == 0 else (lane >= A_HEAD_DIM)
            qm = jnp.where(own, qp.astype(F32) * A_SCALE, 0.0).astype(BF16)
            s = _dot_nt(qm, kp) + bias_ref[2 * p + hh]
            if not has_hist:
                s = jnp.where(k_ok, s, NEG_INF)
            m = jnp.max(s, axis=-1, keepdims=True)
            e = jnp.exp(s - m)
            l = jnp.sum(e, axis=-1, keepdims=True)
            o = _dot(e.astype(BF16), vp) / l
            o_pair = o if hh == 0 else jnp.where(own, o, o_pair)
        outs.append(o_pair)
    o_ref[...] = jnp.concatenate(outs, axis=1).astype(BF16)


def _band_bias(rel_bias_l, tq):
    span = A_REACH + tq
    period = span + tq
    n_far = A_REACH - REL_CLIP + 1
    far = rel_bias_l[:, 2 * REL_CLIP:]
    n_near = min(2 * REL_CLIP, span - n_far)
    near = jnp.flip(rel_bias_l[:, :2 * REL_CLIP], axis=1)[:, :n_near]
    beyond = jnp.broadcast_to(rel_bias_l[:, :1], (A_HEADS, span - n_far - n_near))
    u = jnp.concatenate([jnp.broadcast_to(far, (A_HEADS, n_far)), near, beyond,
                         jnp.broadcast_to(far, (A_HEADS, tq))], axis=1).astype(F32)
    skew = jnp.broadcast_to(u[:, None, :], (A_HEADS, tq, period)).reshape(A_HEADS, tq * period)
    toeplitz = skew[:, :tq * (period - 1)].reshape(A_HEADS, tq, period - 1)[:, :, :span]
    i = np.arange(tq)[:, None]
    j = np.arange(span)[None, :]
    in_band = (j // CHUNK >= i // CHUNK) & (j // CHUNK <= i // CHUNK + A_LEFT_CHUNKS)
    return jnp.where(in_band[None], toeplitz, NEG_INF)


def _band_attn(aq, akv, hist, rel_bias_l):
    n, t, _ = aq.shape
    tq = min(t, 256)
    span = A_REACH + tq
    bias = _band_bias(rel_bias_l, tq)
    has_hist = hist is not None
    in_specs = [pl.BlockSpec((None, tq, D_A), lambda b, j: (b, j, 0)),
                pl.BlockSpec((None, t, 2 * D_A), lambda b, j: (b, 0, 0))]
    args = [aq, akv]
    if has_hist:
        in_specs += [pl.BlockSpec((None, A_REACH, D_A), lambda b, j: (b, 0, 0))] * 2
        args += list(hist)
    in_specs.append(_const_spec((A_HEADS, tq, span)))
    args.append(bias)
    return pl.pallas_call(
        functools.partial(_band_attn_kernel, t=t, tq=tq, has_hist=has_hist),
        grid=(n, t // tq),
        in_specs=in_specs,
        out_specs=pl.BlockSpec((None, tq, D_A), lambda b, j: (b, j, 0)),
        out_shape=jax.ShapeDtypeStruct((n, t, D_A), BF16),
        scratch_shapes=[pltpu.VMEM((A_REACH + t, D_A), BF16)] * 2,
        compiler_params=_params(("parallel", "arbitrary")),
        name="band_attn",
    )(*args)


def _swap_halves(x):
    lane = lax.broadcasted_iota(jnp.int32, x.shape, 1)
    first = (lane % B_QK_DIM) < (B_QK_DIM // 2)
    return jnp.where(first, pltpu.roll(x, LANES - B_QK_DIM // 2, axis=1), pltpu.roll(x, B_QK_DIM // 2, axis=1))


def _retention_kernel(bqk_ref, bv_ref, bg_ref, cos_ref, sin_ref, dmask_ref, qdec_ref, kdec_ref, sdec_ref,
                      s0_ref, gn_ref, ob_ref, sfin_ref, st, *, t, bc):
    for h in range(B_HEADS):
        off = (h % 2) * B_QK_DIM
        st[h] = jnp.zeros((LANES, B_V_DIM), F32)
        st[h, off:off + B_QK_DIM, :] = s0_ref[h]

    lane = lax.broadcasted_iota(jnp.int32, (bc, LANES), 1)

    def chunk(c, carry):
        r0 = pl.multiple_of(c * bc, bc)
        rows = pl.ds(r0, bc)
        cs = cos_ref[rows, :]
        sn = sin_ref[rows, :]
        for p in range(B_HEADS // 2):
            qx = bqk_ref[rows, p * LANES:(p + 1) * LANES].astype(F32)
            kx = bqk_ref[rows, D_BQK + p * LANES:D_BQK + (p + 1) * LANES].astype(F32)
            qr = qx * cs + _swap_halves(qx) * sn
            kr = (kx * cs + _swap_halves(kx) * sn) * (B_QK_DIM ** -0.5)
            for hh in range(2):
                h = 2 * p + hh
                own = (lane < B_QK_DIM) if hh == 0 else (lane >= B_QK_DIM)
                qh = jnp.where(own, qr, 0.0)
                kh = jnp.where(own, kr, 0.0)
                v = bv_ref[rows, h * B_V_DIM:(h + 1) * B_V_DIM]
                att = _dot_nt(qh.astype(BF16), kh.astype(BF16)) * dmask_ref[h]
                o = (_dot(att.astype(BF16), v)
                     + _dot((qh * qdec_ref[h]).astype(BF16), st[h].astype(BF16)))
                st[h] = st[h] * sdec_ref[h] + _dot_tn((kh * kdec_ref[h]).astype(BF16), v)
                mu = jnp.mean(o, axis=-1, keepdims=True)
                d = o - mu
                var = jnp.mean(d * d, axis=-1, keepdims=True)
                on = d * lax.rsqrt(var + EPS) * gn_ref[:, h * B_V_DIM:(h + 1) * B_V_DIM]
                g = bg_ref[rows, h * B_V_DIM:(h + 1) * B_V_DIM].astype(F32)
                ob_ref[rows, h * B_V_DIM:(h + 1) * B_V_DIM] = (g * jax.nn.sigmoid(g) * on).astype(BF16)
        return carry

    lax.fori_loop(0, t // bc, chunk, 0)
    for h in range(B_HEADS):
        off = (h % 2) * B_QK_DIM
        sfin_ref[h] = st[h, off:off + B_QK_DIM, :]


def _retention_tables(t, bc, pos0):
    log_g = jnp.log(1.0 - 2.0 ** (-(RET_GAMMA_EXP0 + jnp.arange(B_HEADS, dtype=F32))))
    i = jnp.arange(bc, dtype=F32)
    diff = i[:, None] - i[None, :]
    dmask = jnp.where(diff[None] >= 0, jnp.exp(jnp.maximum(diff, 0.0)[None] * log_g[:, None, None]), 0.0)
    qdec = jnp.exp((i[None, :] + 1.0) * log_g[:, None])
    kdec = jnp.exp((bc - 1.0 - i)[None, :] * log_g[:, None])
    sdec = jnp.exp(bc * log_g)
    qdec = jnp.broadcast_to(qdec[:, :, None], (B_HEADS, bc, LANES))
    kdec = jnp.broadcast_to(kdec[:, :, None], (B_HEADS, bc, LANES))
    sdec = jnp.broadcast_to(sdec[:, None, None], (B_HEADS, 1, B_V_DIM))
    half = B_QK_DIM // 2
    pos = pos0 + jnp.arange(t, dtype=F32)
    inv_freq = ROPE_BASE ** (-jnp.arange(half, dtype=F32) / half)
    ang = pos[:, None] * inv_freq[None, :]
    cos = jnp.tile(jnp.cos(ang), (1, LANES // half))
    sin = jnp.tile(jnp.concatenate([-jnp.sin(ang), jnp.sin(ang)], axis=1), (1, LANES // B_QK_DIM))
    return cos, sin, dmask, qdec, kdec, sdec


def _retention(bqk, bv, bg, s0, gn_g, pos0):
    n, t, _ = bqk.shape
    bc = min(t, 256)
    tables = _retention_tables(t, bc, pos0)
    seq = lambda w: pl.BlockSpec((None, t, w), lambda b: (b, 0, 0))
    state = pl.BlockSpec((None, B_HEADS, B_QK_DIM, B_V_DIM), lambda b: (b, 0, 0, 0))
    return pl.pallas_call(
        functools.partial(_retention_kernel, t=t, bc=bc),
        grid=(n,),
        in_specs=[seq(2 * D_BQK), seq(D_B), seq(D_B)] + [_const_spec(tb.shape) for tb in tables]
                 + [state, _const_spec((1, D_B))],
        out_specs=[seq(D_B), state],
        out_shape=[jax.ShapeDtypeStruct((n, t, D_B), BF16),
                   jax.ShapeDtypeStruct((n, B_HEADS, B_QK_DIM, B_V_DIM), F32)],
        scratch_shapes=[pltpu.VMEM((B_HEADS, LANES, B_V_DIM), F32)],
        compiler_params=_params(("parallel",)),
        name="retention",
    )(bqk, bv, bg, *tables, s0, gn_g)


def _conv_kernel(*refs, t, tt, has_buf):
    if has_buf:
        cab_ref, buf_ref, w_ref, b_ref, lng_ref, lnb_ref, oc_ref, nc_ref, up = refs
    else:
        cab_ref, w_ref, b_ref, lng_ref, lnb_ref, oc_ref, nc_ref, up = refs
    hist = CONV_WIDTH - 1
    up[0:CONV_PAD, :] = jnp.zeros((CONV_PAD, C_CHANNELS), F32)
    if has_buf:
        up[CONV_PAD - hist:CONV_PAD, :] = buf_ref[...]

    def glu(i, carry):
        rows = pl.ds(pl.multiple_of(i * tt, tt), tt)
        ca = cab_ref[rows, 0:C_CHANNELS]
        cb = cab_ref[rows, C_CHANNELS:2 * C_CHANNELS]
        up[pl.ds(pl.multiple_of(CONV_PAD + i * tt, 8), tt), :] = ca * jax.nn.sigmoid(cb)
        return carry

    lax.fori_loop(0, t // tt, glu, 0)

    def tile(i, carry):
        t0 = pl.multiple_of(i * tt, tt)
        ext = up[pl.ds(t0, tt + CONV_PAD), :]
        acc = jnp.zeros((tt, C_CHANNELS), F32)
        for b in range(8):
            rb = ext if b == 0 else pltpu.roll(ext, b, axis=0)
            for a in range(CONV_PAD // 8):
                d = 8 * a + b
                if d > hist:
                    continue
                k = hist - d
                acc = acc + w_ref[k:k + 1, :] * rb[CONV_PAD - 8 * a:CONV_PAD - 8 * a + tt, :]
        y = acc + b_ref[...]
        mu = jnp.mean(y, axis=-1, keepdims=True)
        d0 = y - mu
        var = jnp.mean(d0 * d0, axis=-1, keepdims=True)
        yn = d0 * lax.rsqrt(var + EPS) * lng_ref[...] + lnb_ref[...]
        oc_ref[pl.ds(t0, tt), :] = (yn * jax.nn.sigmoid(yn)).astype(BF16)
        return carry

    lax.fori_loop(0, t // tt, tile, 0)
    nc_ref[...] = up[CONV_PAD + t - hist:CONV_PAD + t, :]


def _conv_module(cab, buf, conv_w, conv_b, ln_g, ln_b):
    n, t, _ = cab.shape
    tt = min(t, 128)
    has_buf = buf is not None
    hist = CONV_WIDTH - 1
    in_specs = [pl.BlockSpec((None, t, 2 * C_CHANNELS), lambda b: (b, 0, 0))]
    args = [cab]
    if has_buf:
        in_specs.append(pl.BlockSpec((None, hist, C_CHANNELS), lambda b: (b, 0, 0)))
        args.append(buf)
    in_specs += [_const_spec((CONV_WIDTH, C_CHANNELS))] + [_const_spec((1, C_CHANNELS))] * 3
    args += [conv_w, conv_b, ln_g, ln_b]
    return pl.pallas_call(
        functools.partial(_conv_kernel, t=t, tt=tt, has_buf=has_buf),
        grid=(n,),
        in_specs=in_specs,
        out_specs=[pl.BlockSpec((None, t, C_CHANNELS), lambda b: (b, 0, 0)),
                   pl.BlockSpec((None, hist, C_CHANNELS), lambda b: (b, 0, 0))],
        out_shape=[jax.ShapeDtypeStruct((n, t, C_CHANNELS), BF16),
                   jax.ShapeDtypeStruct((n, hist, C_CHANNELS), F32)],
        scratch_shapes=[pltpu.VMEM((CONV_PAD + t, C_CHANNELS), F32)],
        compiler_params=_params(("parallel",)),
        name="conv_module",
    )(*args)


def _mem_kv_kernel(mem_ref, wk_ref, wv_ref, mk_ref, mv_ref):
    m = mem_ref[...].astype(BF16)
    mk_ref[...] = _dot(m, wk_ref[...])
    mv_ref[...] = _dot(m, wv_ref[...])


def _mem_kv(mem2d, wk, wv):
    m = mem2d.shape[0]
    tm = 512
    row = pl.BlockSpec((tm, D_MODEL), lambda i: (i, 0))
    return pl.pallas_call(
        _mem_kv_kernel,
        grid=(m // tm,),
        in_specs=[row, _const_spec((D_MODEL, D_MODEL)), _const_spec((D_MODEL, D_MODEL))],
        out_specs=[row, row],
        out_shape=[jax.ShapeDtypeStruct((m, D_MODEL), F32)] * 2,
        compiler_params=_params(("parallel",)),
        name="mem_kv",
    )(mem2d, wk, wv)


def _route(logits):
    lane = lax.broadcasted_iota(jnp.int32, logits.shape, 1).astype(F32)
    valid = lane < N_EXPERTS
    lg = jnp.where(valid, logits, NEG_INF)
    e = jnp.exp(lg - jnp.max(lg, axis=-1, keepdims=True))
    probs = e / jnp.sum(e, axis=-1, keepdims=True)
    p1 = jnp.max(probs, axis=-1, keepdims=True)
    i1 = jnp.min(jnp.where(probs == p1, lane, float(LANES)), axis=-1, keepdims=True)
    rest = jnp.where(lane == i1, -1.0, probs)
    rest = jnp.where(valid, rest, -1.0)
    p2 = jnp.max(rest, axis=-1, keepdims=True)
    i2 = jnp.min(jnp.where(rest == p2, lane, float(LANES)), axis=-1, keepdims=True)
    tot = p1 + p2
    return lane, i1, i2, p1 / tot, p2 / tot


def _post_mix_kernel(*refs, mode):
    (x_ref, oa_ref, ob_ref, oc_ref, wout_ref, gmem_ref, wq_ref, mk_ref, mv_ref, wo_ref, gffn_ref) = refs[:11]
    if mode == "dense":
        x2_ref, h_ref = refs[11:]
    else:
        router_ref, x2_ref, h_ref, route_ref = refs[11:]
    y = (_dot(oa_ref[...], wout_ref[0:D_A, :])
         + _dot(ob_ref[...], wout_ref[D_A:D_A + D_B, :])
         + _dot(oc_ref[...], wout_ref[D_A + D_B:D_MODEL, :]))
    x1 = x_ref[...] + y
    q = _dot(_rms(x1, gmem_ref[...]).astype(BF16), wq_ref[...]).astype(BF16)
    outs = []
    for h in range(X_HEADS):
        cols = slice(h * X_HEAD_DIM, (h + 1) * X_HEAD_DIM)
        s = _dot_nt(q[:, cols], mk_ref[:, cols].astype(BF16)) * (X_HEAD_DIM ** -0.5)
        e = jnp.exp(s - jnp.max(s, axis=-1, keepdims=True))
        l = jnp.sum(e, axis=-1, keepdims=True)
        outs.append((_dot(e.astype(BF16), mv_ref[:, cols].astype(BF16)) / l).astype(BF16))
    x2 = x1 + _dot(jnp.concatenate(outs, axis=1), wo_ref[...])
    x2_ref[...] = x2
    hf = _rms(x2, gffn_ref[...])
    h_hi = hf.astype(BF16)
    if mode == "grouped":
        tm = hf.shape[0]
        for c in range(TOKEN_TILE_ROWS):
            h_ref[pl.ds(c, tm, stride=TOKEN_TILE_ROWS), :] = hf[:, c * LANES:(c + 1) * LANES]
    else:
        h_ref[...] = h_hi
    if mode != "dense":
        h_lo = (hf - h_hi.astype(F32)).astype(BF16)
        logits = (_dot(h_hi, router_ref[0]) + _dot(h_lo, router_ref[0])) + _dot(h_hi, router_ref[1])
        lane, i1, i2, w1, w2 = _route(logits)
        if mode == "experts":
            route_ref[...] = jnp.where(lane == i1, w1, jnp.where(lane == i2, w2, 0.0))
        else:
            route_ref[...] = jnp.where(lane == 0.0, i1, jnp.where(lane == 1.0, i2,
                                       jnp.where(lane == 2.0, w1, jnp.where(lane == 3.0, w2, 0.0))))


def _post_mix(x, oa, ob, oc, w_out, g_mem, wq, mk, mv, wo, g_ffn, router, mode):
    n, t, _ = x.shape
    tm = min(t, 512)
    tile = lambda w: pl.BlockSpec((None, tm, w), lambda b, i: (b, i, 0))
    mem = pl.BlockSpec((None,) + mk.shape[1:], lambda b, i: (b, 0, 0))
    sq = _const_spec((D_MODEL, D_MODEL))
    vec = _const_spec((1, D_MODEL))
    in_specs = [tile(D_MODEL), tile(D_A), tile(D_B), tile(C_CHANNELS), sq, vec, sq, mem, mem, sq, vec]
    args = [x, oa, ob, oc, w_out, g_mem, wq, mk, mv, wo, g_ffn]
    out_specs = [tile(D_MODEL)]
    out_shape = [jax.ShapeDtypeStruct((n, t, D_MODEL), F32)]
    if mode == "grouped":
        out_specs.append(pl.BlockSpec((None, tm * TOKEN_TILE_ROWS, LANES), lambda b, i: (b, i, 0)))
        out_shape.append(jax.ShapeDtypeStruct((n, t * TOKEN_TILE_ROWS, LANES), F32))
    else:
        out_specs.append(tile(D_MODEL))
        out_shape.append(jax.ShapeDtypeStruct((n, t, D_MODEL), BF16))
    if mode != "dense":
        in_specs.append(_const_spec((2, D_MODEL, LANES)))
        args.append(router)
        out_specs.append(tile(LANES))
        out_shape.append(jax.ShapeDtypeStruct((n, t, LANES), F32))
    return pl.pallas_call(
        functools.partial(_post_mix_kernel, mode=mode),
        grid=(n, t // tm),
        in_specs=in_specs,
        out_specs=out_specs,
        out_shape=out_shape,
        compiler_params=_params(("parallel", "parallel")),
        name="post_mix",
    )(*args)


def _ffn_kernel(x_ref, h_ref, wg_ref, wu_ref, wd_ref, o_ref):
    h = h_ref[...]
    acc = x_ref[...]
    for c in range(D_FF // FF_CHUNK):
        cols = slice(c * FF_CHUNK, (c + 1) * FF_CHUNK)
        g = _dot(h, wg_ref[:, cols])
        u = _dot(h, wu_ref[:, cols])
        acc = acc + _dot((g * jax.nn.sigmoid(g) * u).astype(BF16), wd_ref[cols, :])
    o_ref[...] = acc


def _ffn(x2d, h2d, wg, wu, wd):
    m = x2d.shape[0]
    tm = min(m, 512)
    row = pl.BlockSpec((tm, D_MODEL), lambda i: (i, 0))
    return pl.pallas_call(
        _ffn_kernel,
        grid=(m // tm,),
        in_specs=[row, row, _const_spec((D_MODEL, D_FF)), _const_spec((D_MODEL, D_FF)),
                  _const_spec((D_FF, D_MODEL))],
        out_specs=row,
        out_shape=jax.ShapeDtypeStruct((m, D_MODEL), F32),
        compiler_params=_params(("parallel",)),
        name="ffn_dense",
    )(x2d, h2d, wg, wu, wd)


def _moe_kernel(x_ref, h_ref, comb_ref, wg_ref, wu_ref, wd_ref, gfin_ref, o_ref, acc):
    e = pl.program_id(1)

    @pl.when(e == 0)
    def _():
        acc[...] = x_ref[...]

    h = h_ref[...]
    g = _dot(h, wg_ref[...])
    u = _dot(h, wu_ref[...])
    y = _dot((g * jax.nn.sigmoid(g) * u).astype(BF16), wd_ref[...])
    comb = comb_ref[...]
    lane = lax.broadcasted_iota(jnp.int32, comb.shape, 1)
    ce = jnp.sum(jnp.where(lane == e, comb, 0.0), axis=-1, keepdims=True)
    acc[...] += ce * y

    @pl.when(e == N_EXPERTS - 1)
    def _():
        o_ref[...] = _rms(acc[...], gfin_ref[...])


def _moe(x2d, h2d, comb2d, wg, wu, wd, g_fin):
    m = x2d.shape[0]
    tm = min(m, 512)
    row = lambda w: pl.BlockSpec((tm, w), lambda i, e: (i, 0))
    return pl.pallas_call(
        _moe_kernel,
        grid=(m // tm, N_EXPERTS),
        in_specs=[row(D_MODEL), row(D_MODEL), row(LANES),
                  pl.BlockSpec((None, D_MODEL, D_FF_EXPERT), lambda i, e: (e, 0, 0)),
                  pl.BlockSpec((None, D_MODEL, D_FF_EXPERT), lambda i, e: (e, 0, 0)),
                  pl.BlockSpec((None, D_FF_EXPERT, D_MODEL), lambda i, e: (e, 0, 0)),
                  _const_spec((1, D_MODEL))],
        out_specs=row(D_MODEL),
        out_shape=jax.ShapeDtypeStruct((m, D_MODEL), F32),
        scratch_shapes=[pltpu.VMEM((tm, D_MODEL), F32)],
        compiler_params=_params(("parallel", "arbitrary")),
        name="moe_experts",
    )(x2d, h2d, comb2d, wg, wu, wd, g_fin)


def _moe_plan(route2d, m, tb):
    nb = 2 * m // tb + N_EXPERTS
    eid = jnp.concatenate([route2d[:, 0], route2d[:, 1]]).astype(jnp.int32)
    experts = jnp.arange(N_EXPERTS, dtype=jnp.int32)
    cnt = jnp.sum((eid[:, None] == experts[None, :]).astype(jnp.int32), axis=0)
    gsz = (cnt + tb - 1) // tb * tb
    gend = jnp.cumsum(gsz)
    pad_end = jnp.cumsum(gsz - cnt)
    pad_eid = jnp.sum(jnp.arange(nb * tb - 2 * m, dtype=jnp.int32)[:, None] >= pad_end[None, :], axis=1)
    keys = jnp.concatenate([eid, pad_eid.astype(jnp.int32)])
    vals = jnp.concatenate([jnp.arange(2 * m, dtype=jnp.int32), jnp.full((nb * tb - 2 * m,), -1, jnp.int32)])
    inv = lax.sort((keys, vals), num_keys=1, is_stable=True)[1]
    valid = inv >= 0
    spare = 2 * m + jnp.cumsum(jnp.logical_not(valid).astype(jnp.int32)) - 1
    gidx = jnp.where(valid, jnp.where(inv >= m, inv - m, inv), 0)
    sidx = jnp.where(valid, inv, spare)
    blk_expert = jnp.sum((jnp.arange(nb, dtype=jnp.int32) * tb)[:, None] >= gend[None, :], axis=1)
    blk_expert = jnp.minimum(blk_expert, N_EXPERTS - 1).astype(jnp.int32)
    return blk_expert, gidx.reshape(nb, 1, tb), sidx.reshape(nb, 1, tb)


def _moe_grouped_kernel(be_ref, gcur_ref, gnext_ref, scur_ref, h_hbm, wg_ref, wu_ref, wd_ref, y_hbm,
                        xbuf, ybuf, sem_in, sem_out, *, tb):
    b = pl.program_id(0)
    nb = pl.num_programs(0)
    slot = b % 2
    rows = tb * TOKEN_TILE_ROWS

    def token_rows(i):
        return pl.ds(pl.multiple_of(i * TOKEN_TILE_ROWS, TOKEN_TILE_ROWS), TOKEN_TILE_ROWS)

    def gather(idx_ref, s):
        def body(i, carry):
            for prio in range(2):
                r = 2 * i + prio
                pltpu.make_async_copy(h_hbm.at[token_rows(idx_ref[0, r]), :], xbuf.at[s, token_rows(r), :],
                                      sem_in.at[s]).start(priority=prio)
            return carry
        lax.fori_loop(0, tb // 2, body, 0, unroll=4)

    def wait_gather(s):
        pltpu.make_async_copy(h_hbm.at[pl.ds(0, rows), :], xbuf.at[s], sem_in.at[s]).wait()

    def wait_scatter(s):
        pltpu.make_async_copy(ybuf.at[s], y_hbm.at[pl.ds(0, rows), :], sem_out.at[s]).wait()

    @pl.when(b == 0)
    def _():
        gather(gcur_ref, 0)

    @pl.when(b + 1 < nb)
    def _():
        gather(gnext_ref, 1 - slot)

    wait_gather(slot)

    @pl.when(b >= 2)
    def _():
        wait_scatter(slot)

    xs = xbuf.at[slot]
    x = jnp.concatenate([xs[pl.ds(c, tb, stride=TOKEN_TILE_ROWS), :] for c in range(TOKEN_TILE_ROWS)],
                        axis=1).astype(BF16)
    g = _dot(x, wg_ref[...])
    u = _dot(x, wu_ref[...])
    y = _dot((g * jax.nn.sigmoid(g) * u).astype(BF16), wd_ref[...])
    ys = ybuf.at[slot]
    for c in range(TOKEN_TILE_ROWS):
        ys[pl.ds(c, tb, stride=TOKEN_TILE_ROWS), :] = y[:, c * LANES:(c + 1) * LANES]

    def scatter(i, carry):
        for prio in range(2):
            r = 2 * i + prio
            pltpu.make_async_copy(ybuf.at[slot, token_rows(r), :], y_hbm.at[token_rows(scur_ref[0, r]), :],
                                  sem_out.at[slot]).start(priority=prio)
        return carry
    lax.fori_loop(0, tb // 2, scatter, 0, unroll=4)

    @pl.when(b == nb - 1)
    def _():
        wait_scatter(slot)
        wait_scatter(1 - slot)


def _moe_grouped(h_tiles, blk_expert, gidx, sidx, wg, wu, wd, tb):
    nb = gidx.shape[0]
    assert nb >= 2
    rows = tb * TOKEN_TILE_ROWS
    idx_spec = lambda f: pl.BlockSpec((None, 1, tb), f, memory_space=pltpu.SMEM)
    wspec = lambda shape: pl.BlockSpec((None,) + shape, lambda b, be: (be[b], 0, 0))
    return pl.pallas_call(
        functools.partial(_moe_grouped_kernel, tb=tb),
        grid_spec=pltpu.PrefetchScalarGridSpec(
            num_scalar_prefetch=1,
            grid=(nb,),
            in_specs=[idx_spec(lambda b, be: (b, 0, 0)),
                      idx_spec(lambda b, be: (jnp.minimum(b + 1, nb - 1), 0, 0)),
                      idx_spec(lambda b, be: (b, 0, 0)),
                      pl.BlockSpec(memory_space=pl.ANY),
                      wspec((D_MODEL, D_FF_EXPERT)), wspec((D_MODEL, D_FF_EXPERT)),
                      wspec((D_FF_EXPERT, D_MODEL))],
            out_specs=pl.BlockSpec(memory_space=pl.ANY),
            scratch_shapes=[pltpu.VMEM((2, rows, LANES), F32), pltpu.VMEM((2, rows, LANES), F32),
                            pltpu.SemaphoreType.DMA((2,)), pltpu.SemaphoreType.DMA((2,))]),
        out_shape=jax.ShapeDtypeStruct((nb * rows, LANES), F32),
        compiler_params=_params(("arbitrary",)),
        name="moe_grouped",
    )(blk_expert, gidx, gidx, sidx, h_tiles, wg, wu, wd)


def _moe_combine_kernel(x_ref, route_ref, y0_ref, y1_ref, gfin_ref, o_ref):
    tm = x_ref.shape[0]
    untile = lambda ref: jnp.concatenate(
        [ref[pl.ds(c, tm, stride=TOKEN_TILE_ROWS), :] for c in range(TOKEN_TILE_ROWS)], axis=1)
    route = route_ref[...]
    acc = x_ref[...] + route[:, 2:3] * untile(y0_ref) + route[:, 3:4] * untile(y1_ref)
    o_ref[...] = _rms(acc, gfin_ref[...])


def _moe_combine(x2d, route2d, y_tiles, g_fin):
    m = x2d.shape[0]
    tm = MOE_COMBINE_TILE
    rows = tm * TOKEN_TILE_ROWS
    return pl.pallas_call(
        _moe_combine_kernel,
        grid=(m // tm,),
        in_specs=[pl.BlockSpec((tm, D_MODEL), lambda i: (i, 0)),
                  pl.BlockSpec((tm, LANES), lambda i: (i, 0)),
                  pl.BlockSpec((rows, LANES), lambda i: (i, 0)),
                  pl.BlockSpec((rows, LANES), lambda i: (i + m // tm, 0)),
                  _const_spec((1, D_MODEL))],
        out_specs=pl.BlockSpec((tm, D_MODEL), lambda i: (i, 0)),
        out_shape=jax.ShapeDtypeStruct((m, D_MODEL), F32),
        compiler_params=_params(("parallel",)),
        name="moe_combine",
    )(x2d, route2d, y_tiles, y_tiles, g_fin)


def _layer(x, pos0, attn_hist, ret_state, conv_state, mk, mv, w):
    n, t, _ = x.shape
    aq, akv, bqk, bv, bg, cab = _in_proj(x.reshape(n * t, D_MODEL), w["norm_mix_g"], w["w_in"])
    r3 = lambda a: a.reshape(n, t, a.shape[-1])
    aq, akv, bqk, bv, bg, cab = map(r3, (aq, akv, bqk, bv, bg, cab))
    oa = _band_attn(aq, akv, attn_hist, w["rel_bias"])
    ob, new_s = _retention(bqk, bv, bg, ret_state, w["ret_gn_g"], pos0)
    oc, new_conv = _conv_module(cab, conv_state, w["conv_w"], w["conv_b"], w["conv_ln_g"], w["conv_ln_b"])
    m = n * t
    if "router" not in w:
        mode = "dense"
    elif m >= N_EXPERTS * MOE_BLOCK and m % MOE_COMBINE_TILE == 0:
        mode = "grouped"
    else:
        mode = "experts"
    post = _post_mix(x, oa, ob, oc, w["w_out"], w["norm_mem_g"], w["wx_q"], mk, mv, w["wx_o"],
                     w["norm_ffn_g"], w.get("router"), mode)
    x2 = post[0].reshape(m, D_MODEL)
    if mode == "dense":
        x3 = _ffn(x2, post[1].reshape(m, D_MODEL), w["ffn_g"], w["ffn_u"], w["ffn_d"])
    elif mode == "experts":
        x3 = _moe(x2, post[1].reshape(m, D_MODEL), post[2].reshape(m, LANES),
                  w["moe_g"], w["moe_u"], w["moe_d"], w["final_g"])
    else:
        route = post[2].reshape(m, LANES)
        blk_expert, gidx, sidx = _moe_plan(route, m, MOE_BLOCK)
        y_tiles = _moe_grouped(post[1].reshape(m * TOKEN_TILE_ROWS, LANES), blk_expert, gidx, sidx,
                               w["moe_g"], w["moe_u"], w["moe_d"], MOE_BLOCK)
        x3 = _moe_combine(x2, route, y_tiles, w["final_g"])
    keep = min(A_REACH, t)
    new_kv = akv[:, t - keep:, :]
    new_k = new_kv[..., :D_A].reshape(n, keep, A_HEADS, A_HEAD_DIM)
    new_v = new_kv[..., D_A:].reshape(n, keep, A_HEADS, A_HEAD_DIM)
    return x3.reshape(n, t, D_MODEL), new_k, new_v, new_s, new_conv


def kernel(x_prompt, x_sample, cache_attn_k, cache_attn_v, state_ret, state_conv, cache_mem_k, cache_mem_v,
           mem_prompt, norm_mix_g, w_in, rel_bias, ret_gn_g, conv_w, conv_b, conv_ln_g, conv_ln_b, w_out,
           norm_mem_g, wx_q, wx_k, wx_v, wx_o, norm_ffn_g, ffn_w_gate, ffn_w_up, ffn_w_down,
           router_w, moe_w_gate, moe_w_up, moe_w_down, final_norm_g):
    depth = w_in.shape[0]
    assert depth == 2, "layer 0 dense FFN, layer 1 experts + closing norm"
    n_p, _, _ = x_prompt.shape
    n_s, t_s, _ = x_sample.shape
    n_mem = mem_prompt.shape[1]
    row = lambda a: a.reshape(1, -1).astype(F32)
    xp, xs = x_prompt, x_sample
    mem2d = mem_prompt.reshape(n_p * n_mem, D_MODEL)
    outs_p = [[] for _ in range(6)]
    outs_s = [[] for _ in range(4)]
    for l in range(depth):
        w = dict(norm_mix_g=row(norm_mix_g[l]), w_in=w_in[l].astype(BF16), rel_bias=rel_bias[l],
                 ret_gn_g=row(ret_gn_g[l]), conv_w=conv_w[l].astype(F32), conv_b=row(conv_b[l]),
                 conv_ln_g=row(conv_ln_g[l]), conv_ln_b=row(conv_ln_b[l]), w_out=w_out[l].astype(BF16),
                 norm_mem_g=row(norm_mem_g[l]), wx_q=wx_q[l].astype(BF16), wx_o=wx_o[l].astype(BF16),
                 norm_ffn_g=row(norm_ffn_g[l]))
        if l % 2 == 0:
            i = l // 2
            w.update(ffn_g=ffn_w_gate[i].astype(BF16), ffn_u=ffn_w_up[i].astype(BF16),
                     ffn_d=ffn_w_down[i].astype(BF16))
        else:
            i = l // 2
            r = jnp.pad(router_w[i].astype(F32), ((0, 0), (0, LANES - N_EXPERTS)))
            r_hi = r.astype(BF16)
            r_lo = (r - r_hi.astype(F32)).astype(BF16)
            w.update(router=jnp.stack([r_hi, r_lo]),
                     moe_g=moe_w_gate[i].astype(BF16), moe_u=moe_w_up[i].astype(BF16),
                     moe_d=moe_w_down[i].astype(BF16),
                     final_g=row(final_norm_g) if l == depth - 1 else None)
        mk, mv = _mem_kv(mem2d, wx_k[l].astype(BF16), wx_v[l].astype(BF16))
        mk = mk.reshape(n_p, n_mem, D_MODEL)
        mv = mv.reshape(n_p, n_mem, D_MODEL)
        xp, k_new, v_new, s_new, c_new = _layer(
            xp, 0.0, None, jnp.zeros((n_p, B_HEADS, B_QK_DIM, B_V_DIM), F32), None, mk, mv, w)
        for lst, a in zip(outs_p, (k_new, v_new, s_new, c_new,
                                   mk.reshape(n_p, n_mem, X_HEADS, X_HEAD_DIM),
                                   mv.reshape(n_p, n_mem, X_HEADS, X_HEAD_DIM))):
            lst.append(a)
        hist = (cache_attn_k[l].reshape(n_s, A_REACH, D_A), cache_attn_v[l].reshape(n_s, A_REACH, D_A))
        xs, k_new, v_new, s_new, c_new = _layer(
            xs, float(PAST_LEN), hist, state_ret[l], state_conv[l],
            cache_mem_k[l].reshape(n_s, n_mem, D_MODEL), cache_mem_v[l].reshape(n_s, n_mem, D_MODEL), w)
        for lst, a in zip(outs_s, (k_new, v_new, s_new, c_new)):
            lst.append(a)
    st = lambda lst: jnp.stack(lst)
    return (xp, xs, st(outs_p[0]), st(outs_p[1]), st(outs_p[2]), st(outs_p[3]), st(outs_p[4]), st(outs_p[5]),
            st(outs_s[0]), st(outs_s[1]), st(outs_s[2]), st(outs_s[3]))
```

```python
import functools

import numpy as np
import jax
import jax.numpy as jnp
from jax import lax
from jax.experimental import pallas as pl
from jax.experimental.pallas import tpu as pltpu

F32 = jnp.float32
BF16 = jnp.bfloat16

D_MODEL = 1024
PAST_LEN = 2048
CHUNK = 64
EPS = 1e-6
NEG_INF = -1e30
LANES = 128

A_HEADS = 4
A_HEAD_DIM = 64
D_A = A_HEADS * A_HEAD_DIM
A_LEFT_CHUNKS = 8
A_REACH = A_LEFT_CHUNKS * CHUNK
REL_CLIP = 128
A_SCALE = A_HEAD_DIM ** -0.5

B_HEADS = 4
B_QK_DIM = 64
B_V_DIM = 128
D_BQK = B_HEADS * B_QK_DIM
D_B = B_HEADS * B_V_DIM
RET_GAMMA_EXP0 = 5.0
ROPE_BASE = 10000.0

C_CHANNELS = 256
CONV_WIDTH = 31
CONV_PAD = 32

X_HEADS = 4
X_HEAD_DIM = D_MODEL // X_HEADS

D_FF = 11 * D_MODEL // 4
FF_CHUNK = 256
N_EXPERTS = 8
TOP_K = 2
D_FF_EXPERT = D_FF // 2
MOE_TILE = 512
MOE_CHUNK = 16
MOE_TILE_CAP = 1152
MOE_BLOCK = 512

COL_QA, COL_KVA, COL_QKB, COL_VB, COL_GB, COL_CAB, D_IN = 0, 256, 768, 1280, 1792, 2304, 2816

VMEM_LIMIT = 56 * 1024 * 1024


def _params(sem):
    return pltpu.CompilerParams(dimension_semantics=sem, vmem_limit_bytes=VMEM_LIMIT)


def _rms(x, g):
    return x * lax.rsqrt(jnp.mean(x * x, axis=-1, keepdims=True) + EPS) * g


def _dot(a, b):
    return jnp.dot(a, b, preferred_element_type=F32)


def _dot_nt(a, b):
    return lax.dot_general(a, b, (((1,), (1,)), ((), ())), preferred_element_type=F32)


def _dot_tn(a, b):
    return lax.dot_general(a, b, (((0,), (0,)), ((), ())), preferred_element_type=F32)


def _const_spec(shape):
    return pl.BlockSpec(shape, lambda *_: (0,) * len(shape))


def _in_proj_kernel(x_ref, g_ref, w_ref, aq_ref, akv_ref, bqk_ref, bv_ref, bg_ref, cab_ref):
    h = _rms(x_ref[...], g_ref[...]).astype(BF16)
    aq_ref[...] = _dot(h, w_ref[:, COL_QA:COL_KVA]).astype(BF16)
    akv_ref[...] = _dot(h, w_ref[:, COL_KVA:COL_QKB])
    bqk_ref[...] = _dot(h, w_ref[:, COL_QKB:COL_VB]).astype(BF16)
    bv_ref[...] = _dot(h, w_ref[:, COL_VB:COL_GB]).astype(BF16)
    bg_ref[...] = _dot(h, w_ref[:, COL_GB:COL_CAB]).astype(BF16)
    cab_ref[...] = _dot(h, w_ref[:, COL_CAB:D_IN])


def _in_proj(x2d, g, w_in):
    m = x2d.shape[0]
    tm = min(m, 512)
    widths = ((256, BF16), (512, F32), (512, BF16), (512, BF16), (512, BF16), (512, F32))
    return pl.pallas_call(
        _in_proj_kernel,
        grid=(m // tm,),
        in_specs=[pl.BlockSpec((tm, D_MODEL), lambda i: (i, 0)),
                  _const_spec((1, D_MODEL)),
                  _const_spec((D_MODEL, D_IN))],
        out_specs=[pl.BlockSpec((tm, w), lambda i: (i, 0)) for w, _ in widths],
        out_shape=[jax.ShapeDtypeStruct((m, w), dt) for w, dt in widths],
        compiler_params=_params(("parallel",)),
        name="in_proj",
    )(x2d, g, w_in)


def _band_attn_kernel(*refs, t, tq, has_hist):
    if has_hist:
        aq_ref, akv_ref, hk_ref, hv_ref, bias_ref, o_ref, kc, vc = refs
    else:
        aq_ref, akv_ref, bias_ref, o_ref, kc, vc = refs
    j = pl.program_id(1)
    span = A_REACH + tq

    @pl.when(j == 0)
    def _():
        if has_hist:
            kc[0:A_REACH, :] = hk_ref[...].astype(BF16)
            vc[0:A_REACH, :] = hv_ref[...].astype(BF16)
        else:
            kc[0:A_REACH, :] = jnp.zeros((A_REACH, D_A), BF16)
            vc[0:A_REACH, :] = jnp.zeros((A_REACH, D_A), BF16)
        kc[A_REACH:A_REACH + t, :] = akv_ref[:, 0:D_A].astype(BF16)
        vc[A_REACH:A_REACH + t, :] = akv_ref[:, D_A:2 * D_A].astype(BF16)

    t0 = pl.multiple_of(j * tq, tq)
    q = aq_ref[...]
    lane = lax.broadcasted_iota(jnp.int32, (tq, LANES), 1)
    if not has_hist:
        col = lax.broadcasted_iota(jnp.int32, (tq, span), 1)
        k_ok = col >= A_REACH - t0
    outs = []
    for p in range(A_HEADS // 2):
        qp = q[:, p * LANES:(p + 1) * LANES]
        kp = kc[pl.ds(t0, span), p * LANES:(p + 1) * LANES]
        vp = vc[pl.ds(t0, span), p * LANES:(p + 1) * LANES]
        o_pair = None
        for hh in range(2):
            own = (lane < A_HEAD_DIM) if hh == 0 else (lane >= A_HEAD_DIM)
            qm = jnp.where(own, qp.astype(F32) * A_SCALE, 0.0).astype(BF16)
            s = _dot_nt(qm, kp) + bias_ref[2 * p + hh]
            if not has_hist:
                s = jnp.where(k_ok, s, NEG_INF)
            m = jnp.max(s, axis=-1, keepdims=True)
            e = jnp.exp(s - m)
            l = jnp.sum(e, axis=-1, keepdims=True)
            o = _dot(e.astype(BF16), vp) / l
            o_pair = o if hh == 0 else jnp.where(own, o, o_pair)
        outs.append(o_pair)
    o_ref[...] = jnp.concatenate(outs, axis=1).astype(BF16)


def _band_bias(rel_bias_l, tq):
    span = A_REACH + tq
    period = span + tq
    n_far = A_REACH - REL_CLIP + 1
    far = rel_bias_l[:, 2 * REL_CLIP:]
    n_near = min(2 * REL_CLIP, span - n_far)
    near = jnp.flip(rel_bias_l[:, :2 * REL_CLIP], axis=1)[:, :n_near]
    beyond = jnp.broadcast_to(rel_bias_l[:, :1], (A_HEADS, span - n_far - n_near))
    u = jnp.concatenate([jnp.broadcast_to(far, (A_HEADS, n_far)), near, beyond,
                         jnp.broadcast_to(far, (A_HEADS, tq))], axis=1).astype(F32)
    skew = jnp.broadcast_to(u[:, None, :], (A_HEADS, tq, period)).reshape(A_HEADS, tq * period)
    toeplitz = skew[:, :tq * (period - 1)].reshape(A_HEADS, tq, period - 1)[:, :, :span]
    i = np.arange(tq)[:, None]
    j = np.arange(span)[None, :]
    in_band = (j // CHUNK >= i // CHUNK) & (j // CHUNK <= i // CHUNK + A_LEFT_CHUNKS)
    return jnp.where(in_band[None], toeplitz, NEG_INF)


def _band_attn(aq, akv, hist, rel_bias_l):
    n, t, _ = aq.shape
    tq = min(t, 256)
    span = A_REACH + tq
    bias = _band_bias(rel_bias_l, tq)
    has_hist = hist is not None
    in_specs = [pl.BlockSpec((None, tq, D_A), lambda b, j: (b, j, 0)),
                pl.BlockSpec((None, t, 2 * D_A), lambda b, j: (b, 0, 0))]
    args = [aq, akv]
    if has_hist:
        in_specs += [pl.BlockSpec((None, A_REACH, D_A), lambda b, j: (b, 0, 0))] * 2
        args += list(hist)
    in_specs.append(_const_spec((A_HEADS, tq, span)))
    args.append(bias)
    return pl.pallas_call(
        functools.partial(_band_attn_kernel, t=t, tq=tq, has_hist=has_hist),
        grid=(n, t // tq),
        in_specs=in_specs,
        out_specs=pl.BlockSpec((None, tq, D_A), lambda b, j: (b, j, 0)),
        out_shape=jax.ShapeDtypeStruct((n, t, D_A), BF16),
        scratch_shapes=[pltpu.VMEM((A_REACH + t, D_A), BF16)] * 2,
        compiler_params=_params(("parallel", "arbitrary")),
        name="band_attn",
    )(*args)


def _swap_halves(x):
    lane = lax.broadcasted_iota(jnp.int32, x.shape, 1)
    first = (lane % B_QK_DIM) < (B_QK_DIM // 2)
    return jnp.where(first, pltpu.roll(x, LANES - B_QK_DIM // 2, axis=1), pltpu.roll(x, B_QK_DIM // 2, axis=1))


def _retention_kernel(bqk_ref, bv_ref, bg_ref, cos_ref, sin_ref, dmask_ref, qdec_ref, kdec_ref, sdec_ref,
                      s0_ref, gn_ref, ob_ref, sfin_ref, st, *, t, bc):
    for h in range(B_HEADS):
        off = (h % 2) * B_QK_DIM
        st[h] = jnp.zeros((LANES, B_V_DIM), F32)
        st[h, off:off + B_QK_DIM, :] = s0_ref[h]

    lane = lax.broadcasted_iota(jnp.int32, (bc, LANES), 1)

    def chunk(c, carry):
        r0 = pl.multiple_of(c * bc, bc)
        rows = pl.ds(r0, bc)
        cs = cos_ref[rows, :]
        sn = sin_ref[rows, :]
        for p in range(B_HEADS // 2):
            qx = bqk_ref[rows, p * LANES:(p + 1) * LANES].astype(F32)
            kx = bqk_ref[rows, D_BQK + p * LANES:D_BQK + (p + 1) * LANES].astype(F32)
            qr = qx * cs + _swap_halves(qx) * sn
            kr = (kx * cs + _swap_halves(kx) * sn) * (B_QK_DIM ** -0.5)
            for hh in range(2):
                h = 2 * p + hh
                own = (lane < B_QK_DIM) if hh == 0 else (lane >= B_QK_DIM)
                qh = jnp.where(own, qr, 0.0)
                kh = jnp.where(own, kr, 0.0)
                v = bv_ref[rows, h * B_V_DIM:(h + 1) * B_V_DIM]
                att = _dot_nt(qh.astype(BF16), kh.astype(BF16)) * dmask_ref[h]
                o = (_dot(att.astype(BF16), v)
                     + _dot((qh * qdec_ref[h]).astype(BF16), st[h].astype(BF16)))
                st[h] = st[h] * sdec_ref[h] + _dot_tn((kh * kdec_ref[h]).astype(BF16), v)
                mu = jnp.mean(o, axis=-1, keepdims=True)
                d = o - mu
                var = jnp.mean(d * d, axis=-1, keepdims=True)
                on = d * lax.rsqrt(var + EPS) * gn_ref[:, h * B_V_DIM:(h + 1) * B_V_DIM]
                g = bg_ref[rows, h * B_V_DIM:(h + 1) * B_V_DIM].astype(F32)
                ob_ref[rows, h * B_V_DIM:(h + 1) * B_V_DIM] = (g * jax.nn.sigmoid(g) * on).astype(BF16)
        return carry

    lax.fori_loop(0, t // bc, chunk, 0)
    for h in range(B_HEADS):
        off = (h % 2) * B_QK_DIM
        sfin_ref[h] = st[h, off:off + B_QK_DIM, :]


def _retention_tables(t, bc, pos0):
    log_g = jnp.log(1.0 - 2.0 ** (-(RET_GAMMA_EXP0 + jnp.arange(B_HEADS, dtype=F32))))
    i = jnp.arange(bc, dtype=F32)
    diff = i[:, None] - i[None, :]
    dmask = jnp.where(diff[None] >= 0, jnp.exp(jnp.maximum(diff, 0.0)[None] * log_g[:, None, None]), 0.0)
    qdec = jnp.exp((i[None, :] + 1.0) * log_g[:, None])
    kdec = jnp.exp((bc - 1.0 - i)[None, :] * log_g[:, None])
    sdec = jnp.exp(bc * log_g)
    qdec = jnp.broadcast_to(qdec[:, :, None], (B_HEADS, bc, LANES))
    kdec = jnp.broadcast_to(kdec[:, :, None], (B_HEADS, bc, LANES))
    sdec = jnp.broadcast_to(sdec[:, None, None], (B_HEADS, 1, B_V_DIM))
    half = B_QK_DIM // 2
    pos = pos0 + jnp.arange(t, dtype=F32)
    inv_freq = ROPE_BASE ** (-jnp.arange(half, dtype=F32) / half)
    ang = pos[:, None] * inv_freq[None, :]
    cos = jnp.tile(jnp.cos(ang), (1, LANES // half))
    sin = jnp.tile(jnp.concatenate([-jnp.sin(ang), jnp.sin(ang)], axis=1), (1, LANES // B_QK_DIM))
    return cos, sin, dmask, qdec, kdec, sdec


def _retention(bqk, bv, bg, s0, gn_g, pos0):
    n, t, _ = bqk.shape
    bc = min(t, 256)
    tables = _retention_tables(t, bc, pos0)
    seq = lambda w: pl.BlockSpec((None, t, w), lambda b: (b, 0, 0))
    state = pl.BlockSpec((None, B_HEADS, B_QK_DIM, B_V_DIM), lambda b: (b, 0, 0, 0))
    return pl.pallas_call(
        functools.partial(_retention_kernel, t=t, bc=bc),
        grid=(n,),
        in_specs=[seq(2 * D_BQK), seq(D_B), seq(D_B)] + [_const_spec(tb.shape) for tb in tables]
                 + [state, _const_spec((1, D_B))],
        out_specs=[seq(D_B), state],
        out_shape=[jax.ShapeDtypeStruct((n, t, D_B), BF16),
                   jax.ShapeDtypeStruct((n, B_HEADS, B_QK_DIM, B_V_DIM), F32)],
        scratch_shapes=[pltpu.VMEM((B_HEADS, LANES, B_V_DIM), F32)],
        compiler_params=_params(("parallel",)),
        name="retention",
    )(bqk, bv, bg, *tables, s0, gn_g)


def _conv_kernel(*refs, t, tt, has_buf):
    if has_buf:
        cab_ref, buf_ref, w_ref, b_ref, lng_ref, lnb_ref, oc_ref, nc_ref, up = refs
    else:
        cab_ref, w_ref, b_ref, lng_ref, lnb_ref, oc_ref, nc_ref, up = refs
    hist = CONV_WIDTH - 1
    up[0:CONV_PAD, :] = jnp.zeros((CONV_PAD, C_CHANNELS), F32)
    if has_buf:
        up[CONV_PAD - hist:CONV_PAD, :] = buf_ref[...]

    def glu(i, carry):
        rows = pl.ds(pl.multiple_of(i * tt, tt), tt)
        ca = cab_ref[rows, 0:C_CHANNELS]
        cb = cab_ref[rows, C_CHANNELS:2 * C_CHANNELS]
        up[pl.ds(pl.multiple_of(CONV_PAD + i * tt, 8), tt), :] = ca * jax.nn.sigmoid(cb)
        return carry

    lax.fori_loop(0, t // tt, glu, 0)

    def tile(i, carry):
        t0 = pl.multiple_of(i * tt, tt)
        ext = up[pl.ds(t0, tt + CONV_PAD), :]
        acc = jnp.zeros((tt, C_CHANNELS), F32)
        for b in range(8):
            rb = ext if b == 0 else pltpu.roll(ext, b, axis=0)
            for a in range(CONV_PAD // 8):
                d = 8 * a + b
                if d > hist:
                    continue
                k = hist - d
                acc = acc + w_ref[k:k + 1, :] * rb[CONV_PAD - 8 * a:CONV_PAD - 8 * a + tt, :]
        y = acc + b_ref[...]
        mu = jnp.mean(y, axis=-1, keepdims=True)
        d0 = y - mu
        var = jnp.mean(d0 * d0, axis=-1, keepdims=True)
        yn = d0 * lax.rsqrt(var + EPS) * lng_ref[...] + lnb_ref[...]
        oc_ref[pl.ds(t0, tt), :] = (yn * jax.nn.sigmoid(yn)).astype(BF16)
        return carry

    lax.fori_loop(0, t // tt, tile, 0)
    nc_ref[...] = up[CONV_PAD + t - hist:CONV_PAD + t, :]


def _conv_module(cab, buf, conv_w, conv_b, ln_g, ln_b):
    n, t, _ = cab.shape
    tt = min(t, 128)
    has_buf = buf is not None
    hist = CONV_WIDTH - 1
    in_specs = [pl.BlockSpec((None, t, 2 * C_CHANNELS), lambda b: (b, 0, 0))]
    args = [cab]
    if has_buf:
        in_specs.append(pl.BlockSpec((None, hist, C_CHANNELS), lambda b: (b, 0, 0)))
        args.append(buf)
    in_specs += [_const_spec((CONV_WIDTH, C_CHANNELS))] + [_const_spec((1, C_CHANNELS))] * 3
    args += [conv_w, conv_b, ln_g, ln_b]
    return pl.pallas_call(
        functools.partial(_conv_kernel, t=t, tt=tt, has_buf=has_buf),
        grid=(n,),
        in_specs=in_specs,
        out_specs=[pl.BlockSpec((None, t, C_CHANNELS), lambda b: (b, 0, 0)),
                   pl.BlockSpec((None, hist, C_CHANNELS), lambda b: (b, 0, 0))],
        out_shape=[jax.ShapeDtypeStruct((n, t, C_CHANNELS), BF16),
                   jax.ShapeDtypeStruct((n, hist, C_CHANNELS), F32)],
        scratch_shapes=[pltpu.VMEM((CONV_PAD + t, C_CHANNELS), F32)],
        compiler_params=_params(("parallel",)),
        name="conv_module",
    )(*args)


def _mem_kv_kernel(mem_ref, wk_ref, wv_ref, mk_ref, mv_ref):
    m = mem_ref[...].astype(BF16)
    mk_ref[...] = _dot(m, wk_ref[...])
    mv_ref[...] = _dot(m, wv_ref[...])


def _mem_kv(mem2d, wk, wv):
    m = mem2d.shape[0]
    tm = 512
    row = pl.BlockSpec((tm, D_MODEL), lambda i: (i, 0))
    return pl.pallas_call(
        _mem_kv_kernel,
        grid=(m // tm,),
        in_specs=[row, _const_spec((D_MODEL, D_MODEL)), _const_spec((D_MODEL, D_MODEL))],
        out_specs=[row, row],
        out_shape=[jax.ShapeDtypeStruct((m, D_MODEL), F32)] * 2,
        compiler_params=_params(("parallel",)),
        name="mem_kv",
    )(mem2d, wk, wv)


def _route(logits):
    lane = lax.broadcasted_iota(jnp.int32, logits.shape, 1).astype(F32)
    valid = lane < N_EXPERTS
    lg = jnp.where(valid, logits, NEG_INF)
    e = jnp.exp(lg - jnp.max(lg, axis=-1, keepdims=True))
    probs = e / jnp.sum(e, axis=-1, keepdims=True)
    p1 = jnp.max(probs, axis=-1, keepdims=True)
    i1 = jnp.min(jnp.where(probs == p1, lane, float(LANES)), axis=-1, keepdims=True)
    rest = jnp.where(lane == i1, -1.0, probs)
    rest = jnp.where(valid, rest, -1.0)
    p2 = jnp.max(rest, axis=-1, keepdims=True)
    i2 = jnp.min(jnp.where(rest == p2, lane, float(LANES)), axis=-1, keepdims=True)
    tot = p1 + p2
    return lane, i1, i2, p1 / tot, p2 / tot


def _sort_tile_by_expert(logits, h_hi, tri_ref, hs_ref, route_ref, meta_ref):
    tm = logits.shape[0]
    lt = jnp.transpose(logits)[0:N_EXPERTS, :]
    sub = lax.broadcasted_iota(jnp.int32, (N_EXPERTS, tm), 0).astype(F32)
    ex = jnp.exp(lt - jnp.max(lt, axis=0, keepdims=True))
    probs = ex / jnp.sum(ex, axis=0, keepdims=True)
    p1 = jnp.max(probs, axis=0, keepdims=True)
    i1 = jnp.min(jnp.where(probs == p1, sub, float(N_EXPERTS)), axis=0, keepdims=True)
    rest = jnp.where(sub == i1, -1.0, probs)
    p2 = jnp.max(rest, axis=0, keepdims=True)
    i2 = jnp.min(jnp.where(rest == p2, sub, float(N_EXPERTS)), axis=0, keepdims=True)
    w1 = p1 / (p1 + p2)
    w2 = p2 / (p1 + p2)
    oh1 = jnp.where(sub == i1, 1.0, 0.0)
    oh2 = jnp.where(sub == i2, 1.0, 0.0)
    before1 = _dot(oh1.astype(BF16), tri_ref[...])
    before2 = _dot(oh2.astype(BF16), tri_ref[...])
    cnt1 = jnp.sum(oh1, axis=1, keepdims=True)
    cnt2 = jnp.sum(oh2, axis=1, keepdims=True)
    padded = jnp.floor((cnt1 + cnt2 + (MOE_CHUNK - 1.0)) * (1.0 / MOE_CHUNK)) * MOE_CHUNK
    run = jnp.broadcast_to(padded, (N_EXPERTS, LANES))
    sub8 = lax.broadcasted_iota(jnp.int32, (N_EXPERTS, LANES), 0)
    incl = run
    for k in (1, 2, 4):
        incl = incl + jnp.where(sub8 >= k, pltpu.roll(incl, k, axis=0), 0.0)
    start = (incl - run)[:, 0:1]
    d1 = jnp.sum(oh1 * (start + before1), axis=0, keepdims=True)
    d2 = jnp.sum(oh2 * (start + cnt1 + before2), axis=0, keepdims=True)
    row = lax.broadcasted_iota(jnp.int32, (MOE_TILE_CAP, tm), 0).astype(F32)
    perm = jnp.where(row == d1, 1.0, jnp.where(row == d2, 1.0, 0.0)).astype(BF16)
    hs_ref[...] = _dot(perm, h_hi).astype(BF16)
    info = jnp.concatenate([d1, d2, w1, w2, jnp.zeros((LANES - 4, tm), F32)], axis=0)
    route_ref[...] = jnp.transpose(info)
    lane8 = lax.broadcasted_iota(jnp.int32, (N_EXPERTS, LANES), 1)
    meta_ref[...] = jnp.where(lane8 == 0, start * (1.0 / MOE_CHUNK),
                              jnp.where(lane8 == 1, padded * (1.0 / MOE_CHUNK), 0.0))


def _post_mix_kernel(*refs, mode):
    (x_ref, oa_ref, ob_ref, oc_ref, wout_ref, gmem_ref, wq_ref, mk_ref, mv_ref, wo_ref, gffn_ref) = refs[:11]
    if mode == "dense":
        x2_ref, h_ref = refs[11:]
    elif mode == "experts":
        router_ref, x2_ref, h_ref, route_ref = refs[11:]
    else:
        router_ref, tri_ref, x2_ref, h_ref, route_ref, meta_ref = refs[11:]
    y = (_dot(oa_ref[...], wout_ref[0:D_A, :])
         + _dot(ob_ref[...], wout_ref[D_A:D_A + D_B, :])
         + _dot(oc_ref[...], wout_ref[D_A + D_B:D_MODEL, :]))
    x1 = x_ref[...] + y
    q = _dot(_rms(x1, gmem_ref[...]).astype(BF16), wq_ref[...]).astype(BF16)
    outs = []
    for h in range(X_HEADS):
        cols = slice(h * X_HEAD_DIM, (h + 1) * X_HEAD_DIM)
        s = _dot_nt(q[:, cols], mk_ref[:, cols].astype(BF16)) * (X_HEAD_DIM ** -0.5)
        e = jnp.exp(s - jnp.max(s, axis=-1, keepdims=True))
        l = jnp.sum(e, axis=-1, keepdims=True)
        outs.append((_dot(e.astype(BF16), mv_ref[:, cols].astype(BF16)) / l).astype(BF16))
    x2 = x1 + _dot(jnp.concatenate(outs, axis=1), wo_ref[...])
    x2_ref[...] = x2
    hf = _rms(x2, gffn_ref[...])
    h_hi = hf.astype(BF16)
    if mode != "grouped":
        h_ref[...] = h_hi
    if mode != "dense":
        h_lo = (hf - h_hi.astype(F32)).astype(BF16)
        logits = (_dot(h_hi, router_ref[0]) + _dot(h_lo, router_ref[0])) + _dot(h_hi, router_ref[1])
        if mode == "experts":
            lane, i1, i2, w1, w2 = _route(logits)
            route_ref[...] = jnp.where(lane == i1, w1, jnp.where(lane == i2, w2, 0.0))
        else:
            _sort_tile_by_expert(logits, h_hi, tri_ref, h_ref, route_ref, meta_ref)


def _post_mix(x, oa, ob, oc, w_out, g_mem, wq, mk, mv, wo, g_ffn, router, mode):
    n, t, _ = x.shape
    tm = min(t, 512)
    tile = lambda w: pl.BlockSpec((None, tm, w), lambda b, i: (b, i, 0))
    mem = pl.BlockSpec((None,) + mk.shape[1:], lambda b, i: (b, 0, 0))
    sq = _const_spec((D_MODEL, D_MODEL))
    vec = _const_spec((1, D_MODEL))
    in_specs = [tile(D_MODEL), tile(D_A), tile(D_B), tile(C_CHANNELS), sq, vec, sq, mem, mem, sq, vec]
    args = [x, oa, ob, oc, w_out, g_mem, wq, mk, mv, wo, g_ffn]
    out_specs = [tile(D_MODEL)]
    out_shape = [jax.ShapeDtypeStruct((n, t, D_MODEL), F32)]
    if mode != "dense":
        in_specs.append(_const_spec((2, D_MODEL, LANES)))
        args.append(router)
    if mode == "grouped":
        assert tm == MOE_TILE
        in_specs.append(_const_spec((tm, tm)))
        args.append(jnp.triu(jnp.ones((tm, tm), BF16), k=1))
        out_specs.append(pl.BlockSpec((None, MOE_TILE_CAP, D_MODEL), lambda b, i: (b, i, 0)))
        out_shape.append(jax.ShapeDtypeStruct((n, t // tm * MOE_TILE_CAP, D_MODEL), BF16))
    else:
        out_specs.append(tile(D_MODEL))
        out_shape.append(jax.ShapeDtypeStruct((n, t, D_MODEL), BF16))
    if mode != "dense":
        out_specs.append(tile(LANES))
        out_shape.append(jax.ShapeDtypeStruct((n, t, LANES), F32))
    if mode == "grouped":
        out_specs.append(pl.BlockSpec((None, N_EXPERTS, LANES), lambda b, i: (b, i, 0)))
        out_shape.append(jax.ShapeDtypeStruct((n, t // tm * N_EXPERTS, LANES), F32))
    return pl.pallas_call(
        functools.partial(_post_mix_kernel, mode=mode),
        grid=(n, t // tm),
        in_specs=in_specs,
        out_specs=out_specs,
        out_shape=out_shape,
        compiler_params=_params(("parallel", "parallel")),
        name="post_mix",
    )(*args)


def _ffn_kernel(x_ref, h_ref, wg_ref, wu_ref, wd_ref, o_ref):
    h = h_ref[...]
    acc = x_ref[...]
    for c in range(D_FF // FF_CHUNK):
        cols = slice(c * FF_CHUNK, (c + 1) * FF_CHUNK)
        g = _dot(h, wg_ref[:, cols])
        u = _dot(h, wu_ref[:, cols])
        acc = acc + _dot((g * jax.nn.sigmoid(g) * u).astype(BF16), wd_ref[cols, :])
    o_ref[...] = acc


def _ffn(x2d, h2d, wg, wu, wd):
    m = x2d.shape[0]
    tm = min(m, 512)
    row = pl.BlockSpec((tm, D_MODEL), lambda i: (i, 0))
    return pl.pallas_call(
        _ffn_kernel,
        grid=(m // tm,),
        in_specs=[row, row, _const_spec((D_MODEL, D_FF)), _const_spec((D_MODEL, D_FF)),
                  _const_spec((D_FF, D_MODEL))],
        out_specs=row,
        out_shape=jax.ShapeDtypeStruct((m, D_MODEL), F32),
        compiler_params=_params(("parallel",)),
        name="ffn_dense",
    )(x2d, h2d, wg, wu, wd)


def _moe_kernel(x_ref, h_ref, comb_ref, wg_ref, wu_ref, wd_ref, gfin_ref, o_ref, acc):
    e = pl.program_id(1)

    @pl.when(e == 0)
    def _():
        acc[...] = x_ref[...]

    h = h_ref[...]
    g = _dot(h, wg_ref[...])
    u = _dot(h, wu_ref[...])
    y = _dot((g * jax.nn.sigmoid(g) * u).astype(BF16), wd_ref[...])
    comb = comb_ref[...]
    lane = lax.broadcasted_iota(jnp.int32, comb.shape, 1)
    ce = jnp.sum(jnp.where(lane == e, comb, 0.0), axis=-1, keepdims=True)
    acc[...] += ce * y

    @pl.when(e == N_EXPERTS - 1)
    def _():
        o_ref[...] = _rms(acc[...], gfin_ref[...])


def _moe(x2d, h2d, comb2d, wg, wu, wd, g_fin):
    m = x2d.shape[0]
    tm = min(m, 512)
    row = lambda w: pl.BlockSpec((tm, w), lambda i, e: (i, 0))
    return pl.pallas_call(
        _moe_kernel,
        grid=(m // tm, N_EXPERTS),
        in_specs=[row(D_MODEL), row(D_MODEL), row(LANES),
                  pl.BlockSpec((None, D_MODEL, D_FF_EXPERT), lambda i, e: (e, 0, 0)),
                  pl.BlockSpec((None, D_MODEL, D_FF_EXPERT), lambda i, e: (e, 0, 0)),
                  pl.BlockSpec((None, D_FF_EXPERT, D_MODEL), lambda i, e: (e, 0, 0)),
                  _const_spec((1, D_MODEL))],
        out_specs=row(D_MODEL),
        out_shape=jax.ShapeDtypeStruct((m, D_MODEL), F32),
        scratch_shapes=[pltpu.VMEM((tm, D_MODEL), F32)],
        compiler_params=_params(("parallel", "arbitrary")),
        name="moe_experts",
    )(x2d, h2d, comb2d, wg, wu, wd, g_fin)


def _moe_plan(meta, n_tiles):
    cpb = MOE_BLOCK // MOE_CHUNK
    tile_chunks = MOE_TILE_CAP // MOE_CHUNK
    nb = -(-(n_tiles * tile_chunks + N_EXPERTS * (cpb - 1)) // cpb)
    first = meta[:, :, 0].astype(jnp.int32)
    cnt = meta[:, :, 1].astype(jnp.int32)
    tot = jnp.sum(cnt, axis=0)
    eend = jnp.cumsum((tot + cpb - 1) // cpb * cpb)
    q = jnp.arange(nb * cpb, dtype=jnp.int32)
    e_q = jnp.minimum(jnp.sum((q[:, None] >= eend[None, :]).astype(jnp.int32), axis=1), N_EXPERTS - 1)
    oh_e = (e_q[:, None] == jnp.arange(N_EXPERTS, dtype=jnp.int32)[None, :]).astype(jnp.int32)
    pick = lambda table: jnp.sum(oh_e[:, :, None] * table.T[None, :, :], axis=1)
    j = q - jnp.sum(oh_e * (eend - (tot + cpb - 1) // cpb * cpb)[None, :], axis=1)
    valid = (j < jnp.sum(oh_e * tot[None, :], axis=1)) & (q < eend[N_EXPERTS - 1])
    incl = jnp.cumsum(cnt, axis=0)
    tile_q = jnp.minimum(jnp.sum((j[:, None] >= pick(incl)).astype(jnp.int32), axis=1), n_tiles - 1)
    oh_t = (tile_q[:, None] == jnp.arange(n_tiles, dtype=jnp.int32)[None, :]).astype(jnp.int32)
    chunk = (tile_q * tile_chunks + jnp.sum(oh_t * pick(first), axis=1)
             + j - jnp.sum(oh_t * pick(incl - cnt), axis=1))
    k = jnp.cumsum(jnp.logical_not(valid).astype(jnp.int32)) - 1
    used = jnp.sum(cnt, axis=1)
    free_incl = jnp.cumsum(tile_chunks - used)
    tile_k = jnp.minimum(jnp.sum((k[:, None] >= free_incl[None, :]).astype(jnp.int32), axis=1), n_tiles - 1)
    oh_k = (tile_k[:, None] == jnp.arange(n_tiles, dtype=jnp.int32)[None, :]).astype(jnp.int32)
    in_tile = (tile_k * tile_chunks + jnp.sum(oh_k * used[None, :], axis=1)
               + k - jnp.sum(oh_k * (free_incl - (tile_chunks - used))[None, :], axis=1))
    n_free = free_incl[n_tiles - 1]
    spare = jnp.where(k < n_free, in_tile, n_tiles * tile_chunks + k - n_free)
    src = jnp.where(valid, chunk, 0)
    dst = jnp.where(valid, chunk, spare)
    blk_expert = e_q[::cpb]
    n_active = (eend[N_EXPERTS - 1] // cpb).reshape(1)
    return blk_expert, n_active, src.reshape(nb, 1, cpb), dst.reshape(nb, 1, cpb)


def _moe_grouped_kernel(be_ref, nact_ref, src_ref, src_next_ref, dst_ref, hs_hbm, wg_ref, wu_ref, wd_ref,
                        y_hbm, xbuf, ybuf, sem_in, sem_out):
    b = pl.program_id(0)
    nb = pl.num_programs(0)
    slot = b % 2
    cpb = MOE_BLOCK // MOE_CHUNK

    def chunk_rows(i):
        return pl.ds(pl.multiple_of(i * MOE_CHUNK, MOE_CHUNK), MOE_CHUNK)

    def gather(idx_ref, s):
        for c in range(cpb):
            pltpu.make_async_copy(hs_hbm.at[chunk_rows(idx_ref[0, c]), :],
                                  xbuf.at[s, c * MOE_CHUNK:(c + 1) * MOE_CHUNK, :],
                                  sem_in.at[s]).start(priority=c % 2)

    def wait_gather(s):
        pltpu.make_async_copy(hs_hbm.at[pl.ds(0, MOE_BLOCK), :], xbuf.at[s], sem_in.at[s]).wait()

    def wait_scatter(s):
        pltpu.make_async_copy(ybuf.at[s], y_hbm.at[pl.ds(0, MOE_BLOCK), :], sem_out.at[s]).wait()

    @pl.when(b == 0)
    def _():
        gather(src_ref, 0)

    @pl.when(b + 1 < nb)
    def _():
        gather(src_next_ref, 1 - slot)

    wait_gather(slot)

    @pl.when(b >= 2)
    def _():
        wait_scatter(slot)

    @pl.when(b < nact_ref[0])
    def _():
        x = xbuf[slot]
        g = _dot(x, wg_ref[...])
        u = _dot(x, wu_ref[...])
        ybuf[slot] = _dot((g * jax.nn.sigmoid(g) * u).astype(BF16), wd_ref[...]).astype(BF16)

    @pl.when(b >= nact_ref[0])
    def _():
        ybuf[slot] = jnp.zeros((MOE_BLOCK, D_MODEL), BF16)

    for c in range(cpb):
        pltpu.make_async_copy(ybuf.at[slot, c * MOE_CHUNK:(c + 1) * MOE_CHUNK, :],
                              y_hbm.at[chunk_rows(dst_ref[0, c]), :],
                              sem_out.at[slot]).start(priority=c % 2)

    @pl.when(b == nb - 1)
    def _():
        wait_scatter(slot)
        wait_scatter(1 - slot)


def _moe_grouped(hs, blk_expert, n_active, src, dst, wg, wu, wd):
    nb, _, cpb = src.shape
    assert nb >= 2
    idx_spec = lambda f: pl.BlockSpec((None, 1, cpb), f, memory_space=pltpu.SMEM)
    wspec = lambda shape: pl.BlockSpec((None,) + shape, lambda b, be, na: (be[b], 0, 0))
    return pl.pallas_call(
        _moe_grouped_kernel,
        grid_spec=pltpu.PrefetchScalarGridSpec(
            num_scalar_prefetch=2,
            grid=(nb,),
            in_specs=[idx_spec(lambda b, be, na: (b, 0, 0)),
                      idx_spec(lambda b, be, na: (jnp.minimum(b + 1, nb - 1), 0, 0)),
                      idx_spec(lambda b, be, na: (b, 0, 0)),
                      pl.BlockSpec(memory_space=pl.ANY),
                      wspec((D_MODEL, D_FF_EXPERT)), wspec((D_MODEL, D_FF_EXPERT)),
                      wspec((D_FF_EXPERT, D_MODEL))],
            out_specs=pl.BlockSpec(memory_space=pl.ANY),
            scratch_shapes=[pltpu.VMEM((2, MOE_BLOCK, D_MODEL), BF16), pltpu.VMEM((2, MOE_BLOCK, D_MODEL), BF16),
                            pltpu.SemaphoreType.DMA((2,)), pltpu.SemaphoreType.DMA((2,))]),
        out_shape=jax.ShapeDtypeStruct((nb * MOE_BLOCK, D_MODEL), BF16),
        compiler_params=_params(("arbitrary",)),
        name="moe_grouped",
    )(blk_expert, n_active, src, src, dst, hs, wg, wu, wd)


def _moe_combine_kernel(x_ref, route_ref, y_ref, gfin_ref, o_ref):
    tm = x_ref.shape[0]
    route = route_ref[...]
    d1, d2, w1, w2 = route[:, 0:1], route[:, 1:2], route[:, 2:3], route[:, 3:4]
    col = lax.broadcasted_iota(jnp.int32, (tm, MOE_TILE_CAP), 1).astype(F32)
    wm = jnp.where(col == d1, w1, jnp.where(col == d2, w2, 0.0))
    wm_hi = wm.astype(BF16)
    wm_lo = (wm - wm_hi.astype(F32)).astype(BF16)
    y = y_ref[...]
    acc = x_ref[...] + (_dot(wm_hi, y) + _dot(wm_lo, y))
    o_ref[...] = _rms(acc, gfin_ref[...])


def _moe_combine(x2d, route2d, ys, g_fin):
    m = x2d.shape[0]
    tm = MOE_TILE
    return pl.pallas_call(
        _moe_combine_kernel,
        grid=(m // tm,),
        in_specs=[pl.BlockSpec((tm, D_MODEL), lambda i: (i, 0)),
                  pl.BlockSpec((tm, LANES), lambda i: (i, 0)),
                  pl.BlockSpec((MOE_TILE_CAP, D_MODEL), lambda i: (i, 0)),
                  _const_spec((1, D_MODEL))],
        out_specs=pl.BlockSpec((tm, D_MODEL), lambda i: (i, 0)),
        out_shape=jax.ShapeDtypeStruct((m, D_MODEL), F32),
        compiler_params=_params(("parallel",)),
        name="moe_combine",
    )(x2d, route2d, ys, g_fin)


def _layer(x, pos0, attn_hist, ret_state, conv_state, mk, mv, w):
    n, t, _ = x.shape
    aq, akv, bqk, bv, bg, cab = _in_proj(x.reshape(n * t, D_MODEL), w["norm_mix_g"], w["w_in"])
    r3 = lambda a: a.reshape(n, t, a.shape[-1])
    aq, akv, bqk, bv, bg, cab = map(r3, (aq, akv, bqk, bv, bg, cab))
    oa = _band_attn(aq, akv, attn_hist, w["rel_bias"])
    ob, new_s = _retention(bqk, bv, bg, ret_state, w["ret_gn_g"], pos0)
    oc, new_conv = _conv_module(cab, conv_state, w["conv_w"], w["conv_b"], w["conv_ln_g"], w["conv_ln_b"])
    m = n * t
    if "router" not in w:
        mode = "dense"
    elif m >= N_EXPERTS * MOE_BLOCK and t % MOE_TILE == 0:
        mode = "grouped"
    else:
        mode = "experts"
    post = _post_mix(x, oa, ob, oc, w["w_out"], w["norm_mem_g"], w["wx_q"], mk, mv, w["wx_o"],
                     w["norm_ffn_g"], w.get("router"), mode)
    x2 = post[0].reshape(m, D_MODEL)
    if mode == "dense":
        x3 = _ffn(x2, post[1].reshape(m, D_MODEL), w["ffn_g"], w["ffn_u"], w["ffn_d"])
    elif mode == "experts":
        x3 = _moe(x2, post[1].reshape(m, D_MODEL), post[2].reshape(m, LANES),
                  w["moe_g"], w["moe_u"], w["moe_d"], w["final_g"])
    else:
        n_tiles = m // MOE_TILE
        blk_expert, n_active, src, dst = _moe_plan(post[3].reshape(n_tiles, N_EXPERTS, LANES), n_tiles)
        ys = _moe_grouped(post[1].reshape(n_tiles * MOE_TILE_CAP, D_MODEL), blk_expert, n_active, src, dst,
                          w["moe_g"], w["moe_u"], w["moe_d"])
        x3 = _moe_combine(x2, post[2].reshape(m, LANES), ys, w["final_g"])
    keep = min(A_REACH, t)
    new_kv = akv[:, t - keep:, :]
    new_k = new_kv[..., :D_A].reshape(n, keep, A_HEADS, A_HEAD_DIM)
    new_v = new_kv[..., D_A:].reshape(n, keep, A_HEADS, A_HEAD_DIM)
    return x3.reshape(n, t, D_MODEL), new_k, new_v, new_s, new_conv


def kernel(x_prompt, x_sample, cache_attn_k, cache_attn_v, state_ret, state_conv, cache_mem_k, cache_mem_v,
           mem_prompt, norm_mix_g, w_in, rel_bias, ret_gn_g, conv_w, conv_b, conv_ln_g, conv_ln_b, w_out,
           norm_mem_g, wx_q, wx_k, wx_v, wx_o, norm_ffn_g, ffn_w_gate, ffn_w_up, ffn_w_down,
           router_w, moe_w_gate, moe_w_up, moe_w_down, final_norm_g):
    depth = w_in.shape[0]
    assert depth == 2, "layer 0 dense FFN, layer 1 experts + closing norm"
    n_p, _, _ = x_prompt.shape
    n_s, t_s, _ = x_sample.shape
    n_mem = mem_prompt.shape[1]
    row = lambda a: a.reshape(1, -1).astype(F32)
    xp, xs = x_prompt, x_sample
    mem2d = mem_prompt.reshape(n_p * n_mem, D_MODEL)
    outs_p = [[] for _ in range(6)]
    outs_s = [[] for _ in range(4)]
    for l in range(depth):
        w = dict(norm_mix_g=row(norm_mix_g[l]), w_in=w_in[l].astype(BF16), rel_bias=rel_bias[l],
                 ret_gn_g=row(ret_gn_g[l]), conv_w=conv_w[l].astype(F32), conv_b=row(conv_b[l]),
                 conv_ln_g=row(conv_ln_g[l]), conv_ln_b=row(conv_ln_b[l]), w_out=w_out[l].astype(BF16),
                 norm_mem_g=row(norm_mem_g[l]), wx_q=wx_q[l].astype(BF16), wx_o=wx_o[l].astype(BF16),
                 norm_ffn_g=row(norm_ffn_g[l]))
        if l % 2 == 0:
            i = l // 2
            w.update(ffn_g=ffn_w_gate[i].astype(BF16), ffn_u=ffn_w_up[i].astype(BF16),
                     ffn_d=ffn_w_down[i].astype(BF16))
        else:
            i = l // 2
            r = jnp.pad(router_w[i].astype(F32), ((0, 0), (0, LANES - N_EXPERTS)))
            r_hi = r.astype(BF16)
            r_lo = (r - r_hi.astype(F32)).astype(BF16)
            w.update(router=jnp.stack([r_hi, r_lo]),
                     moe_g=moe_w_gate[i].astype(BF16), moe_u=moe_w_up[i].astype(BF16),
                     moe_d=moe_w_down[i].astype(BF16),
                     final_g=row(final_norm_g) if l == depth - 1 else None)
        mk, mv = _mem_kv(mem2d, wx_k[l].astype(BF16), wx_v[l].astype(BF16))
        mk = mk.reshape(n_p, n_mem, D_MODEL)
        mv = mv.reshape(n_p, n_mem, D_MODEL)
        xp, k_new, v_new, s_new, c_new = _layer(
            xp, 0.0, None, jnp.zeros((n_p, B_HEADS, B_QK_DIM, B_V_DIM), F32), None, mk, mv, w)
        for lst, a in zip(outs_p, (k_new, v_new, s_new, c_new,
                                   mk.reshape(n_p, n_mem, X_HEADS, X_HEAD_DIM),
                                   mv.reshape(n_p, n_mem, X_HEADS, X_HEAD_DIM))):
            lst.append(a)
        hist = (cache_attn_k[l].reshape(n_s, A_REACH, D_A), cache_attn_v[l].reshape(n_s, A_REACH, D_A))
        xs, k_new, v_new, s_new, c_new = _layer(
            xs, float(PAST_LEN), hist, state_ret[l], state_conv[l],
            cache_mem_k[l].reshape(n_s, n_mem, D_MODEL), cache_mem_v[l].reshape(n_s, n_mem, D_MODEL), w)
        for lst, a in zip(outs_s, (k_new, v_new, s_new, c_new)):
            lst.append(a)
    st = lambda lst: jnp.stack(lst)
    return (xp, xs, st(outs_p[0]), st(outs_p[1]), st(outs_p[2]), st(outs_p[3]), st(outs_p[4]), st(outs_p[5]),
            st(outs_s[0]), st(outs_s[1]), st(outs_s[2]), st(outs_s[3]))
```

```python
import functools

import numpy as np
import jax
import jax.numpy as jnp
from jax import lax
from jax.experimental import pallas as pl
from jax.experimental.pallas import tpu as pltpu

F32 = jnp.float32
BF16 = jnp.bfloat16

D_MODEL = 1024
PAST_LEN = 2048
CHUNK = 64
EPS = 1e-6
NEG_INF = -1e30
LANES = 128

A_HEADS = 4
A_HEAD_DIM = 64
D_A = A_HEADS * A_HEAD_DIM
A_LEFT_CHUNKS = 8
A_REACH = A_LEFT_CHUNKS * CHUNK
REL_CLIP = 128
A_SCALE = A_HEAD_DIM ** -0.5

B_HEADS = 4
B_QK_DIM = 64
B_V_DIM = 128
D_BQK = B_HEADS * B_QK_DIM
D_B = B_HEADS * B_V_DIM
RET_GAMMA_EXP0 = 5.0
ROPE_BASE = 10000.0

C_CHANNELS = 256
CONV_WIDTH = 31
CONV_PAD = 32

X_HEADS = 4
X_HEAD_DIM = D_MODEL // X_HEADS

D_FF = 11 * D_MODEL // 4
FF_CHUNK = 256
N_EXPERTS = 8
TOP_K = 2
D_FF_EXPERT = D_FF // 2
MOE_TILE = 512
MOE_CHUNK = 16
MOE_TILE_CAP = 1152
MOE_BLOCK = 512

COL_QA, COL_KVA, COL_QKB, COL_VB, COL_GB, COL_CAB, D_IN = 0, 256, 768, 1280, 1792, 2304, 2816

VMEM_LIMIT = 56 * 1024 * 1024


def _params(sem):
    return pltpu.CompilerParams(dimension_semantics=sem, vmem_limit_bytes=VMEM_LIMIT)


def _rms(x, g):
    return x * lax.rsqrt(jnp.mean(x * x, axis=-1, keepdims=True) + EPS) * g


def _dot(a, b):
    return jnp.dot(a, b, preferred_element_type=F32)


def _dot_nt(a, b):
    return lax.dot_general(a, b, (((1,), (1,)), ((), ())), preferred_element_type=F32)


def _dot_tn(a, b):
    return lax.dot_general(a, b, (((0,), (0,)), ((), ())), preferred_element_type=F32)


def _const_spec(shape):
    return pl.BlockSpec(shape, lambda *_: (0,) * len(shape))


def _in_proj_kernel(x_ref, g_ref, w_ref, aq_ref, akv_ref, bqk_ref, bv_ref, bg_ref, cab_ref):
    h = _rms(x_ref[...], g_ref[...]).astype(BF16)
    aq_ref[...] = _dot(h, w_ref[:, COL_QA:COL_KVA]).astype(BF16)
    akv_ref[...] = _dot(h, w_ref[:, COL_KVA:COL_QKB])
    bqk_ref[...] = _dot(h, w_ref[:, COL_QKB:COL_VB]).astype(BF16)
    bv_ref[...] = _dot(h, w_ref[:, COL_VB:COL_GB]).astype(BF16)
    bg_ref[...] = _dot(h, w_ref[:, COL_GB:COL_CAB]).astype(BF16)
    cab_ref[...] = _dot(h, w_ref[:, COL_CAB:D_IN])


def _conv_tile(ext, w_ref, b_ref, lng_ref, lnb_ref, tt):
    hist = CONV_WIDTH - 1
    acc = jnp.zeros((tt, C_CHANNELS), F32)
    for b in range(8):
        rb = ext if b == 0 else pltpu.roll(ext, b, axis=0)
        for a in range(CONV_PAD // 8):
            d = 8 * a + b
            if d > hist:
                continue
            k = hist - d
            acc = acc + w_ref[k:k + 1, :] * rb[CONV_PAD - 8 * a:CONV_PAD - 8 * a + tt, :]
    y = acc + b_ref[...]
    mu = jnp.mean(y, axis=-1, keepdims=True)
    d0 = y - mu
    var = jnp.mean(d0 * d0, axis=-1, keepdims=True)
    yn = d0 * lax.rsqrt(var + EPS) * lng_ref[...] + lnb_ref[...]
    return yn * jax.nn.sigmoid(yn)


def _in_proj_conv_kernel(x_ref, g_ref, w_ref, cw_ref, cb_ref, lng_ref, lnb_ref,
                         aq_ref, akv_ref, bqk_ref, bv_ref, bg_ref, oc_ref, nc_ref, up, *, tm, tt):
    hist = CONV_WIDTH - 1

    @pl.when(pl.program_id(1) == 0)
    def _():
        up[tm:tm + CONV_PAD, :] = jnp.zeros((CONV_PAD, C_CHANNELS), F32)

    h = _rms(x_ref[...], g_ref[...]).astype(BF16)
    cab = _dot(h, w_ref[:, COL_CAB:D_IN])
    up[0:CONV_PAD, :] = up[tm:tm + CONV_PAD, :]
    up[CONV_PAD:CONV_PAD + tm, :] = cab[:, 0:C_CHANNELS] * jax.nn.sigmoid(cab[:, C_CHANNELS:2 * C_CHANNELS])
    nc_ref[...] = up[CONV_PAD + tm - hist:CONV_PAD + tm, :]
    cols = ((aq_ref, COL_QA, COL_KVA), (akv_ref, COL_KVA, COL_QKB), (bqk_ref, COL_QKB, COL_VB),
            (bv_ref, COL_VB, COL_GB), (bg_ref, COL_GB, COL_CAB))
    for i in range(tm // tt):
        ext = up[i * tt:i * tt + tt + CONV_PAD, :]
        oc_ref[i * tt:(i + 1) * tt, :] = _conv_tile(ext, cw_ref, cb_ref, lng_ref, lnb_ref, tt).astype(BF16)
        for o_ref, lo, hi in cols[i::tm // tt]:
            o_ref[...] = _dot(h, w_ref[:, lo:hi]).astype(o_ref.dtype)


def _in_proj_conv(x, g, w_in, conv_w, conv_b, ln_g, ln_b):
    n, t, _ = x.shape
    tm = min(t, 512)
    tt = min(tm, 128)
    hist = CONV_WIDTH - 1
    widths = ((D_A, BF16), (2 * D_A, F32), (2 * D_BQK, BF16), (D_B, BF16), (D_B, BF16), (C_CHANNELS, BF16))
    tile = lambda w: pl.BlockSpec((None, tm, w), lambda b, j: (b, j, 0))
    return pl.pallas_call(
        functools.partial(_in_proj_conv_kernel, tm=tm, tt=tt),
        grid=(n, t // tm),
        in_specs=[tile(D_MODEL), _const_spec((1, D_MODEL)), _const_spec((D_MODEL, D_IN)),
                  _const_spec((CONV_WIDTH, C_CHANNELS))] + [_const_spec((1, C_CHANNELS))] * 3,
        out_specs=[tile(w) for w, _ in widths] + [pl.BlockSpec((None, hist, C_CHANNELS), lambda b, j: (b, 0, 0))],
        out_shape=[jax.ShapeDtypeStruct((n, t, w), dt) for w, dt in widths]
                  + [jax.ShapeDtypeStruct((n, hist, C_CHANNELS), F32)],
        scratch_shapes=[pltpu.VMEM((CONV_PAD + tm, C_CHANNELS), F32)],
        compiler_params=_params(("parallel", "arbitrary")),
        name="in_proj_conv",
    )(x, g, w_in, conv_w, conv_b, ln_g, ln_b)


def _in_proj(x2d, g, w_in):
    m = x2d.shape[0]
    tm = min(m, 512)
    widths = ((256, BF16), (512, F32), (512, BF16), (512, BF16), (512, BF16), (512, F32))
    return pl.pallas_call(
        _in_proj_kernel,
        grid=(m // tm,),
        in_specs=[pl.BlockSpec((tm, D_MODEL), lambda i: (i, 0)),
                  _const_spec((1, D_MODEL)),
                  _const_spec((D_MODEL, D_IN))],
        out_specs=[pl.BlockSpec((tm, w), lambda i: (i, 0)) for w, _ in widths],
        out_shape=[jax.ShapeDtypeStruct((m, w), dt) for w, dt in widths],
        compiler_params=_params(("parallel",)),
        name="in_proj",
    )(x2d, g, w_in)


def _band_attn_kernel(*refs, t, tq, has_hist):
    if has_hist:
        aq_ref, akv_ref, hk_ref, hv_ref, bias_ref, o_ref, kc, vc = refs
    else:
        aq_ref, akv_ref, bias_ref, o_ref, kc, vc = refs
    j = pl.program_id(1)
    span = A_REACH + tq

    @pl.when(j == 0)
    def _():
        if has_hist:
            kc[0:A_REACH, :] = hk_ref[...].astype(BF16)
            vc[0:A_REACH, :] = hv_ref[...].astype(BF16)
        else:
            kc[0:A_REACH, :] = jnp.zeros((A_REACH, D_A), BF16)
            vc[0:A_REACH, :] = jnp.zeros((A_REACH, D_A), BF16)
        kc[A_REACH:A_REACH + t, :] = akv_ref[:, 0:D_A].astype(BF16)
        vc[A_REACH:A_REACH + t, :] = akv_ref[:, D_A:2 * D_A].astype(BF16)

    t0 = pl.multiple_of(j * tq, tq)
    q = aq_ref[...]
    lane = lax.broadcasted_iota(jnp.int32, (tq, LANES), 1)
    if not has_hist:
        col = lax.broadcasted_iota(jnp.int32, (tq, span), 1)
        k_ok = col >= A_REACH - t0
    outs = []
    for p in range(A_HEADS // 2):
        qp = q[:, p * LANES:(p + 1) * LANES]
        kp = kc[pl.ds(t0, span), p * LANES:(p + 1) * LANES]
        vp = vc[pl.ds(t0, span), p * LANES:(p + 1) * LANES]
        o_pair = None
        for hh in range(2):
            own = (lane < A_HEAD_DIM) if hh == 0 else (lane >= A_HEAD_DIM)
            qm = jnp.where(own, qp.astype(F32) * A_SCALE, 0.0).astype(BF16)
            s = _dot_nt(qm, kp) + bias_ref[2 * p + hh]
            if not has_hist:
                s = jnp.where(k_ok, s, NEG_INF)
            m = jnp.max(s, axis=-1, keepdims=True)
            e = jnp.exp(s - m)
            l = jnp.sum(e, axis=-1, keepdims=True)
            o = _dot(e.astype(BF16), vp) / l
            o_pair = o if hh == 0 else jnp.where(own, o, o_pair)
        outs.append(o_pair)
    o_ref[...] = jnp.concatenate(outs, axis=1).astype(BF16)


def _band_bias(rel_bias_l, tq):
    span = A_REACH + tq
    period = span + tq
    n_far = A_REACH - REL_CLIP + 1
    far = rel_bias_l[:, 2 * REL_CLIP:]
    n_near = min(2 * REL_CLIP, span - n_far)
    near = jnp.flip(rel_bias_l[:, :2 * REL_CLIP], axis=1)[:, :n_near]
    beyond = jnp.broadcast_to(rel_bias_l[:, :1], (A_HEADS, span - n_far - n_near))
    u = jnp.concatenate([jnp.broadcast_to(far, (A_HEADS, n_far)), near, beyond,
                         jnp.broadcast_to(far, (A_HEADS, tq))], axis=1).astype(F32)
    skew = jnp.broadcast_to(u[:, None, :], (A_HEADS, tq, period)).reshape(A_HEADS, tq * period)
    toeplitz = skew[:, :tq * (period - 1)].reshape(A_HEADS, tq, period - 1)[:, :, :span]
    i = np.arange(tq)[:, None]
    j = np.arange(span)[None, :]
    in_band = (j // CHUNK >= i // CHUNK) & (j // CHUNK <= i // CHUNK + A_LEFT_CHUNKS)
    return jnp.where(in_band[None], toeplitz, NEG_INF)


def _band_attn(aq, akv, hist, rel_bias_l):
    n, t, _ = aq.shape
    tq = min(t, 256)
    span = A_REACH + tq
    bias = _band_bias(rel_bias_l, tq)
    has_hist = hist is not None
    in_specs = [pl.BlockSpec((None, tq, D_A), lambda b, j: (b, j, 0)),
                pl.BlockSpec((None, t, 2 * D_A), lambda b, j: (b, 0, 0))]
    args = [aq, akv]
    if has_hist:
        in_specs += [pl.BlockSpec((None, A_REACH, D_A), lambda b, j: (b, 0, 0))] * 2
        args += list(hist)
    in_specs.append(_const_spec((A_HEADS, tq, span)))
    args.append(bias)
    return pl.pallas_call(
        functools.partial(_band_attn_kernel, t=t, tq=tq, has_hist=has_hist),
        grid=(n, t // tq),
        in_specs=in_specs,
        out_specs=pl.BlockSpec((None, tq, D_A), lambda b, j: (b, j, 0)),
        out_shape=jax.ShapeDtypeStruct((n, t, D_A), BF16),
        scratch_shapes=[pltpu.VMEM((A_REACH + t, D_A), BF16)] * 2,
        compiler_params=_params(("parallel", "arbitrary")),
        name="band_attn",
    )(*args)


def _swap_halves(x):
    lane = lax.broadcasted_iota(jnp.int32, x.shape, 1)
    first = (lane % B_QK_DIM) < (B_QK_DIM // 2)
    return jnp.where(first, pltpu.roll(x, LANES - B_QK_DIM // 2, axis=1), pltpu.roll(x, B_QK_DIM // 2, axis=1))


def _retention_kernel(bqk_ref, bv_ref, bg_ref, cos_ref, sin_ref, dmask_ref, qdec_ref, kdec_ref, sdec_ref,
                      s0_ref, gn_ref, ob_ref, sfin_ref, st, *, t, bc):
    for h in range(B_HEADS):
        off = (h % 2) * B_QK_DIM
        st[h] = jnp.zeros((LANES, B_V_DIM), F32)
        st[h, off:off + B_QK_DIM, :] = s0_ref[h]

    lane = lax.broadcasted_iota(jnp.int32, (bc, LANES), 1)

    def chunk(c, carry):
        r0 = pl.multiple_of(c * bc, bc)
        rows = pl.ds(r0, bc)
        cs = cos_ref[rows, :]
        sn = sin_ref[rows, :]
        for p in range(B_HEADS // 2):
            qx = bqk_ref[rows, p * LANES:(p + 1) * LANES].astype(F32)
            kx = bqk_ref[rows, D_BQK + p * LANES:D_BQK + (p + 1) * LANES].astype(F32)
            qr = qx * cs + _swap_halves(qx) * sn
            kr = (kx * cs + _swap_halves(kx) * sn) * (B_QK_DIM ** -0.5)
            for hh in range(2):
                h = 2 * p + hh
                own = (lane < B_QK_DIM) if hh == 0 else (lane >= B_QK_DIM)
                qh = jnp.where(own, qr, 0.0)
                kh = jnp.where(own, kr, 0.0)
                v = bv_ref[rows, h * B_V_DIM:(h + 1) * B_V_DIM]
                att = _dot_nt(qh.astype(BF16), kh.astype(BF16)) * dmask_ref[h]
                o = (_dot(att.astype(BF16), v)
                     + _dot((qh * qdec_ref[h]).astype(BF16), st[h].astype(BF16)))
                st[h] = st[h] * sdec_ref[h] + _dot_tn((kh * kdec_ref[h]).astype(BF16), v)
                mu = jnp.mean(o, axis=-1, keepdims=True)
                d = o - mu
                var = jnp.mean(d * d, axis=-1, keepdims=True)
                on = d * lax.rsqrt(var + EPS) * gn_ref[:, h * B_V_DIM:(h + 1) * B_V_DIM]
                g = bg_ref[rows, h * B_V_DIM:(h + 1) * B_V_DIM].astype(F32)
                ob_ref[rows, h * B_V_DIM:(h + 1) * B_V_DIM] = (g * jax.nn.sigmoid(g) * on).astype(BF16)
        return carry

    lax.fori_loop(0, t // bc, chunk, 0)
    for h in range(B_HEADS):
        off = (h % 2) * B_QK_DIM
        sfin_ref[h] = st[h, off:off + B_QK_DIM, :]


def _retention_tables(t, bc, pos0):
    log_g = jnp.log(1.0 - 2.0 ** (-(RET_GAMMA_EXP0 + jnp.arange(B_HEADS, dtype=F32))))
    i = jnp.arange(bc, dtype=F32)
    diff = i[:, None] - i[None, :]
    dmask = jnp.where(diff[None] >= 0, jnp.exp(jnp.maximum(diff, 0.0)[None] * log_g[:, None, None]), 0.0)
    qdec = jnp.exp((i[None, :] + 1.0) * log_g[:, None])
    kdec = jnp.exp((bc - 1.0 - i)[None, :] * log_g[:, None])
    sdec = jnp.exp(bc * log_g)
    qdec = jnp.broadcast_to(qdec[:, :, None], (B_HEADS, bc, LANES))
    kdec = jnp.broadcast_to(kdec[:, :, None], (B_HEADS, bc, LANES))
    sdec = jnp.broadcast_to(sdec[:, None, None], (B_HEADS, 1, B_V_DIM))
    half = B_QK_DIM // 2
    pos = pos0 + jnp.arange(t, dtype=F32)
    inv_freq = ROPE_BASE ** (-jnp.arange(half, dtype=F32) / half)
    ang = pos[:, None] * inv_freq[None, :]
    cos = jnp.tile(jnp.cos(ang), (1, LANES // half))
    sin = jnp.tile(jnp.concatenate([-jnp.sin(ang), jnp.sin(ang)], axis=1), (1, LANES // B_QK_DIM))
    return cos, sin, dmask, qdec, kdec, sdec


def _retention(bqk, bv, bg, s0, gn_g, pos0):
    n, t, _ = bqk.shape
    bc = min(t, 256)
    tables = _retention_tables(t, bc, pos0)
    seq = lambda w: pl.BlockSpec((None, t, w), lambda b: (b, 0, 0))
    state = pl.BlockSpec((None, B_HEADS, B_QK_DIM, B_V_DIM), lambda b: (b, 0, 0, 0))
    return pl.pallas_call(
        functools.partial(_retention_kernel, t=t, bc=bc),
        grid=(n,),
        in_specs=[seq(2 * D_BQK), seq(D_B), seq(D_B)] + [_const_spec(tb.shape) for tb in tables]
                 + [state, _const_spec((1, D_B))],
        out_specs=[seq(D_B), state],
        out_shape=[jax.ShapeDtypeStruct((n, t, D_B), BF16),
                   jax.ShapeDtypeStruct((n, B_HEADS, B_QK_DIM, B_V_DIM), F32)],
        scratch_shapes=[pltpu.VMEM((B_HEADS, LANES, B_V_DIM), F32)],
        compiler_params=_params(("parallel",)),
        name="retention",
    )(bqk, bv, bg, *tables, s0, gn_g)


def _conv_kernel(cab_ref, buf_ref, w_ref, b_ref, lng_ref, lnb_ref, oc_ref, nc_ref, up, *, t, tt):
    hist = CONV_WIDTH - 1
    up[0:CONV_PAD, :] = jnp.zeros((CONV_PAD, C_CHANNELS), F32)
    up[CONV_PAD - hist:CONV_PAD, :] = buf_ref[...]

    def glu(i, carry):
        rows = pl.ds(pl.multiple_of(i * tt, tt), tt)
        ca = cab_ref[rows, 0:C_CHANNELS]
        cb = cab_ref[rows, C_CHANNELS:2 * C_CHANNELS]
        up[pl.ds(pl.multiple_of(CONV_PAD + i * tt, 8), tt), :] = ca * jax.nn.sigmoid(cb)
        return carry

    lax.fori_loop(0, t // tt, glu, 0)

    def tile(i, carry):
        t0 = pl.multiple_of(i * tt, tt)
        ext = up[pl.ds(t0, tt + CONV_PAD), :]
        oc_ref[pl.ds(t0, tt), :] = _conv_tile(ext, w_ref, b_ref, lng_ref, lnb_ref, tt).astype(BF16)
        return carry

    lax.fori_loop(0, t // tt, tile, 0)
    nc_ref[...] = up[CONV_PAD + t - hist:CONV_PAD + t, :]


def _conv_module(cab, buf, conv_w, conv_b, ln_g, ln_b):
    n, t, _ = cab.shape
    tt = min(t, 128)
    hist = CONV_WIDTH - 1
    return pl.pallas_call(
        functools.partial(_conv_kernel, t=t, tt=tt),
        grid=(n,),
        in_specs=[pl.BlockSpec((None, t, 2 * C_CHANNELS), lambda b: (b, 0, 0)),
                  pl.BlockSpec((None, hist, C_CHANNELS), lambda b: (b, 0, 0)),
                  _const_spec((CONV_WIDTH, C_CHANNELS))] + [_const_spec((1, C_CHANNELS))] * 3,
        out_specs=[pl.BlockSpec((None, t, C_CHANNELS), lambda b: (b, 0, 0)),
                   pl.BlockSpec((None, hist, C_CHANNELS), lambda b: (b, 0, 0))],
        out_shape=[jax.ShapeDtypeStruct((n, t, C_CHANNELS), BF16),
                   jax.ShapeDtypeStruct((n, hist, C_CHANNELS), F32)],
        scratch_shapes=[pltpu.VMEM((CONV_PAD + t, C_CHANNELS), F32)],
        compiler_params=_params(("parallel",)),
        name="conv_module",
    )(cab, buf, conv_w, conv_b, ln_g, ln_b)


def _mem_kv_kernel(mem_ref, wk_ref, wv_ref, mk_ref, mv_ref):
    m = mem_ref[...].astype(BF16)
    mk_ref[...] = _dot(m, wk_ref[...])
    mv_ref[...] = _dot(m, wv_ref[...])


def _mem_kv(mem2d, wk, wv):
    depth = wk.shape[0]
    m = mem2d.shape[0]
    tm = 512
    wspec = pl.BlockSpec((None, D_MODEL, D_MODEL), lambda l, i: (l, 0, 0))
    ospec = pl.BlockSpec((None, tm, D_MODEL), lambda l, i: (l, i, 0))
    return pl.pallas_call(
        _mem_kv_kernel,
        grid=(depth, m // tm),
        in_specs=[pl.BlockSpec((tm, D_MODEL), lambda l, i: (i, 0)), wspec, wspec],
        out_specs=[ospec, ospec],
        out_shape=[jax.ShapeDtypeStruct((depth, m, D_MODEL), F32)] * 2,
        compiler_params=_params(("parallel", "parallel")),
        name="mem_kv",
    )(mem2d, wk, wv)


def _route(logits):
    lane = lax.broadcasted_iota(jnp.int32, logits.shape, 1).astype(F32)
    valid = lane < N_EXPERTS
    lg = jnp.where(valid, logits, NEG_INF)
    e = jnp.exp(lg - jnp.max(lg, axis=-1, keepdims=True))
    probs = e / jnp.sum(e, axis=-1, keepdims=True)
    p1 = jnp.max(probs, axis=-1, keepdims=True)
    i1 = jnp.min(jnp.where(probs == p1, lane, float(LANES)), axis=-1, keepdims=True)
    rest = jnp.where(lane == i1, -1.0, probs)
    rest = jnp.where(valid, rest, -1.0)
    p2 = jnp.max(rest, axis=-1, keepdims=True)
    i2 = jnp.min(jnp.where(rest == p2, lane, float(LANES)), axis=-1, keepdims=True)
    tot = p1 + p2
    return lane, i1, i2, p1 / tot, p2 / tot


def _sort_tile_by_expert(logits, h_hi, tri_ref, hs_ref, route_ref, meta_ref):
    tm = logits.shape[0]
    lt = jnp.transpose(logits)[0:N_EXPERTS, :]
    sub = lax.broadcasted_iota(jnp.int32, (N_EXPERTS, tm), 0).astype(F32)
    ex = jnp.exp(lt - jnp.max(lt, axis=0, keepdims=True))
    probs = ex / jnp.sum(ex, axis=0, keepdims=True)
    p1 = jnp.max(probs, axis=0, keepdims=True)
    i1 = jnp.min(jnp.where(probs == p1, sub, float(N_EXPERTS)), axis=0, keepdims=True)
    rest = jnp.where(sub == i1, -1.0, probs)
    p2 = jnp.max(rest, axis=0, keepdims=True)
    i2 = jnp.min(jnp.where(rest == p2, sub, float(N_EXPERTS)), axis=0, keepdims=True)
    w1 = p1 / (p1 + p2)
    w2 = p2 / (p1 + p2)
    oh1 = jnp.where(sub == i1, 1.0, 0.0)
    oh2 = jnp.where(sub == i2, 1.0, 0.0)
    before1 = _dot(oh1.astype(BF16), tri_ref[...])
    before2 = _dot(oh2.astype(BF16), tri_ref[...])
    cnt1 = jnp.sum(oh1, axis=1, keepdims=True)
    cnt2 = jnp.sum(oh2, axis=1, keepdims=True)
    padded = jnp.floor((cnt1 + cnt2 + (MOE_CHUNK - 1.0)) * (1.0 / MOE_CHUNK)) * MOE_CHUNK
    run = jnp.broadcast_to(padded, (N_EXPERTS, LANES))
    sub8 = lax.broadcasted_iota(jnp.int32, (N_EXPERTS, LANES), 0)
    incl = run
    for k in (1, 2, 4):
        incl = incl + jnp.where(sub8 >= k, pltpu.roll(incl, k, axis=0), 0.0)
    start = (incl - run)[:, 0:1]
    d1 = jnp.sum(oh1 * (start + before1), axis=0, keepdims=True)
    d2 = jnp.sum(oh2 * (start + cnt1 + before2), axis=0, keepdims=True)
    row = lax.broadcasted_iota(jnp.int32, (MOE_TILE_CAP, tm), 0).astype(F32)
    perm = jnp.where(row == d1, 1.0, jnp.where(row == d2, 1.0, 0.0)).astype(BF16)
    hs_ref[...] = _dot(perm, h_hi).astype(BF16)
    info = jnp.concatenate([d1, d2, w1, w2, jnp.zeros((LANES - 4, tm), F32)], axis=0)
    route_ref[...] = jnp.transpose(info)
    lane8 = lax.broadcasted_iota(jnp.int32, (N_EXPERTS, LANES), 1)
    meta_ref[...] = jnp.where(lane8 == 0, start * (1.0 / MOE_CHUNK),
                              jnp.where(lane8 == 1, padded * (1.0 / MOE_CHUNK), 0.0))


def _post_mix_kernel(*refs, mode):
    (x_ref, oa_ref, ob_ref, oc_ref, wout_ref, gmem_ref, wq_ref, mk_ref, mv_ref, wo_ref, gffn_ref) = refs[:11]
    if mode == "dense":
        x2_ref, h_ref = refs[11:]
    elif mode == "experts":
        router_ref, x2_ref, h_ref, route_ref = refs[11:]
    else:
        router_ref, tri_ref, x2_ref, h_ref, route_ref, meta_ref = refs[11:]
    y = (_dot(oa_ref[...], wout_ref[0:D_A, :])
         + _dot(ob_ref[...], wout_ref[D_A:D_A + D_B, :])
         + _dot(oc_ref[...], wout_ref[D_A + D_B:D_MODEL, :]))
    x1 = x_ref[...] + y
    q = _dot(_rms(x1, gmem_ref[...]).astype(BF16), wq_ref[...]).astype(BF16)
    outs = []
    for h in range(X_HEADS):
        cols = slice(h * X_HEAD_DIM, (h + 1) * X_HEAD_DIM)
        s = _dot_nt(q[:, cols], mk_ref[:, cols].astype(BF16)) * (X_HEAD_DIM ** -0.5)
        e = jnp.exp(s - jnp.max(s, axis=-1, keepdims=True))
        l = jnp.sum(e, axis=-1, keepdims=True)
        outs.append((_dot(e.astype(BF16), mv_ref[:, cols].astype(BF16)) / l).astype(BF16))
    x2 = x1 + _dot(jnp.concatenate(outs, axis=1), wo_ref[...])
    x2_ref[...] = x2
    hf = _rms(x2, gffn_ref[...])
    h_hi = hf.astype(BF16)
    if mode != "grouped":
        h_ref[...] = h_hi
    if mode != "dense":
        h_lo = (hf - h_hi.astype(F32)).astype(BF16)
        logits = (_dot(h_hi, router_ref[0]) + _dot(h_lo, router_ref[0])) + _dot(h_hi, router_ref[1])
        if mode == "experts":
            lane, i1, i2, w1, w2 = _route(logits)
            route_ref[...] = jnp.where(lane == i1, w1, jnp.where(lane == i2, w2, 0.0))
        else:
            _sort_tile_by_expert(logits, h_hi, tri_ref, h_ref, route_ref, meta_ref)


def _post_mix(x, oa, ob, oc, w_out, g_mem, wq, mk, mv, mem_base, wo, g_ffn, router, mode):
    n, t, _ = x.shape
    tm = min(t, 512)
    tile = lambda w: pl.BlockSpec((None, tm, w), lambda b, i: (b, i, 0))
    mem = pl.BlockSpec((None,) + mk.shape[1:], lambda b, i: (mem_base + b, 0, 0))
    sq = _const_spec((D_MODEL, D_MODEL))
    vec = _const_spec((1, D_MODEL))
    in_specs = [tile(D_MODEL), tile(D_A), tile(D_B), tile(C_CHANNELS), sq, vec, sq, mem, mem, sq, vec]
    args = [x, oa, ob, oc, w_out, g_mem, wq, mk, mv, wo, g_ffn]
    out_specs = [tile(D_MODEL)]
    out_shape = [jax.ShapeDtypeStruct((n, t, D_MODEL), F32)]
    if mode != "dense":
        in_specs.append(_const_spec((2, D_MODEL, LANES)))
        args.append(router)
    if mode == "grouped":
        assert tm == MOE_TILE
        in_specs.append(_const_spec((tm, tm)))
        args.append(jnp.triu(jnp.ones((tm, tm), BF16), k=1))
        out_specs.append(pl.BlockSpec((None, MOE_TILE_CAP, D_MODEL), lambda b, i: (b, i, 0)))
        out_shape.append(jax.ShapeDtypeStruct((n, t // tm * MOE_TILE_CAP, D_MODEL), BF16))
    else:
        out_specs.append(tile(D_MODEL))
        out_shape.append(jax.ShapeDtypeStruct((n, t, D_MODEL), BF16))
    if mode != "dense":
        out_specs.append(tile(LANES))
        out_shape.append(jax.ShapeDtypeStruct((n, t, LANES), F32))
    if mode == "grouped":
        out_specs.append(pl.BlockSpec((None, N_EXPERTS, LANES), lambda b, i: (b, i, 0)))
        out_shape.append(jax.ShapeDtypeStruct((n, t // tm * N_EXPERTS, LANES), F32))
    return pl.pallas_call(
        functools.partial(_post_mix_kernel, mode=mode),
        grid=(n, t // tm),
        in_specs=in_specs,
        out_specs=out_specs,
        out_shape=out_shape,
        compiler_params=_params(("parallel", "parallel")),
        name="post_mix",
    )(*args)


def _ffn_kernel(x_ref, h_ref, wg_ref, wu_ref, wd_ref, o_ref):
    h = h_ref[...]
    acc = x_ref[...]
    for c in range(D_FF // FF_CHUNK):
        cols = slice(c * FF_CHUNK, (c + 1) * FF_CHUNK)
        g = _dot(h, wg_ref[:, cols])
        u = _dot(h, wu_ref[:, cols])
        acc = acc + _dot((g * jax.nn.sigmoid(g) * u).astype(BF16), wd_ref[cols, :])
    o_ref[...] = acc


def _ffn(x2d, h2d, wg, wu, wd):
    m = x2d.shape[0]
    tm = min(m, 512)
    row = pl.BlockSpec((tm, D_MODEL), lambda i: (i, 0))
    return pl.pallas_call(
        _ffn_kernel,
        grid=(m // tm,),
        in_specs=[row, row, _const_spec((D_MODEL, D_FF)), _const_spec((D_MODEL, D_FF)),
                  _const_spec((D_FF, D_MODEL))],
        out_specs=row,
        out_shape=jax.ShapeDtypeStruct((m, D_MODEL), F32),
        compiler_params=_params(("parallel",)),
        name="ffn_dense",
    )(x2d, h2d, wg, wu, wd)


def _moe_kernel(x_ref, h_ref, comb_ref, wg_ref, wu_ref, wd_ref, gfin_ref, o_ref, acc):
    e = pl.program_id(1)

    @pl.when(e == 0)
    def _():
        acc[...] = x_ref[...]

    h = h_ref[...]
    g = _dot(h, wg_ref[...])
    u = _dot(h, wu_ref[...])
    y = _dot((g * jax.nn.sigmoid(g) * u).astype(BF16), wd_ref[...])
    comb = comb_ref[...]
    lane = lax.broadcasted_iota(jnp.int32, comb.shape, 1)
    ce = jnp.sum(jnp.where(lane == e, comb, 0.0), axis=-1, keepdims=True)
    acc[...] += ce * y

    @pl.when(e == N_EXPERTS - 1)
    def _():
        o_ref[...] = _rms(acc[...], gfin_ref[...])


def _moe(x2d, h2d, comb2d, wg, wu, wd, g_fin):
    m = x2d.shape[0]
    tm = min(m, 512)
    row = lambda w: pl.BlockSpec((tm, w), lambda i, e: (i, 0))
    return pl.pallas_call(
        _moe_kernel,
        grid=(m // tm, N_EXPERTS),
        in_specs=[row(D_MODEL), row(D_MODEL), row(LANES),
                  pl.BlockSpec((None, D_MODEL, D_FF_EXPERT), lambda i, e: (e, 0, 0)),
                  pl.BlockSpec((None, D_MODEL, D_FF_EXPERT), lambda i, e: (e, 0, 0)),
                  pl.BlockSpec((None, D_FF_EXPERT, D_MODEL), lambda i, e: (e, 0, 0)),
                  _const_spec((1, D_MODEL))],
        out_specs=row(D_MODEL),
        out_shape=jax.ShapeDtypeStruct((m, D_MODEL), F32),
        scratch_shapes=[pltpu.VMEM((tm, D_MODEL), F32)],
        compiler_params=_params(("parallel", "arbitrary")),
        name="moe_experts",
    )(x2d, h2d, comb2d, wg, wu, wd, g_fin)


def _moe_plan(meta, n_tiles):
    cpb = MOE_BLOCK // MOE_CHUNK
    tile_chunks = MOE_TILE_CAP // MOE_CHUNK
    nb = -(-(n_tiles * tile_chunks + N_EXPERTS * (cpb - 1)) // cpb)
    first = meta[:, :, 0].astype(jnp.int32)
    cnt = meta[:, :, 1].astype(jnp.int32)
    tot = jnp.sum(cnt, axis=0)
    eend = jnp.cumsum((tot + cpb - 1) // cpb * cpb)
    q = jnp.arange(nb * cpb, dtype=jnp.int32)
    e_q = jnp.minimum(jnp.sum((q[:, None] >= eend[None, :]).astype(jnp.int32), axis=1), N_EXPERTS - 1)
    oh_e = (e_q[:, None] == jnp.arange(N_EXPERTS, dtype=jnp.int32)[None, :]).astype(jnp.int32)
    pick = lambda table: jnp.sum(oh_e[:, :, None] * table.T[None, :, :], axis=1)
    j = q - jnp.sum(oh_e * (eend - (tot + cpb - 1) // cpb * cpb)[None, :], axis=1)
    valid = (j < jnp.sum(oh_e * tot[None, :], axis=1)) & (q < eend[N_EXPERTS - 1])
    incl = jnp.cumsum(cnt, axis=0)
    tile_q = jnp.minimum(jnp.sum((j[:, None] >= pick(incl)).astype(jnp.int32), axis=1), n_tiles - 1)
    oh_t = (tile_q[:, None] == jnp.arange(n_tiles, dtype=jnp.int32)[None, :]).astype(jnp.int32)
    chunk = (tile_q * tile_chunks + jnp.sum(oh_t * pick(first), axis=1)
             + j - jnp.sum(oh_t * pick(incl - cnt), axis=1))
    k = jnp.cumsum(jnp.logical_not(valid).astype(jnp.int32)) - 1
    used = jnp.sum(cnt, axis=1)
    free_incl = jnp.cumsum(tile_chunks - used)
    tile_k = jnp.minimum(jnp.sum((k[:, None] >= free_incl[None, :]).astype(jnp.int32), axis=1), n_tiles - 1)
    oh_k = (tile_k[:, None] == jnp.arange(n_tiles, dtype=jnp.int32)[None, :]).astype(jnp.int32)
    in_tile = (tile_k * tile_chunks + jnp.sum(oh_k * used[None, :], axis=1)
               + k - jnp.sum(oh_k * (free_incl - (tile_chunks - used))[None, :], axis=1))
    n_free = free_incl[n_tiles - 1]
    spare = jnp.where(k < n_free, in_tile, n_tiles * tile_chunks + k - n_free)
    src = jnp.where(valid, chunk, 0)
    dst = jnp.where(valid, chunk, spare)
    blk_expert = e_q[::cpb]
    n_active = (eend[N_EXPERTS - 1] // cpb).reshape(1)
    return blk_expert, n_active, src.reshape(nb, 1, cpb), dst.reshape(nb, 1, cpb)


def _moe_grouped_kernel(be_ref, nact_ref, src_ref, src_next_ref, dst_ref, hs_hbm, wg_ref, wu_ref, wd_ref,
                        y_hbm, xbuf, ybuf, sem_in, sem_out):
    b = pl.program_id(0)
    nb = pl.num_programs(0)
    slot = b % 2
    cpb = MOE_BLOCK // MOE_CHUNK

    def chunk_rows(i):
        return pl.ds(pl.multiple_of(i * MOE_CHUNK, MOE_CHUNK), MOE_CHUNK)

    def gather(idx_ref, s):
        for c in range(cpb):
            pltpu.make_async_copy(hs_hbm.at[chunk_rows(idx_ref[0, c]), :],
                                  xbuf.at[s, c * MOE_CHUNK:(c + 1) * MOE_CHUNK, :],
                                  sem_in.at[s]).start(priority=c % 2)

    def wait_gather(s):
        pltpu.make_async_copy(hs_hbm.at[pl.ds(0, MOE_BLOCK), :], xbuf.at[s], sem_in.at[s]).wait()

    def wait_scatter(s):
        pltpu.make_async_copy(ybuf.at[s], y_hbm.at[pl.ds(0, MOE_BLOCK), :], sem_out.at[s]).wait()

    @pl.when(b == 0)
    def _():
        gather(src_ref, 0)

    @pl.when(b + 1 < nb)
    def _():
        gather(src_next_ref, 1 - slot)

    wait_gather(slot)

    @pl.when(b >= 2)
    def _():
        wait_scatter(slot)

    @pl.when(b < nact_ref[0])
    def _():
        x = xbuf[slot]
        g = _dot(x, wg_ref[...])
        u = _dot(x, wu_ref[...])
        ybuf[slot] = _dot((g * jax.nn.sigmoid(g) * u).astype(BF16), wd_ref[...]).astype(BF16)

    @pl.when(b >= nact_ref[0])
    def _():
        ybuf[slot] = jnp.zeros((MOE_BLOCK, D_MODEL), BF16)

    for c in range(cpb):
        pltpu.make_async_copy(ybuf.at[slot, c * MOE_CHUNK:(c + 1) * MOE_CHUNK, :],
                              y_hbm.at[chunk_rows(dst_ref[0, c]), :],
                              sem_out.at[slot]).start(priority=c % 2)

    @pl.when(b == nb - 1)
    def _():
        wait_scatter(slot)
        wait_scatter(1 - slot)


def _moe_grouped(hs, blk_expert, n_active, src, dst, wg, wu, wd):
    nb, _, cpb = src.shape
    assert nb >= 2
    idx_spec = lambda f: pl.BlockSpec((None, 1, cpb), f, memory_space=pltpu.SMEM)
    wspec = lambda shape: pl.BlockSpec((None,) + shape, lambda b, be, na: (be[b], 0, 0))
    return pl.pallas_call(
        _moe_grouped_kernel,
        grid_spec=pltpu.PrefetchScalarGridSpec(
            num_scalar_prefetch=2,
            grid=(nb,),
            in_specs=[idx_spec(lambda b, be, na: (b, 0, 0)),
                      idx_spec(lambda b, be, na: (jnp.minimum(b + 1, nb - 1), 0, 0)),
                      idx_spec(lambda b, be, na: (b, 0, 0)),
                      pl.BlockSpec(memory_space=pl.ANY),
                      wspec((D_MODEL, D_FF_EXPERT)), wspec((D_MODEL, D_FF_EXPERT)),
                      wspec((D_FF_EXPERT, D_MODEL))],
            out_specs=pl.BlockSpec(memory_space=pl.ANY),
            scratch_shapes=[pltpu.VMEM((2, MOE_BLOCK, D_MODEL), BF16), pltpu.VMEM((2, MOE_BLOCK, D_MODEL), BF16),
                            pltpu.SemaphoreType.DMA((2,)), pltpu.SemaphoreType.DMA((2,))]),
        out_shape=jax.ShapeDtypeStruct((nb * MOE_BLOCK, D_MODEL), BF16),
        compiler_params=_params(("arbitrary",)),
        name="moe_grouped",
    )(blk_expert, n_active, src, src, dst, hs, wg, wu, wd)


def _moe_combine_kernel(x_ref, route_ref, y_ref, gfin_ref, o_ref):
    tm = x_ref.shape[0]
    route = route_ref[...]
    d1, d2, w1, w2 = route[:, 0:1], route[:, 1:2], route[:, 2:3], route[:, 3:4]
    col = lax.broadcasted_iota(jnp.int32, (tm, MOE_TILE_CAP), 1).astype(F32)
    wm = jnp.where(col == d1, w1, jnp.where(col == d2, w2, 0.0))
    wm_hi = wm.astype(BF16)
    wm_lo = (wm - wm_hi.astype(F32)).astype(BF16)
    y = y_ref[...]
    acc = x_ref[...] + (_dot(wm_hi, y) + _dot(wm_lo, y))
    o_ref[...] = _rms(acc, gfin_ref[...])


def _moe_combine(x2d, route2d, ys, g_fin):
    m = x2d.shape[0]
    tm = MOE_TILE
    return pl.pallas_call(
        _moe_combine_kernel,
        grid=(m // tm,),
        in_specs=[pl.BlockSpec((tm, D_MODEL), lambda i: (i, 0)),
                  pl.BlockSpec((tm, LANES), lambda i: (i, 0)),
                  pl.BlockSpec((MOE_TILE_CAP, D_MODEL), lambda i: (i, 0)),
                  _const_spec((1, D_MODEL))],
        out_specs=pl.BlockSpec((tm, D_MODEL), lambda i: (i, 0)),
        out_shape=jax.ShapeDtypeStruct((m, D_MODEL), F32),
        compiler_params=_params(("parallel",)),
        name="moe_combine",
    )(x2d, route2d, ys, g_fin)


def _layer(x, pos0, attn_hist, ret_state, conv_state, mk, mv, mem_base, w):
    n, t, _ = x.shape
    conv_w = (w["conv_w"], w["conv_b"], w["conv_ln_g"], w["conv_ln_b"])
    if conv_state is None:
        aq, akv, bqk, bv, bg, oc, new_conv = _in_proj_conv(x, w["norm_mix_g"], w["w_in"], *conv_w)
    else:
        aq, akv, bqk, bv, bg, cab = _in_proj(x.reshape(n * t, D_MODEL), w["norm_mix_g"], w["w_in"])
        r3 = lambda a: a.reshape(n, t, a.shape[-1])
        aq, akv, bqk, bv, bg, cab = map(r3, (aq, akv, bqk, bv, bg, cab))
        oc, new_conv = _conv_module(cab, conv_state, *conv_w)
    oa = _band_attn(aq, akv, attn_hist, w["rel_bias"])
    ob, new_s = _retention(bqk, bv, bg, ret_state, w["ret_gn_g"], pos0)
    m = n * t
    if "router" not in w:
        mode = "dense"
    elif m >= N_EXPERTS * MOE_BLOCK and t % MOE_TILE == 0:
        mode = "grouped"
    else:
        mode = "experts"
    post = _post_mix(x, oa, ob, oc, w["w_out"], w["norm_mem_g"], w["wx_q"], mk, mv, mem_base, w["wx_o"],
                     w["norm_ffn_g"], w.get("router"), mode)
    x2 = post[0].reshape(m, D_MODEL)
    if mode == "dense":
        x3 = _ffn(x2, post[1].reshape(m, D_MODEL), w["ffn_g"], w["ffn_u"], w["ffn_d"])
    elif mode == "experts":
        x3 = _moe(x2, post[1].reshape(m, D_MODEL), post[2].reshape(m, LANES),
                  w["moe_g"], w["moe_u"], w["moe_d"], w["final_g"])
    else:
        n_tiles = m // MOE_TILE
        blk_expert, n_active, src, dst = _moe_plan(post[3].reshape(n_tiles, N_EXPERTS, LANES), n_tiles)
        ys = _moe_grouped(post[1].reshape(n_tiles * MOE_TILE_CAP, D_MODEL), blk_expert, n_active, src, dst,
                          w["moe_g"], w["moe_u"], w["moe_d"])
        x3 = _moe_combine(x2, post[2].reshape(m, LANES), ys, w["final_g"])
    keep = min(A_REACH, t)
    new_kv = akv[:, t - keep:, :]
    new_k = new_kv[..., :D_A].reshape(n, keep, A_HEADS, A_HEAD_DIM)
    new_v = new_kv[..., D_A:].reshape(n, keep, A_HEADS, A_HEAD_DIM)
    return x3.reshape(n, t, D_MODEL), new_k, new_v, new_s, new_conv


def kernel(x_prompt, x_sample, cache_attn_k, cache_attn_v, state_ret, state_conv, cache_mem_k, cache_mem_v,
           mem_prompt, norm_mix_g, w_in, rel_bias, ret_gn_g, conv_w, conv_b, conv_ln_g, conv_ln_b, w_out,
           norm_mem_g, wx_q, wx_k, wx_v, wx_o, norm_ffn_g, ffn_w_gate, ffn_w_up, ffn_w_down,
           router_w, moe_w_gate, moe_w_up, moe_w_down, final_norm_g):
    depth = w_in.shape[0]
    assert depth == 2, "layer 0 dense FFN, layer 1 experts + closing norm"
    n_p, _, _ = x_prompt.shape
    n_s, t_s, _ = x_sample.shape
    n_mem = mem_prompt.shape[1]
    row = lambda a: a.reshape(1, -1).astype(F32)
    xp, xs = x_prompt, x_sample
    mem2d = mem_prompt.reshape(n_p * n_mem, D_MODEL)
    outs_p = [[] for _ in range(4)]
    outs_s = [[] for _ in range(4)]
    mk_all, mv_all = _mem_kv(mem2d, wx_k.astype(BF16), wx_v.astype(BF16))
    mk_p = mk_all.reshape(depth * n_p, n_mem, D_MODEL)
    mv_p = mv_all.reshape(depth * n_p, n_mem, D_MODEL)
    mk_s = cache_mem_k.reshape(depth * n_s, n_mem, D_MODEL)
    mv_s = cache_mem_v.reshape(depth * n_s, n_mem, D_MODEL)
    for l in range(depth):
        w = dict(norm_mix_g=row(norm_mix_g[l]), w_in=w_in[l].astype(BF16), rel_bias=rel_bias[l],
                 ret_gn_g=row(ret_gn_g[l]), conv_w=conv_w[l].astype(F32), conv_b=row(conv_b[l]),
                 conv_ln_g=row(conv_ln_g[l]), conv_ln_b=row(conv_ln_b[l]), w_out=w_out[l].astype(BF16),
                 norm_mem_g=row(norm_mem_g[l]), wx_q=wx_q[l].astype(BF16), wx_o=wx_o[l].astype(BF16),
                 norm_ffn_g=row(norm_ffn_g[l]))
        if l % 2 == 0:
            i = l // 2
            w.update(ffn_g=ffn_w_gate[i].astype(BF16), ffn_u=ffn_w_up[i].astype(BF16),
                     ffn_d=ffn_w_down[i].astype(BF16))
        else:
            i = l // 2
            r = jnp.pad(router_w[i].astype(F32), ((0, 0), (0, LANES - N_EXPERTS)))
            r_hi = r.astype(BF16)
            r_lo = (r - r_hi.astype(F32)).astype(BF16)
            w.update(router=jnp.stack([r_hi, r_lo]),
                     moe_g=moe_w_gate[i].astype(BF16), moe_u=moe_w_up[i].astype(BF16),
                     moe_d=moe_w_down[i].astype(BF16),
                     final_g=row(final_norm_g) if l == depth - 1 else None)
        xp, k_new, v_new, s_new, c_new = _layer(
            xp, 0.0, None, jnp.zeros((n_p, B_HEADS, B_QK_DIM, B_V_DIM), F32), None, mk_p, mv_p, l * n_p, w)
        for lst, a in zip(outs_p, (k_new, v_new, s_new, c_new)):
            lst.append(a)
        hist = (cache_attn_k[l].reshape(n_s, A_REACH, D_A), cache_attn_v[l].reshape(n_s, A_REACH, D_A))
        xs, k_new, v_new, s_new, c_new = _layer(
            xs, float(PAST_LEN), hist, state_ret[l], state_conv[l], mk_s, mv_s, l * n_s, w)
        for lst, a in zip(outs_s, (k_new, v_new, s_new, c_new)):
            lst.append(a)
    st = lambda lst: jnp.stack(lst)
    mem_shape = (depth, n_p, n_mem, X_HEADS, X_HEAD_DIM)
    return (xp, xs, st(outs_p[0]), st(outs_p[1]), st(outs_p[2]), st(outs_p[3]),
            mk_all.reshape(mem_shape), mv_all.reshape(mem_shape),
            st(outs_s[0]), st(outs_s[1]), st(outs_s[2]), st(outs_s[3]))
```

```python
import functools

import numpy as np
import jax
import jax.numpy as jnp
from jax import lax
from jax.experimental import pallas as pl
from jax.experimental.pallas import tpu as pltpu

F32 = jnp.float32
BF16 = jnp.bfloat16

D_MODEL = 1024
PAST_LEN = 2048
CHUNK = 64
EPS = 1e-6
NEG_INF = -1e30
LANES = 128

A_HEADS = 4
A_HEAD_DIM = 64
D_A = A_HEADS * A_HEAD_DIM
A_LEFT_CHUNKS = 8
A_REACH = A_LEFT_CHUNKS * CHUNK
REL_CLIP = 128
A_SCALE = A_HEAD_DIM ** -0.5

B_HEADS = 4
B_QK_DIM = 64
B_V_DIM = 128
D_BQK = B_HEADS * B_QK_DIM
D_B = B_HEADS * B_V_DIM
RET_GAMMA_EXP0 = 5.0
ROPE_BASE = 10000.0

C_CHANNELS = 256
CONV_WIDTH = 31
CONV_PAD = 32

X_HEADS = 4
X_HEAD_DIM = D_MODEL // X_HEADS

D_FF = 11 * D_MODEL // 4
FF_CHUNK = 256
N_EXPERTS = 8
TOP_K = 2
D_FF_EXPERT = D_FF // 2
MOE_TILE = 512
MOE_CHUNK = 16
MOE_TILE_CAP = 1152
MOE_BLOCK = 512

COL_QA, COL_KVA, COL_QKB, COL_VB, COL_GB, COL_CAB, D_IN = 0, 256, 768, 1280, 1792, 2304, 2816

VMEM_LIMIT = 56 * 1024 * 1024


def _params(sem):
    return pltpu.CompilerParams(dimension_semantics=sem, vmem_limit_bytes=VMEM_LIMIT)


def _rms(x, g):
    return x * lax.rsqrt(jnp.mean(x * x, axis=-1, keepdims=True) + EPS) * g


def _dot(a, b):
    return jnp.dot(a, b, preferred_element_type=F32)


def _dot_nt(a, b):
    return lax.dot_general(a, b, (((1,), (1,)), ((), ())), preferred_element_type=F32)


def _dot_tn(a, b):
    return lax.dot_general(a, b, (((0,), (0,)), ((), ())), preferred_element_type=F32)


def _const_spec(shape):
    return pl.BlockSpec(shape, lambda *_: (0,) * len(shape))


def _in_proj_kernel(x_ref, g_ref, w_ref, aq_ref, akv_ref, bqk_ref, bv_ref, bg_ref, cab_ref):
    h = _rms(x_ref[...], g_ref[...]).astype(BF16)
    aq_ref[...] = _dot(h, w_ref[:, COL_QA:COL_KVA]).astype(BF16)
    akv_ref[...] = _dot(h, w_ref[:, COL_KVA:COL_QKB])
    bqk_ref[...] = _dot(h, w_ref[:, COL_QKB:COL_VB]).astype(BF16)
    bv_ref[...] = _dot(h, w_ref[:, COL_VB:COL_GB]).astype(BF16)
    bg_ref[...] = _dot(h, w_ref[:, COL_GB:COL_CAB]).astype(BF16)
    cab_ref[...] = _dot(h, w_ref[:, COL_CAB:D_IN])


def _conv_tile(ext, w_ref, b_ref, lng_ref, lnb_ref, tt):
    hist = CONV_WIDTH - 1
    acc = jnp.zeros((tt, C_CHANNELS), F32)
    for b in range(8):
        rb = ext if b == 0 else pltpu.roll(ext, b, axis=0)
        for a in range(CONV_PAD // 8):
            d = 8 * a + b
            if d > hist:
                continue
            k = hist - d
            acc = acc + w_ref[k:k + 1, :] * rb[CONV_PAD - 8 * a:CONV_PAD - 8 * a + tt, :]
    y = acc + b_ref[...]
    mu = jnp.mean(y, axis=-1, keepdims=True)
    d0 = y - mu
    var = jnp.mean(d0 * d0, axis=-1, keepdims=True)
    yn = d0 * lax.rsqrt(var + EPS) * lng_ref[...] + lnb_ref[...]
    return yn * jax.nn.sigmoid(yn)


def _in_proj_conv_kernel(x_ref, g_ref, w_ref, cw_ref, cb_ref, lng_ref, lnb_ref,
                         aq_ref, akv_ref, bqk_ref, bv_ref, bg_ref, oc_ref, nc_ref, up, *, tm, tt):
    hist = CONV_WIDTH - 1

    @pl.when(pl.program_id(1) == 0)
    def _():
        up[tm:tm + CONV_PAD, :] = jnp.zeros((CONV_PAD, C_CHANNELS), F32)

    h = _rms(x_ref[...], g_ref[...]).astype(BF16)
    cab = _dot(h, w_ref[:, COL_CAB:D_IN])
    up[0:CONV_PAD, :] = up[tm:tm + CONV_PAD, :]
    up[CONV_PAD:CONV_PAD + tm, :] = cab[:, 0:C_CHANNELS] * jax.nn.sigmoid(cab[:, C_CHANNELS:2 * C_CHANNELS])
    nc_ref[...] = up[CONV_PAD + tm - hist:CONV_PAD + tm, :]
    cols = ((aq_ref, COL_QA, COL_KVA), (akv_ref, COL_KVA, COL_QKB), (bqk_ref, COL_QKB, COL_VB),
            (bv_ref, COL_VB, COL_GB), (bg_ref, COL_GB, COL_CAB))
    for i in range(tm // tt):
        ext = up[i * tt:i * tt + tt + CONV_PAD, :]
        oc_ref[i * tt:(i + 1) * tt, :] = _conv_tile(ext, cw_ref, cb_ref, lng_ref, lnb_ref, tt).astype(BF16)
        for o_ref, lo, hi in cols[i::tm // tt]:
            o_ref[...] = _dot(h, w_ref[:, lo:hi]).astype(o_ref.dtype)


def _in_proj_conv(x, g, w_in, conv_w, conv_b, ln_g, ln_b):
    n, t, _ = x.shape
    tm = min(t, 512)
    tt = min(tm, 128)
    hist = CONV_WIDTH - 1
    widths = ((D_A, BF16), (2 * D_A, F32), (2 * D_BQK, BF16), (D_B, BF16), (D_B, BF16), (C_CHANNELS, BF16))
    tile = lambda w: pl.BlockSpec((None, tm, w), lambda b, j: (b, j, 0))
    return pl.pallas_call(
        functools.partial(_in_proj_conv_kernel, tm=tm, tt=tt),
        grid=(n, t // tm),
        in_specs=[tile(D_MODEL), _const_spec((1, D_MODEL)), _const_spec((D_MODEL, D_IN)),
                  _const_spec((CONV_WIDTH, C_CHANNELS))] + [_const_spec((1, C_CHANNELS))] * 3,
        out_specs=[tile(w) for w, _ in widths] + [pl.BlockSpec((None, hist, C_CHANNELS), lambda b, j: (b, 0, 0))],
        out_shape=[jax.ShapeDtypeStruct((n, t, w), dt) for w, dt in widths]
                  + [jax.ShapeDtypeStruct((n, hist, C_CHANNELS), F32)],
        scratch_shapes=[pltpu.VMEM((CONV_PAD + tm, C_CHANNELS), F32)],
        compiler_params=_params(("parallel", "arbitrary")),
        name="in_proj_conv",
    )(x, g, w_in, conv_w, conv_b, ln_g, ln_b)


def _in_proj(x2d, g, w_in):
    m = x2d.shape[0]
    tm = min(m, 512)
    widths = ((256, BF16), (512, F32), (512, BF16), (512, BF16), (512, BF16), (512, F32))
    return pl.pallas_call(
        _in_proj_kernel,
        grid=(m // tm,),
        in_specs=[pl.BlockSpec((tm, D_MODEL), lambda i: (i, 0)),
                  _const_spec((1, D_MODEL)),
                  _const_spec((D_MODEL, D_IN))],
        out_specs=[pl.BlockSpec((tm, w), lambda i: (i, 0)) for w, _ in widths],
        out_shape=[jax.ShapeDtypeStruct((m, w), dt) for w, dt in widths],
        compiler_params=_params(("parallel",)),
        name="in_proj",
    )(x2d, g, w_in)


def _band_attn_kernel(*refs, t, tq, nsub, has_hist):
    if has_hist:
        aq_ref, akv_ref, hk_ref, hv_ref, bias_ref, o_ref, kc, vc = refs
    else:
        aq_ref, akv_ref, bias_ref, o_ref, kc, vc = refs
    j = pl.program_id(1)
    span = A_REACH + tq

    @pl.when(j == 0)
    def _():
        if has_hist:
            kc[0:A_REACH, :] = hk_ref[...].astype(BF16)
            vc[0:A_REACH, :] = hv_ref[...].astype(BF16)
        else:
            kc[0:A_REACH, :] = jnp.zeros((A_REACH, D_A), BF16)
            vc[0:A_REACH, :] = jnp.zeros((A_REACH, D_A), BF16)
        kc[A_REACH:A_REACH + t, :] = akv_ref[:, 0:D_A].astype(BF16)
        vc[A_REACH:A_REACH + t, :] = akv_ref[:, D_A:2 * D_A].astype(BF16)

    lane = lax.broadcasted_iota(jnp.int32, (tq, LANES), 1)
    col = lax.broadcasted_iota(jnp.int32, (tq, span), 1)
    for sub in range(nsub):
        t0 = pl.multiple_of((j * nsub + sub) * tq, tq)
        q = aq_ref[sub * tq:(sub + 1) * tq, :]
        outs = []
        for p in range(A_HEADS // 2):
            qp = q[:, p * LANES:(p + 1) * LANES]
            kp = kc[pl.ds(t0, span), p * LANES:(p + 1) * LANES]
            vp = vc[pl.ds(t0, span), p * LANES:(p + 1) * LANES]
            o_pair = None
            for hh in range(2):
                own = (lane < A_HEAD_DIM) if hh == 0 else (lane >= A_HEAD_DIM)
                qm = jnp.where(own, qp.astype(F32) * A_SCALE, 0.0).astype(BF16)
                s = _dot_nt(qm, kp) + bias_ref[2 * p + hh]
                if not has_hist:
                    s = jnp.where(col >= A_REACH - t0, s, NEG_INF)
                m = jnp.max(s, axis=-1, keepdims=True)
                e = jnp.exp(s - m)
                l = jnp.sum(e, axis=-1, keepdims=True)
                o = _dot(e.astype(BF16), vp) / l
                o_pair = o if hh == 0 else jnp.where(own, o, o_pair)
            outs.append(o_pair)
        o_ref[sub * tq:(sub + 1) * tq, :] = jnp.concatenate(outs, axis=1).astype(BF16)


def _band_bias(rel_bias_l, tq):
    span = A_REACH + tq
    period = span + tq
    n_far = A_REACH - REL_CLIP + 1
    far = rel_bias_l[:, 2 * REL_CLIP:]
    n_near = min(2 * REL_CLIP, span - n_far)
    near = jnp.flip(rel_bias_l[:, :2 * REL_CLIP], axis=1)[:, :n_near]
    beyond = jnp.broadcast_to(rel_bias_l[:, :1], (A_HEADS, span - n_far - n_near))
    u = jnp.concatenate([jnp.broadcast_to(far, (A_HEADS, n_far)), near, beyond,
                         jnp.broadcast_to(far, (A_HEADS, tq))], axis=1).astype(F32)
    skew = jnp.broadcast_to(u[:, None, :], (A_HEADS, tq, period)).reshape(A_HEADS, tq * period)
    toeplitz = skew[:, :tq * (period - 1)].reshape(A_HEADS, tq, period - 1)[:, :, :span]
    i = np.arange(tq)[:, None]
    j = np.arange(span)[None, :]
    in_band = (j // CHUNK >= i // CHUNK) & (j // CHUNK <= i // CHUNK + A_LEFT_CHUNKS)
    return jnp.where(in_band[None], toeplitz, NEG_INF)


def _band_attn(aq, akv, hist, rel_bias_l):
    n, t, _ = aq.shape
    tq = min(t, 256)
    nsub = 2 if t % (2 * tq) == 0 else 1
    span = A_REACH + tq
    bias = _band_bias(rel_bias_l, tq)
    has_hist = hist is not None
    in_specs = [pl.BlockSpec((None, nsub * tq, D_A), lambda b, j: (b, j, 0)),
                pl.BlockSpec((None, t, 2 * D_A), lambda b, j: (b, 0, 0))]
    args = [aq, akv]
    if has_hist:
        in_specs += [pl.BlockSpec((None, A_REACH, D_A), lambda b, j: (b, 0, 0))] * 2
        args += list(hist)
    in_specs.append(_const_spec((A_HEADS, tq, span)))
    args.append(bias)
    return pl.pallas_call(
        functools.partial(_band_attn_kernel, t=t, tq=tq, nsub=nsub, has_hist=has_hist),
        grid=(n, t // (nsub * tq)),
        in_specs=in_specs,
        out_specs=pl.BlockSpec((None, nsub * tq, D_A), lambda b, j: (b, j, 0)),
        out_shape=jax.ShapeDtypeStruct((n, t, D_A), BF16),
        scratch_shapes=[pltpu.VMEM((A_REACH + t, D_A), BF16)] * 2,
        compiler_params=_params(("parallel", "arbitrary")),
        name="band_attn",
    )(*args)


def _swap_halves(x):
    lane = lax.broadcasted_iota(jnp.int32, x.shape, 1)
    first = (lane % B_QK_DIM) < (B_QK_DIM // 2)
    return jnp.where(first, pltpu.roll(x, LANES - B_QK_DIM // 2, axis=1), pltpu.roll(x, B_QK_DIM // 2, axis=1))


def _retention_kernel(bqk_ref, bv_ref, bg_ref, cos_ref, sin_ref, dmask_ref, qdec_ref, kdec_ref, sdec_ref,
                      s0_ref, gn_ref, ob_ref, sfin_ref, st, *, t, bc):
    for h in range(B_HEADS):
        off = (h % 2) * B_QK_DIM
        st[h] = jnp.zeros((LANES, B_V_DIM), F32)
        st[h, off:off + B_QK_DIM, :] = s0_ref[h]

    lane = lax.broadcasted_iota(jnp.int32, (bc, LANES), 1)

    def chunk(c, carry):
        r0 = pl.multiple_of(c * bc, bc)
        rows = pl.ds(r0, bc)
        cs = cos_ref[rows, :]
        sn = sin_ref[rows, :]
        for p in range(B_HEADS // 2):
            qx = bqk_ref[rows, p * LANES:(p + 1) * LANES].astype(F32)
            kx = bqk_ref[rows, D_BQK + p * LANES:D_BQK + (p + 1) * LANES].astype(F32)
            qr = qx * cs + _swap_halves(qx) * sn
            kr = (kx * cs + _swap_halves(kx) * sn) * (B_QK_DIM ** -0.5)
            for hh in range(2):
                h = 2 * p + hh
                own = (lane < B_QK_DIM) if hh == 0 else (lane >= B_QK_DIM)
                qh = jnp.where(own, qr, 0.0)
                kh = jnp.where(own, kr, 0.0)
                v = bv_ref[rows, h * B_V_DIM:(h + 1) * B_V_DIM]
                att = _dot_nt(qh.astype(BF16), kh.astype(BF16)) * dmask_ref[h]
                o = (_dot(att.astype(BF16), v)
                     + _dot((qh * qdec_ref[h]).astype(BF16), st[h].astype(BF16)))
                st[h] = st[h] * sdec_ref[h] + _dot_tn((kh * kdec_ref[h]).astype(BF16), v)
                mu = jnp.mean(o, axis=-1, keepdims=True)
                d = o - mu
                var = jnp.mean(d * d, axis=-1, keepdims=True)
                on = d * lax.rsqrt(var + EPS) * gn_ref[:, h * B_V_DIM:(h + 1) * B_V_DIM]
                g = bg_ref[rows, h * B_V_DIM:(h + 1) * B_V_DIM].astype(F32)
                ob_ref[rows, h * B_V_DIM:(h + 1) * B_V_DIM] = (g * jax.nn.sigmoid(g) * on).astype(BF16)
        return carry

    lax.fori_loop(0, t // bc, chunk, 0)
    for h in range(B_HEADS):
        off = (h % 2) * B_QK_DIM
        sfin_ref[h] = st[h, off:off + B_QK_DIM, :]


def _retention_tables(t, bc, pos0):
    log_g = jnp.log(1.0 - 2.0 ** (-(RET_GAMMA_EXP0 + jnp.arange(B_HEADS, dtype=F32))))
    i = jnp.arange(bc, dtype=F32)
    diff = i[:, None] - i[None, :]
    dmask = jnp.where(diff[None] >= 0, jnp.exp(jnp.maximum(diff, 0.0)[None] * log_g[:, None, None]), 0.0)
    qdec = jnp.exp((i[None, :] + 1.0) * log_g[:, None])
    kdec = jnp.exp((bc - 1.0 - i)[None, :] * log_g[:, None])
    sdec = jnp.exp(bc * log_g)
    qdec = jnp.broadcast_to(qdec[:, :, None], (B_HEADS, bc, LANES))
    kdec = jnp.broadcast_to(kdec[:, :, None], (B_HEADS, bc, LANES))
    sdec = jnp.broadcast_to(sdec[:, None, None], (B_HEADS, 1, B_V_DIM))
    half = B_QK_DIM // 2
    pos = pos0 + jnp.arange(t, dtype=F32)
    inv_freq = ROPE_BASE ** (-jnp.arange(half, dtype=F32) / half)
    ang = pos[:, None] * inv_freq[None, :]
    cos = jnp.tile(jnp.cos(ang), (1, LANES // half))
    sin = jnp.tile(jnp.concatenate([-jnp.sin(ang), jnp.sin(ang)], axis=1), (1, LANES // B_QK_DIM))
    return cos, sin, dmask, qdec, kdec, sdec


def _retention(bqk, bv, bg, s0, gn_g, pos0):
    n, t, _ = bqk.shape
    bc = min(t, 256)
    tables = _retention_tables(t, bc, pos0)
    seq = lambda w: pl.BlockSpec((None, t, w), lambda b: (b, 0, 0))
    state = pl.BlockSpec((None, B_HEADS, B_QK_DIM, B_V_DIM), lambda b: (b, 0, 0, 0))
    return pl.pallas_call(
        functools.partial(_retention_kernel, t=t, bc=bc),
        grid=(n,),
        in_specs=[seq(2 * D_BQK), seq(D_B), seq(D_B)] + [_const_spec(tb.shape) for tb in tables]
                 + [state, _const_spec((1, D_B))],
        out_specs=[seq(D_B), state],
        out_shape=[jax.ShapeDtypeStruct((n, t, D_B), BF16),
                   jax.ShapeDtypeStruct((n, B_HEADS, B_QK_DIM, B_V_DIM), F32)],
        scratch_shapes=[pltpu.VMEM((B_HEADS, LANES, B_V_DIM), F32)],
        compiler_params=_params(("parallel",)),
        name="retention",
    )(bqk, bv, bg, *tables, s0, gn_g)


def _conv_kernel(cab_ref, buf_ref, w_ref, b_ref, lng_ref, lnb_ref, oc_ref, nc_ref, up, *, t, tt):
    hist = CONV_WIDTH - 1
    up[0:CONV_PAD, :] = jnp.zeros((CONV_PAD, C_CHANNELS), F32)
    up[CONV_PAD - hist:CONV_PAD, :] = buf_ref[...]

    def glu(i, carry):
        rows = pl.ds(pl.multiple_of(i * tt, tt), tt)
        ca = cab_ref[rows, 0:C_CHANNELS]
        cb = cab_ref[rows, C_CHANNELS:2 * C_CHANNELS]
        up[pl.ds(pl.multiple_of(CONV_PAD + i * tt, 8), tt), :] = ca * jax.nn.sigmoid(cb)
        return carry

    lax.fori_loop(0, t // tt, glu, 0)

    def tile(i, carry):
        t0 = pl.multiple_of(i * tt, tt)
        ext = up[pl.ds(t0, tt + CONV_PAD), :]
        oc_ref[pl.ds(t0, tt), :] = _conv_tile(ext, w_ref, b_ref, lng_ref, lnb_ref, tt).astype(BF16)
        return carry

    lax.fori_loop(0, t // tt, tile, 0)
    nc_ref[...] = up[CONV_PAD + t - hist:CONV_PAD + t, :]


def _conv_module(cab, buf, conv_w, conv_b, ln_g, ln_b):
    n, t, _ = cab.shape
    tt = min(t, 128)
    hist = CONV_WIDTH - 1
    return pl.pallas_call(
        functools.partial(_conv_kernel, t=t, tt=tt),
        grid=(n,),
        in_specs=[pl.BlockSpec((None, t, 2 * C_CHANNELS), lambda b: (b, 0, 0)),
                  pl.BlockSpec((None, hist, C_CHANNELS), lambda b: (b, 0, 0)),
                  _const_spec((CONV_WIDTH, C_CHANNELS))] + [_const_spec((1, C_CHANNELS))] * 3,
        out_specs=[pl.BlockSpec((None, t, C_CHANNELS), lambda b: (b, 0, 0)),
                   pl.BlockSpec((None, hist, C_CHANNELS), lambda b: (b, 0, 0))],
        out_shape=[jax.ShapeDtypeStruct((n, t, C_CHANNELS), BF16),
                   jax.ShapeDtypeStruct((n, hist, C_CHANNELS), F32)],
        scratch_shapes=[pltpu.VMEM((CONV_PAD + t, C_CHANNELS), F32)],
        compiler_params=_params(("parallel",)),
        name="conv_module",
    )(cab, buf, conv_w, conv_b, ln_g, ln_b)


def _mem_kv_kernel(mem_ref, wk_ref, wv_ref, mk_ref, mv_ref, mkh_ref, mvh_ref):
    m = mem_ref[...].astype(BF16)
    mk_ref[...] = _dot(m, wk_ref[...])
    mv_ref[...] = _dot(m, wv_ref[...])
    n_mem = mkh_ref.shape[1]
    for src_ref, dst_ref in ((mk_ref, mkh_ref), (mv_ref, mvh_ref)):
        for b in range(mkh_ref.shape[0]):
            for h in range(X_HEADS):
                dst_ref[b, :, h, :] = src_ref[b * n_mem:(b + 1) * n_mem, h * X_HEAD_DIM:(h + 1) * X_HEAD_DIM]


def _mem_kv(mem2d, wk, wv, n_mem):
    depth = wk.shape[0]
    m = mem2d.shape[0]
    tm = 512
    nb = tm // n_mem
    wspec = pl.BlockSpec((None, D_MODEL, D_MODEL), lambda l, i: (l, 0, 0))
    ospec = pl.BlockSpec((None, tm, D_MODEL), lambda l, i: (l, i, 0))
    hspec = pl.BlockSpec((None, nb, n_mem, X_HEADS, X_HEAD_DIM), lambda l, i: (l, i, 0, 0, 0))
    heads = jax.ShapeDtypeStruct((depth, m // n_mem, n_mem, X_HEADS, X_HEAD_DIM), F32)
    return pl.pallas_call(
        _mem_kv_kernel,
        grid=(depth, m // tm),
        in_specs=[pl.BlockSpec((tm, D_MODEL), lambda l, i: (i, 0)), wspec, wspec],
        out_specs=[ospec, ospec, hspec, hspec],
        out_shape=[jax.ShapeDtypeStruct((depth, m, D_MODEL), F32)] * 2 + [heads, heads],
        compiler_params=_params(("parallel", "parallel")),
        name="mem_kv",
    )(mem2d, wk, wv)


def _route(logits):
    lane = lax.broadcasted_iota(jnp.int32, logits.shape, 1).astype(F32)
    valid = lane < N_EXPERTS
    lg = jnp.where(valid, logits, NEG_INF)
    e = jnp.exp(lg - jnp.max(lg, axis=-1, keepdims=True))
    probs = e / jnp.sum(e, axis=-1, keepdims=True)
    p1 = jnp.max(probs, axis=-1, keepdims=True)
    i1 = jnp.min(jnp.where(probs == p1, lane, float(LANES)), axis=-1, keepdims=True)
    rest = jnp.where(lane == i1, -1.0, probs)
    rest = jnp.where(valid, rest, -1.0)
    p2 = jnp.max(rest, axis=-1, keepdims=True)
    i2 = jnp.min(jnp.where(rest == p2, lane, float(LANES)), axis=-1, keepdims=True)
    tot = p1 + p2
    return lane, i1, i2, p1 / tot, p2 / tot


def _sort_tile_by_expert(logits, h_hi, tri_ref, hs_ref, route_ref, meta_ref):
    tm = logits.shape[0]
    lt = jnp.transpose(logits)[0:N_EXPERTS, :]
    sub = lax.broadcasted_iota(jnp.int32, (N_EXPERTS, tm), 0).astype(F32)
    ex = jnp.exp(lt - jnp.max(lt, axis=0, keepdims=True))
    probs = ex / jnp.sum(ex, axis=0, keepdims=True)
    p1 = jnp.max(probs, axis=0, keepdims=True)
    i1 = jnp.min(jnp.where(probs == p1, sub, float(N_EXPERTS)), axis=0, keepdims=True)
    rest = jnp.where(sub == i1, -1.0, probs)
    p2 = jnp.max(rest, axis=0, keepdims=True)
    i2 = jnp.min(jnp.where(rest == p2, sub, float(N_EXPERTS)), axis=0, keepdims=True)
    w1 = p1 / (p1 + p2)
    w2 = p2 / (p1 + p2)
    oh1 = jnp.where(sub == i1, 1.0, 0.0)
    oh2 = jnp.where(sub == i2, 1.0, 0.0)
    before1 = _dot(oh1.astype(BF16), tri_ref[...])
    before2 = _dot(oh2.astype(BF16), tri_ref[...])
    cnt1 = jnp.sum(oh1, axis=1, keepdims=True)
    cnt2 = jnp.sum(oh2, axis=1, keepdims=True)
    padded = jnp.floor((cnt1 + cnt2 + (MOE_CHUNK - 1.0)) * (1.0 / MOE_CHUNK)) * MOE_CHUNK
    run = jnp.broadcast_to(padded, (N_EXPERTS, LANES))
    sub8 = lax.broadcasted_iota(jnp.int32, (N_EXPERTS, LANES), 0)
    incl = run
    for k in (1, 2, 4):
        incl = incl + jnp.where(sub8 >= k, pltpu.roll(incl, k, axis=0), 0.0)
    start = (incl - run)[:, 0:1]
    d1 = jnp.sum(oh1 * (start + before1), axis=0, keepdims=True)
    d2 = jnp.sum(oh2 * (start + cnt1 + before2), axis=0, keepdims=True)
    row = lax.broadcasted_iota(jnp.int32, (MOE_TILE_CAP, tm), 0).astype(F32)
    perm = jnp.where(row == d1, 1.0, jnp.where(row == d2, 1.0, 0.0)).astype(BF16)
    hs_ref[...] = _dot(perm, h_hi).astype(BF16)
    info = jnp.concatenate([d1, d2, w1, w2, jnp.zeros((LANES - 4, tm), F32)], axis=0)
    route_ref[...] = jnp.transpose(info)
    lane8 = lax.broadcasted_iota(jnp.int32, (N_EXPERTS, LANES), 1)
    meta_ref[...] = jnp.where(lane8 == 0, start * (1.0 / MOE_CHUNK),
                              jnp.where(lane8 == 1, padded * (1.0 / MOE_CHUNK), 0.0))


def _post_mix_kernel(*refs, mode):
    (x_ref, oa_ref, ob_ref, oc_ref, wout_ref, gmem_ref, wq_ref, mk_ref, mv_ref, wo_ref, gffn_ref) = refs[:11]
    if mode == "dense":
        x2_ref, h_ref = refs[11:]
    elif mode == "experts":
        router_ref, x2_ref, h_ref, route_ref = refs[11:]
    else:
        router_ref, tri_ref, x2_ref, h_ref, route_ref, meta_ref = refs[11:]
    y = (_dot(oa_ref[...], wout_ref[0:D_A, :])
         + _dot(ob_ref[...], wout_ref[D_A:D_A + D_B, :])
         + _dot(oc_ref[...], wout_ref[D_A + D_B:D_MODEL, :]))
    x1 = x_ref[...] + y
    q = _dot(_rms(x1, gmem_ref[...]).astype(BF16), wq_ref[...]).astype(BF16)
    outs = []
    for h in range(X_HEADS):
        cols = slice(h * X_HEAD_DIM, (h + 1) * X_HEAD_DIM)
        s = _dot_nt(q[:, cols], mk_ref[:, cols].astype(BF16)) * (X_HEAD_DIM ** -0.5)
        e = jnp.exp(s - jnp.max(s, axis=-1, keepdims=True))
        l = jnp.sum(e, axis=-1, keepdims=True)
        outs.append((_dot(e.astype(BF16), mv_ref[:, cols].astype(BF16)) / l).astype(BF16))
    x2 = x1 + _dot(jnp.concatenate(outs, axis=1), wo_ref[...])
    x2_ref[...] = x2
    hf = _rms(x2, gffn_ref[...])
    h_hi = hf.astype(BF16)
    if mode != "grouped":
        h_ref[...] = h_hi
    if mode != "dense":
        h_lo = (hf - h_hi.astype(F32)).astype(BF16)
        logits = (_dot(h_hi, router_ref[0]) + _dot(h_lo, router_ref[0])) + _dot(h_hi, router_ref[1])
        if mode == "experts":
            lane, i1, i2, w1, w2 = _route(logits)
            route_ref[...] = jnp.where(lane == i1, w1, jnp.where(lane == i2, w2, 0.0))
        else:
            _sort_tile_by_expert(logits, h_hi, tri_ref, h_ref, route_ref, meta_ref)


def _post_mix(x, oa, ob, oc, w_out, g_mem, wq, mk, mv, mem_base, wo, g_ffn, router, mode):
    n, t, _ = x.shape
    tm = min(t, 512)
    tile = lambda w: pl.BlockSpec((None, tm, w), lambda b, i: (b, i, 0))
    mem = pl.BlockSpec((None,) + mk.shape[1:], lambda b, i: (mem_base + b, 0, 0))
    sq = _const_spec((D_MODEL, D_MODEL))
    vec = _const_spec((1, D_MODEL))
    in_specs = [tile(D_MODEL), tile(D_A), tile(D_B), tile(C_CHANNELS), sq, vec, sq, mem, mem, sq, vec]
    args = [x, oa, ob, oc, w_out, g_mem, wq, mk, mv, wo, g_ffn]
    out_specs = [tile(D_MODEL)]
    out_shape = [jax.ShapeDtypeStruct((n, t, D_MODEL), F32)]
    if mode != "dense":
        in_specs.append(_const_spec((2, D_MODEL, LANES)))
        args.append(router)
    if mode == "grouped":
        assert tm == MOE_TILE
        in_specs.append(_const_spec((tm, tm)))
        args.append(jnp.triu(jnp.ones((tm, tm), BF16), k=1))
        out_specs.append(pl.BlockSpec((None, MOE_TILE_CAP, D_MODEL), lambda b, i: (b, i, 0)))
        out_shape.append(jax.ShapeDtypeStruct((n, t // tm * MOE_TILE_CAP, D_MODEL), BF16))
    else:
        out_specs.append(tile(D_MODEL))
        out_shape.append(jax.ShapeDtypeStruct((n, t, D_MODEL), BF16))
    if mode != "dense":
        out_specs.append(tile(LANES))
        out_shape.append(jax.ShapeDtypeStruct((n, t, LANES), F32))
    if mode == "grouped":
        out_specs.append(pl.BlockSpec((None, N_EXPERTS, LANES), lambda b, i: (b, i, 0)))
        out_shape.append(jax.ShapeDtypeStruct((n, t // tm * N_EXPERTS, LANES), F32))
    return pl.pallas_call(
        functools.partial(_post_mix_kernel, mode=mode),
        grid=(n, t // tm),
        in_specs=in_specs,
        out_specs=out_specs,
        out_shape=out_shape,
        compiler_params=_params(("parallel", "parallel")),
        name="post_mix",
    )(*args)


def _ffn_kernel(x_ref, h_ref, wg_ref, wu_ref, wd_ref, o_ref):
    h = h_ref[...]
    acc = x_ref[...]
    for c in range(D_FF // FF_CHUNK):
        cols = slice(c * FF_CHUNK, (c + 1) * FF_CHUNK)
        g = _dot(h, wg_ref[:, cols])
        u = _dot(h, wu_ref[:, cols])
        acc = acc + _dot((g * jax.nn.sigmoid(g) * u).astype(BF16), wd_ref[cols, :])
    o_ref[...] = acc


def _ffn(x2d, h2d, wg, wu, wd):
    m = x2d.shape[0]
    tm = min(m, 512)
    row = pl.BlockSpec((tm, D_MODEL), lambda i: (i, 0))
    return pl.pallas_call(
        _ffn_kernel,
        grid=(m // tm,),
        in_specs=[row, row, _const_spec((D_MODEL, D_FF)), _const_spec((D_MODEL, D_FF)),
                  _const_spec((D_FF, D_MODEL))],
        out_specs=row,
        out_shape=jax.ShapeDtypeStruct((m, D_MODEL), F32),
        compiler_params=_params(("parallel",)),
        name="ffn_dense",
    )(x2d, h2d, wg, wu, wd)


def _moe_kernel(x_ref, h_ref, comb_ref, wg_ref, wu_ref, wd_ref, gfin_ref, o_ref, acc):
    e = pl.program_id(1)

    @pl.when(e == 0)
    def _():
        acc[...] = x_ref[...]

    h = h_ref[...]
    g = _dot(h, wg_ref[...])
    u = _dot(h, wu_ref[...])
    y = _dot((g * jax.nn.sigmoid(g) * u).astype(BF16), wd_ref[...])
    comb = comb_ref[...]
    lane = lax.broadcasted_iota(jnp.int32, comb.shape, 1)
    ce = jnp.sum(jnp.where(lane == e, comb, 0.0), axis=-1, keepdims=True)
    acc[...] += ce * y

    @pl.when(e == N_EXPERTS - 1)
    def _():
        o_ref[...] = _rms(acc[...], gfin_ref[...])


def _moe(x2d, h2d, comb2d, wg, wu, wd, g_fin):
    m = x2d.shape[0]
    tm = min(m, 512)
    row = lambda w: pl.BlockSpec((tm, w), lambda i, e: (i, 0))
    return pl.pallas_call(
        _moe_kernel,
        grid=(m // tm, N_EXPERTS),
        in_specs=[row(D_MODEL), row(D_MODEL), row(LANES),
                  pl.BlockSpec((None, D_MODEL, D_FF_EXPERT), lambda i, e: (e, 0, 0)),
                  pl.BlockSpec((None, D_MODEL, D_FF_EXPERT), lambda i, e: (e, 0, 0)),
                  pl.BlockSpec((None, D_FF_EXPERT, D_MODEL), lambda i, e: (e, 0, 0)),
                  _const_spec((1, D_MODEL))],
        out_specs=row(D_MODEL),
        out_shape=jax.ShapeDtypeStruct((m, D_MODEL), F32),
        scratch_shapes=[pltpu.VMEM((tm, D_MODEL), F32)],
        compiler_params=_params(("parallel", "arbitrary")),
        name="moe_experts",
    )(x2d, h2d, comb2d, wg, wu, wd, g_fin)


def _moe_plan(meta, n_tiles):
    cpb = MOE_BLOCK // MOE_CHUNK
    tile_chunks = MOE_TILE_CAP // MOE_CHUNK
    nb = -(-(n_tiles * tile_chunks + N_EXPERTS * (cpb - 1)) // cpb)
    first = meta[:, :, 0].astype(jnp.int32)
    cnt = meta[:, :, 1].astype(jnp.int32)
    tot = jnp.sum(cnt, axis=0)
    eend = jnp.cumsum((tot + cpb - 1) // cpb * cpb)
    q = jnp.arange(nb * cpb, dtype=jnp.int32)
    e_q = jnp.minimum(jnp.sum((q[:, None] >= eend[None, :]).astype(jnp.int32), axis=1), N_EXPERTS - 1)
    oh_e = (e_q[:, None] == jnp.arange(N_EXPERTS, dtype=jnp.int32)[None, :]).astype(jnp.int32)
    pick = lambda table: jnp.sum(oh_e[:, :, None] * table.T[None, :, :], axis=1)
    j = q - jnp.sum(oh_e * (eend - (tot + cpb - 1) // cpb * cpb)[None, :], axis=1)
    valid = (j < jnp.sum(oh_e * tot[None, :], axis=1)) & (q < eend[N_EXPERTS - 1])
    incl = jnp.cumsum(cnt, axis=0)
    tile_q = jnp.minimum(jnp.sum((j[:, None] >= pick(incl)).astype(jnp.int32), axis=1), n_tiles - 1)
    oh_t = (tile_q[:, None] == jnp.arange(n_tiles, dtype=jnp.int32)[None, :]).astype(jnp.int32)
    chunk = (tile_q * tile_chunks + jnp.sum(oh_t * pick(first), axis=1)
             + j - jnp.sum(oh_t * pick(incl - cnt), axis=1))
    k = jnp.cumsum(jnp.logical_not(valid).astype(jnp.int32)) - 1
    used = jnp.sum(cnt, axis=1)
    free_incl = jnp.cumsum(tile_chunks - used)
    tile_k = jnp.minimum(jnp.sum((k[:, None] >= free_incl[None, :]).astype(jnp.int32), axis=1), n_tiles - 1)
    oh_k = (tile_k[:, None] == jnp.arange(n_tiles, dtype=jnp.int32)[None, :]).astype(jnp.int32)
    in_tile = (tile_k * tile_chunks + jnp.sum(oh_k * used[None, :], axis=1)
               + k - jnp.sum(oh_k * (free_incl - (tile_chunks - used))[None, :], axis=1))
    n_free = free_incl[n_tiles - 1]
    spare = jnp.where(k < n_free, in_tile, n_tiles * tile_chunks + k - n_free)
    src = jnp.where(valid, chunk, 0)
    dst = jnp.where(valid, chunk, spare)
    blk_expert = e_q[::cpb]
    n_active = (eend[N_EXPERTS - 1] // cpb).reshape(1)
    return blk_expert, n_active, src.reshape(nb, 1, cpb), dst.reshape(nb, 1, cpb)


def _moe_grouped_kernel(be_ref, nact_ref, src_ref, src_next_ref, dst_ref, hs_hbm, wg_ref, wu_ref, wd_ref,
                        y_hbm, xbuf, ybuf, sem_in, sem_out):
    b = pl.program_id(0)
    nb = pl.num_programs(0)
    slot = b % 2
    cpb = MOE_BLOCK // MOE_CHUNK

    def chunk_rows(i):
        return pl.ds(pl.multiple_of(i * MOE_CHUNK, MOE_CHUNK), MOE_CHUNK)

    def gather(idx_ref, s):
        for c in range(cpb):
            pltpu.make_async_copy(hs_hbm.at[chunk_rows(idx_ref[0, c]), :],
                                  xbuf.at[s, c * MOE_CHUNK:(c + 1) * MOE_CHUNK, :],
                                  sem_in.at[s]).start(priority=c % 2)

    def wait_gather(s):
        pltpu.make_async_copy(hs_hbm.at[pl.ds(0, MOE_BLOCK), :], xbuf.at[s], sem_in.at[s]).wait()

    def wait_scatter(s):
        pltpu.make_async_copy(ybuf.at[s], y_hbm.at[pl.ds(0, MOE_BLOCK), :], sem_out.at[s]).wait()

    @pl.when(b == 0)
    def _():
        gather(src_ref, 0)

    @pl.when(b + 1 < nb)
    def _():
        gather(src_next_ref, 1 - slot)

    wait_gather(slot)

    @pl.when(b >= 2)
    def _():
        wait_scatter(slot)

    @pl.when(b < nact_ref[0])
    def _():
        x = xbuf[slot]
        g = _dot(x, wg_ref[...])
        u = _dot(x, wu_ref[...])
        ybuf[slot] = _dot((g * jax.nn.sigmoid(g) * u).astype(BF16), wd_ref[...]).astype(BF16)

    @pl.when(b >= nact_ref[0])
    def _():
        ybuf[slot] = jnp.zeros((MOE_BLOCK, D_MODEL), BF16)

    for c in range(cpb):
        pltpu.make_async_copy(ybuf.at[slot, c * MOE_CHUNK:(c + 1) * MOE_CHUNK, :],
                              y_hbm.at[chunk_rows(dst_ref[0, c]), :],
                              sem_out.at[slot]).start(priority=c % 2)

    @pl.when(b == nb - 1)
    def _():
        wait_scatter(slot)
        wait_scatter(1 - slot)


def _moe_grouped(hs, blk_expert, n_active, src, dst, wg, wu, wd):
    nb, _, cpb = src.shape
    assert nb >= 2
    idx_spec = lambda f: pl.BlockSpec((None, 1, cpb), f, memory_space=pltpu.SMEM)
    wspec = lambda shape: pl.BlockSpec((None,) + shape, lambda b, be, na: (be[b], 0, 0))
    return pl.pallas_call(
        _moe_grouped_kernel,
        grid_spec=pltpu.PrefetchScalarGridSpec(
            num_scalar_prefetch=2,
            grid=(nb,),
            in_specs=[idx_spec(lambda b, be, na: (b, 0, 0)),
                      idx_spec(lambda b, be, na: (jnp.minimum(b + 1, nb - 1), 0, 0)),
                      idx_spec(lambda b, be, na: (b, 0, 0)),
                      pl.BlockSpec(memory_space=pl.ANY),
                      wspec((D_MODEL, D_FF_EXPERT)), wspec((D_MODEL, D_FF_EXPERT)),
                      wspec((D_FF_EXPERT, D_MODEL))],
            out_specs=pl.BlockSpec(memory_space=pl.ANY),
            scratch_shapes=[pltpu.VMEM((2, MOE_BLOCK, D_MODEL), BF16), pltpu.VMEM((2, MOE_BLOCK, D_MODEL), BF16),
                            pltpu.SemaphoreType.DMA((2,)), pltpu.SemaphoreType.DMA((2,))]),
        out_shape=jax.ShapeDtypeStruct((nb * MOE_BLOCK, D_MODEL), BF16),
        compiler_params=_params(("arbitrary",)),
        name="moe_grouped",
    )(blk_expert, n_active, src, src, dst, hs, wg, wu, wd)


def _moe_combine_kernel(x_ref, route_ref, y_ref, gfin_ref, o_ref):
    tm = x_ref.shape[0]
    route = route_ref[...]
    d1, d2, w1, w2 = route[:, 0:1], route[:, 1:2], route[:, 2:3], route[:, 3:4]
    col = lax.broadcasted_iota(jnp.int32, (tm, MOE_TILE_CAP), 1).astype(F32)
    wm = jnp.where(col == d1, w1, jnp.where(col == d2, w2, 0.0))
    wm_hi = wm.astype(BF16)
    wm_lo = (wm - wm_hi.astype(F32)).astype(BF16)
    y = y_ref[...]
    acc = x_ref[...] + (_dot(wm_hi, y) + _dot(wm_lo, y))
    o_ref[...] = _rms(acc, gfin_ref[...])


def _moe_combine(x2d, route2d, ys, g_fin):
    m = x2d.shape[0]
    tm = MOE_TILE
    return pl.pallas_call(
        _moe_combine_kernel,
        grid=(m // tm,),
        in_specs=[pl.BlockSpec((tm, D_MODEL), lambda i: (i, 0)),
                  pl.BlockSpec((tm, LANES), lambda i: (i, 0)),
                  pl.BlockSpec((MOE_TILE_CAP, D_MODEL), lambda i: (i, 0)),
                  _const_spec((1, D_MODEL))],
        out_specs=pl.BlockSpec((tm, D_MODEL), lambda i: (i, 0)),
        out_shape=jax.ShapeDtypeStruct((m, D_MODEL), F32),
        compiler_params=_params(("parallel",)),
        name="moe_combine",
    )(x2d, route2d, ys, g_fin)


def _layer(x, pos0, attn_hist, ret_state, conv_state, mk, mv, mem_base, w):
    n, t, _ = x.shape
    conv_w = (w["conv_w"], w["conv_b"], w["conv_ln_g"], w["conv_ln_b"])
    if conv_state is None:
        aq, akv, bqk, bv, bg, oc, new_conv = _in_proj_conv(x, w["norm_mix_g"], w["w_in"], *conv_w)
    else:
        aq, akv, bqk, bv, bg, cab = _in_proj(x.reshape(n * t, D_MODEL), w["norm_mix_g"], w["w_in"])
        r3 = lambda a: a.reshape(n, t, a.shape[-1])
        aq, akv, bqk, bv, bg, cab = map(r3, (aq, akv, bqk, bv, bg, cab))
        oc, new_conv = _conv_module(cab, conv_state, *conv_w)
    oa = _band_attn(aq, akv, attn_hist, w["rel_bias"])
    ob, new_s = _retention(bqk, bv, bg, ret_state, w["ret_gn_g"], pos0)
    m = n * t
    if "router" not in w:
        mode = "dense"
    elif m >= N_EXPERTS * MOE_BLOCK and t % MOE_TILE == 0:
        mode = "grouped"
    else:
        mode = "experts"
    post = _post_mix(x, oa, ob, oc, w["w_out"], w["norm_mem_g"], w["wx_q"], mk, mv, mem_base, w["wx_o"],
                     w["norm_ffn_g"], w.get("router"), mode)
    x2 = post[0].reshape(m, D_MODEL)
    if mode == "dense":
        x3 = _ffn(x2, post[1].reshape(m, D_MODEL), w["ffn_g"], w["ffn_u"], w["ffn_d"])
    elif mode == "experts":
        x3 = _moe(x2, post[1].reshape(m, D_MODEL), post[2].reshape(m, LANES),
                  w["moe_g"], w["moe_u"], w["moe_d"], w["final_g"])
    else:
        n_tiles = m // MOE_TILE
        blk_expert, n_active, src, dst = _moe_plan(post[3].reshape(n_tiles, N_EXPERTS, LANES), n_tiles)
        ys = _moe_grouped(post[1].reshape(n_tiles * MOE_TILE_CAP, D_MODEL), blk_expert, n_active, src, dst,
                          w["moe_g"], w["moe_u"], w["moe_d"])
        x3 = _moe_combine(x2, post[2].reshape(m, LANES), ys, w["final_g"])
    keep = min(A_REACH, t)
    new_kv = akv[:, t - keep:, :]
    new_k = new_kv[..., :D_A].reshape(n, keep, A_HEADS, A_HEAD_DIM)
    new_v = new_kv[..., D_A:].reshape(n, keep, A_HEADS, A_HEAD_DIM)
    return x3.reshape(n, t, D_MODEL), new_k, new_v, new_s, new_conv


def kernel(x_prompt, x_sample, cache_attn_k, cache_attn_v, state_ret, state_conv, cache_mem_k, cache_mem_v,
           mem_prompt, norm_mix_g, w_in, rel_bias, ret_gn_g, conv_w, conv_b, conv_ln_g, conv_ln_b, w_out,
           norm_mem_g, wx_q, wx_k, wx_v, wx_o, norm_ffn_g, ffn_w_gate, ffn_w_up, ffn_w_down,
           router_w, moe_w_gate, moe_w_up, moe_w_down, final_norm_g):
    depth = w_in.shape[0]
    assert depth == 2, "layer 0 dense FFN, layer 1 experts + closing norm"
    n_p, _, _ = x_prompt.shape
    n_s, t_s, _ = x_sample.shape
    n_mem = mem_prompt.shape[1]
    row = lambda a: a.reshape(1, -1).astype(F32)
    xp, xs = x_prompt, x_sample
    mem2d = mem_prompt.reshape(n_p * n_mem, D_MODEL)
    outs_p = [[] for _ in range(4)]
    outs_s = [[] for _ in range(4)]
    mk_all, mv_all, mk_heads, mv_heads = _mem_kv(mem2d, wx_k.astype(BF16), wx_v.astype(BF16), n_mem)
    mk_p = mk_all.reshape(depth * n_p, n_mem, D_MODEL)
    mv_p = mv_all.reshape(depth * n_p, n_mem, D_MODEL)
    mk_s = cache_mem_k.reshape(depth * n_s, n_mem, D_MODEL)
    mv_s = cache_mem_v.reshape(depth * n_s, n_mem, D_MODEL)
    for l in range(depth):
        w = dict(norm_mix_g=row(norm_mix_g[l]), w_in=w_in[l].astype(BF16), rel_bias=rel_bias[l],
                 ret_gn_g=row(ret_gn_g[l]), conv_w=conv_w[l].astype(F32), conv_b=row(conv_b[l]),
                 conv_ln_g=row(conv_ln_g[l]), conv_ln_b=row(conv_ln_b[l]), w_out=w_out[l].astype(BF16),
                 norm_mem_g=row(norm_mem_g[l]), wx_q=wx_q[l].astype(BF16), wx_o=wx_o[l].astype(BF16),
                 norm_ffn_g=row(norm_ffn_g[l]))
        if l % 2 == 0:
            i = l // 2
            w.update(ffn_g=ffn_w_gate[i].astype(BF16), ffn_u=ffn_w_up[i].astype(BF16),
                     ffn_d=ffn_w_down[i].astype(BF16))
        else:
            i = l // 2
            r = jnp.pad(router_w[i].astype(F32), ((0, 0), (0, LANES - N_EXPERTS)))
            r_hi = r.astype(BF16)
            r_lo = (r - r_hi.astype(F32)).astype(BF16)
            w.update(router=jnp.stack([r_hi, r_lo]),
                     moe_g=moe_w_gate[i].astype(BF16), moe_u=moe_w_up[i].astype(BF16),
                     moe_d=moe_w_down[i].astype(BF16),
                     final_g=row(final_norm_g) if l == depth - 1 else None)
        xp, k_new, v_new, s_new, c_new = _layer(
            xp, 0.0, None, jnp.zeros((n_p, B_HEADS, B_QK_DIM, B_V_DIM), F32), None, mk_p, mv_p, l * n_p, w)
        for lst, a in zip(outs_p, (k_new, v_new, s_new, c_new)):
            lst.append(a)
        hist = (cache_attn_k[l].reshape(n_s, A_REACH, D_A), cache_attn_v[l].reshape(n_s, A_REACH, D_A))
        xs, k_new, v_new, s_new, c_new = _layer(
            xs, float(PAST_LEN), hist, state_ret[l], state_conv[l], mk_s, mv_s, l * n_s, w)
        for lst, a in zip(outs_s, (k_new, v_new, s_new, c_new)):
            lst.append(a)
    st = lambda lst: jnp.stack(lst)
    return (xp, xs, st(outs_p[0]), st(outs_p[1]), st(outs_p[2]), st(outs_p[3]), mk_heads, mv_heads,
            st(outs_s[0]), st(outs_s[1]), st(outs_s[2]), st(outs_s[3]))
```

```python
import functools

import numpy as np
import jax
import jax.numpy as jnp
from jax import lax
from jax.experimental import pallas as pl
from jax.experimental.pallas import tpu as pltpu

F32 = jnp.float32
BF16 = jnp.bfloat16

D_MODEL = 1024
PAST_LEN = 2048
CHUNK = 64
EPS = 1e-6
NEG_INF = -1e30
LANES = 128

A_HEADS = 4
A_HEAD_DIM = 64
D_A = A_HEADS * A_HEAD_DIM
A_LEFT_CHUNKS = 8
A_REACH = A_LEFT_CHUNKS * CHUNK
REL_CLIP = 128
A_SCALE = A_HEAD_DIM ** -0.5

B_HEADS = 4
B_QK_DIM = 64
B_V_DIM = 128
D_BQK = B_HEADS * B_QK_DIM
D_B = B_HEADS * B_V_DIM
RET_GAMMA_EXP0 = 5.0
ROPE_BASE = 10000.0

C_CHANNELS = 256
CONV_WIDTH = 31
CONV_PAD = 32

X_HEADS = 4
X_HEAD_DIM = D_MODEL // X_HEADS

D_FF = 11 * D_MODEL // 4
FF_CHUNK = 256
N_EXPERTS = 8
TOP_K = 2
D_FF_EXPERT = D_FF // 2
MOE_TILE = 512
MOE_CHUNK = 16
MOE_TILE_CAP = 1152
MOE_BLOCK = 512

COL_QA, COL_KVA, COL_QKB, COL_VB, COL_GB, COL_CAB, D_IN = 0, 256, 768, 1280, 1792, 2304, 2816

VMEM_LIMIT = 56 * 1024 * 1024


def _params(sem):
    return pltpu.CompilerParams(dimension_semantics=sem, vmem_limit_bytes=VMEM_LIMIT)


def _rms(x, g):
    return x * lax.rsqrt(jnp.mean(x * x, axis=-1, keepdims=True) + EPS) * g


def _dot(a, b):
    return jnp.dot(a, b, preferred_element_type=F32)


def _dot_nt(a, b):
    return lax.dot_general(a, b, (((1,), (1,)), ((), ())), preferred_element_type=F32)


def _dot_tn(a, b):
    return lax.dot_general(a, b, (((0,), (0,)), ((), ())), preferred_element_type=F32)


def _const_spec(shape):
    return pl.BlockSpec(shape, lambda *_: (0,) * len(shape), pipeline_mode=pl.Buffered(1))


def _in_proj_kernel(x_ref, g_ref, w_ref, aq_ref, akv_ref, bqk_ref, bv_ref, bg_ref, cab_ref):
    h = _rms(x_ref[...], g_ref[...]).astype(BF16)
    aq_ref[...] = _dot(h, w_ref[:, COL_QA:COL_KVA]).astype(BF16)
    akv_ref[...] = _dot(h, w_ref[:, COL_KVA:COL_QKB])
    bqk_ref[...] = _dot(h, w_ref[:, COL_QKB:COL_VB]).astype(BF16)
    bv_ref[...] = _dot(h, w_ref[:, COL_VB:COL_GB]).astype(BF16)
    bg_ref[...] = _dot(h, w_ref[:, COL_GB:COL_CAB]).astype(BF16)
    cab_ref[...] = _dot(h, w_ref[:, COL_CAB:D_IN])


def _conv_tile(ext, w_ref, b_ref, lng_ref, lnb_ref, tt):
    hist = CONV_WIDTH - 1
    acc = jnp.zeros((tt, C_CHANNELS), F32)
    for b in range(8):
        rb = ext if b == 0 else pltpu.roll(ext, b, axis=0)
        for a in range(CONV_PAD // 8):
            d = 8 * a + b
            if d > hist:
                continue
            k = hist - d
            acc = acc + w_ref[k:k + 1, :] * rb[CONV_PAD - 8 * a:CONV_PAD - 8 * a + tt, :]
    y = acc + b_ref[...]
    mu = jnp.mean(y, axis=-1, keepdims=True)
    d0 = y - mu
    var = jnp.mean(d0 * d0, axis=-1, keepdims=True)
    yn = d0 * lax.rsqrt(var + EPS) * lng_ref[...] + lnb_ref[...]
    return yn * jax.nn.sigmoid(yn)


def _in_proj_conv_kernel(x_ref, g_ref, w_ref, cw_ref, cb_ref, lng_ref, lnb_ref,
                         aq_ref, akv_ref, bqk_ref, bv_ref, bg_ref, oc_ref, nc_ref, up, *, tm, tt):
    hist = CONV_WIDTH - 1

    @pl.when(pl.program_id(1) == 0)
    def _():
        up[tm:tm + CONV_PAD, :] = jnp.zeros((CONV_PAD, C_CHANNELS), F32)

    h = _rms(x_ref[...], g_ref[...]).astype(BF16)
    cab = _dot(h, w_ref[:, COL_CAB:D_IN])
    up[0:CONV_PAD, :] = up[tm:tm + CONV_PAD, :]
    up[CONV_PAD:CONV_PAD + tm, :] = cab[:, 0:C_CHANNELS] * jax.nn.sigmoid(cab[:, C_CHANNELS:2 * C_CHANNELS])
    nc_ref[...] = up[CONV_PAD + tm - hist:CONV_PAD + tm, :]
    cols = ((aq_ref, COL_QA, COL_KVA), (akv_ref, COL_KVA, COL_QKB), (bqk_ref, COL_QKB, COL_VB),
            (bv_ref, COL_VB, COL_GB), (bg_ref, COL_GB, COL_CAB))
    for i in range(tm // tt):
        ext = up[i * tt:i * tt + tt + CONV_PAD, :]
        oc_ref[i * tt:(i + 1) * tt, :] = _conv_tile(ext, cw_ref, cb_ref, lng_ref, lnb_ref, tt).astype(BF16)
        for o_ref, lo, hi in cols[i::tm // tt]:
            o_ref[...] = _dot(h, w_ref[:, lo:hi]).astype(o_ref.dtype)


def _in_proj_conv(x, g, w_in, conv_w, conv_b, ln_g, ln_b):
    n, t, _ = x.shape
    tm = min(t, 512)
    tt = min(tm, 128)
    hist = CONV_WIDTH - 1
    widths = ((D_A, BF16), (2 * D_A, F32), (2 * D_BQK, BF16), (D_B, BF16), (D_B, BF16), (C_CHANNELS, BF16))
    tile = lambda w: pl.BlockSpec((None, tm, w), lambda b, j: (b, j, 0))
    return pl.pallas_call(
        functools.partial(_in_proj_conv_kernel, tm=tm, tt=tt),
        grid=(n, t // tm),
        in_specs=[tile(D_MODEL), _const_spec((1, D_MODEL)), _const_spec((D_MODEL, D_IN)),
                  _const_spec((CONV_WIDTH, C_CHANNELS))] + [_const_spec((1, C_CHANNELS))] * 3,
        out_specs=[tile(w) for w, _ in widths] + [pl.BlockSpec((None, hist, C_CHANNELS), lambda b, j: (b, 0, 0))],
        out_shape=[jax.ShapeDtypeStruct((n, t, w), dt) for w, dt in widths]
                  + [jax.ShapeDtypeStruct((n, hist, C_CHANNELS), F32)],
        scratch_shapes=[pltpu.VMEM((CONV_PAD + tm, C_CHANNELS), F32)],
        compiler_params=_params(("parallel", "arbitrary")),
        name="in_proj_conv",
    )(x, g, w_in, conv_w, conv_b, ln_g, ln_b)


def _in_proj(x2d, g, w_in):
    m = x2d.shape[0]
    tm = min(m, 512)
    widths = ((256, BF16), (512, F32), (512, BF16), (512, BF16), (512, BF16), (512, F32))
    return pl.pallas_call(
        _in_proj_kernel,
        grid=(m // tm,),
        in_specs=[pl.BlockSpec((tm, D_MODEL), lambda i: (i, 0)),
                  _const_spec((1, D_MODEL)),
                  _const_spec((D_MODEL, D_IN))],
        out_specs=[pl.BlockSpec((tm, w), lambda i: (i, 0)) for w, _ in widths],
        out_shape=[jax.ShapeDtypeStruct((m, w), dt) for w, dt in widths],
        compiler_params=_params(("parallel",)),
        name="in_proj",
    )(x2d, g, w_in)


def _band_attn_kernel(*refs, t, tq, nsub, has_hist):
    if has_hist:
        aq_ref, akv_ref, hk_ref, hv_ref, bias_ref, o_ref, kc, vc = refs
    else:
        aq_ref, akv_ref, bias_ref, o_ref, kc, vc = refs
    j = pl.program_id(1)
    span = A_REACH + tq

    @pl.when(j == 0)
    def _():
        if has_hist:
            kc[0:A_REACH, :] = hk_ref[...].astype(BF16)
            vc[0:A_REACH, :] = hv_ref[...].astype(BF16)
        else:
            kc[0:A_REACH, :] = jnp.zeros((A_REACH, D_A), BF16)
            vc[0:A_REACH, :] = jnp.zeros((A_REACH, D_A), BF16)
        kc[A_REACH:A_REACH + t, :] = akv_ref[:, 0:D_A].astype(BF16)
        vc[A_REACH:A_REACH + t, :] = akv_ref[:, D_A:2 * D_A].astype(BF16)

    lane = lax.broadcasted_iota(jnp.int32, (tq, LANES), 1)
    col = lax.broadcasted_iota(jnp.int32, (tq, span), 1)
    for sub in range(nsub):
        t0 = pl.multiple_of((j * nsub + sub) * tq, tq)
        q = aq_ref[sub * tq:(sub + 1) * tq, :]
        outs = []
        for p in range(A_HEADS // 2):
            qp = q[:, p * LANES:(p + 1) * LANES]
            kp = kc[pl.ds(t0, span), p * LANES:(p + 1) * LANES]
            vp = vc[pl.ds(t0, span), p * LANES:(p + 1) * LANES]
            o_pair = None
            for hh in range(2):
                own = (lane < A_HEAD_DIM) if hh == 0 else (lane >= A_HEAD_DIM)
                qm = jnp.where(own, qp.astype(F32) * A_SCALE, 0.0).astype(BF16)
                s = _dot_nt(qm, kp) + bias_ref[2 * p + hh]
                if not has_hist:
                    s = jnp.where(col >= A_REACH - t0, s, NEG_INF)
                m = jnp.max(s, axis=-1, keepdims=True)
                e = jnp.exp(s - m)
                l = jnp.sum(e, axis=-1, keepdims=True)
                o = _dot(e.astype(BF16), vp) / l
                o_pair = o if hh == 0 else jnp.where(own, o, o_pair)
            outs.append(o_pair)
        o_ref[sub * tq:(sub + 1) * tq, :] = jnp.concatenate(outs, axis=1).astype(BF16)


def _band_bias(rel_bias_l, tq):
    span = A_REACH + tq
    period = span + tq
    n_far = A_REACH - REL_CLIP + 1
    far = rel_bias_l[:, 2 * REL_CLIP:]
    n_near = min(2 * REL_CLIP, span - n_far)
    near = jnp.flip(rel_bias_l[:, :2 * REL_CLIP], axis=1)[:, :n_near]
    beyond = jnp.broadcast_to(rel_bias_l[:, :1], (A_HEADS, span - n_far - n_near))
    u = jnp.concatenate([jnp.broadcast_to(far, (A_HEADS, n_far)), near, beyond,
                         jnp.broadcast_to(far, (A_HEADS, tq))], axis=1).astype(F32)
    skew = jnp.broadcast_to(u[:, None, :], (A_HEADS, tq, period)).reshape(A_HEADS, tq * period)
    toeplitz = skew[:, :tq * (period - 1)].reshape(A_HEADS, tq, period - 1)[:, :, :span]
    i = np.arange(tq)[:, None]
    j = np.arange(span)[None, :]
    in_band = (j // CHUNK >= i // CHUNK) & (j // CHUNK <= i // CHUNK + A_LEFT_CHUNKS)
    return jnp.where(in_band[None], toeplitz, NEG_INF)


def _band_attn(aq, akv, hist, rel_bias_l):
    n, t, _ = aq.shape
    tq = min(t, 256)
    nsub = 2 if t % (2 * tq) == 0 else 1
    span = A_REACH + tq
    bias = _band_bias(rel_bias_l, tq)
    has_hist = hist is not None
    in_specs = [pl.BlockSpec((None, nsub * tq, D_A), lambda b, j: (b, j, 0)),
                pl.BlockSpec((None, t, 2 * D_A), lambda b, j: (b, 0, 0))]
    args = [aq, akv]
    if has_hist:
        in_specs += [pl.BlockSpec((None, A_REACH, D_A), lambda b, j: (b, 0, 0))] * 2
        args += list(hist)
    in_specs.append(_const_spec((A_HEADS, tq, span)))
    args.append(bias)
    return pl.pallas_call(
        functools.partial(_band_attn_kernel, t=t, tq=tq, nsub=nsub, has_hist=has_hist),
        grid=(n, t // (nsub * tq)),
        in_specs=in_specs,
        out_specs=pl.BlockSpec((None, nsub * tq, D_A), lambda b, j: (b, j, 0)),
        out_shape=jax.ShapeDtypeStruct((n, t, D_A), BF16),
        scratch_shapes=[pltpu.VMEM((A_REACH + t, D_A), BF16)] * 2,
        compiler_params=_params(("parallel", "arbitrary")),
        name="band_attn",
    )(*args)


def _swap_halves(x):
    lane = lax.broadcasted_iota(jnp.int32, x.shape, 1)
    first = (lane % B_QK_DIM) < (B_QK_DIM // 2)
    return jnp.where(first, pltpu.roll(x, LANES - B_QK_DIM // 2, axis=1), pltpu.roll(x, B_QK_DIM // 2, axis=1))


def _retention_kernel(bqk_ref, bv_ref, bg_ref, cos_ref, sin_ref, dmask_ref, qdec_ref, kdec_ref, sdec_ref,
                      s0_ref, gn_ref, ob_ref, sfin_ref, st, *, t, bc):
    for h in range(B_HEADS):
        off = (h % 2) * B_QK_DIM
        st[h] = jnp.zeros((LANES, B_V_DIM), F32)
        st[h, off:off + B_QK_DIM, :] = s0_ref[h]

    lane = lax.broadcasted_iota(jnp.int32, (bc, LANES), 1)

    def chunk(c, carry):
        r0 = pl.multiple_of(c * bc, bc)
        rows = pl.ds(r0, bc)
        cs = cos_ref[rows, :]
        sn = sin_ref[rows, :]
        for p in range(B_HEADS // 2):
            qx = bqk_ref[rows, p * LANES:(p + 1) * LANES].astype(F32)
            kx = bqk_ref[rows, D_BQK + p * LANES:D_BQK + (p + 1) * LANES].astype(F32)
            qr = qx * cs + _swap_halves(qx) * sn
            kr = (kx * cs + _swap_halves(kx) * sn) * (B_QK_DIM ** -0.5)
            for hh in range(2):
                h = 2 * p + hh
                own = (lane < B_QK_DIM) if hh == 0 else (lane >= B_QK_DIM)
                qh = jnp.where(own, qr, 0.0)
                kh = jnp.where(own, kr, 0.0)
                v = bv_ref[rows, h * B_V_DIM:(h + 1) * B_V_DIM]
                att = _dot_nt(qh.astype(BF16), kh.astype(BF16)) * dmask_ref[h]
                o = (_dot(att.astype(BF16), v)
                     + _dot((qh * qdec_ref[h]).astype(BF16), st[h].astype(BF16)))
                st[h] = st[h] * sdec_ref[h] + _dot_tn((kh * kdec_ref[h]).astype(BF16), v)
                mu = jnp.mean(o, axis=-1, keepdims=True)
                d = o - mu
                var = jnp.mean(d * d, axis=-1, keepdims=True)
                on = d * lax.rsqrt(var + EPS) * gn_ref[:, h * B_V_DIM:(h + 1) * B_V_DIM]
                g = bg_ref[rows, h * B_V_DIM:(h + 1) * B_V_DIM].astype(F32)
                ob_ref[rows, h * B_V_DIM:(h + 1) * B_V_DIM] = (g * jax.nn.sigmoid(g) * on).astype(BF16)
        return carry

    lax.fori_loop(0, t // bc, chunk, 0)
    for h in range(B_HEADS):
        off = (h % 2) * B_QK_DIM
        sfin_ref[h] = st[h, off:off + B_QK_DIM, :]


def _retention_tables(t, bc, pos0):
    log_g = jnp.log(1.0 - 2.0 ** (-(RET_GAMMA_EXP0 + jnp.arange(B_HEADS, dtype=F32))))
    i = jnp.arange(bc, dtype=F32)
    diff = i[:, None] - i[None, :]
    dmask = jnp.where(diff[None] >= 0, jnp.exp(jnp.maximum(diff, 0.0)[None] * log_g[:, None, None]), 0.0)
    qdec = jnp.exp((i[None, :] + 1.0) * log_g[:, None])
    kdec = jnp.exp((bc - 1.0 - i)[None, :] * log_g[:, None])
    sdec = jnp.exp(bc * log_g)
    qdec = jnp.broadcast_to(qdec[:, :, None], (B_HEADS, bc, LANES))
    kdec = jnp.broadcast_to(kdec[:, :, None], (B_HEADS, bc, LANES))
    sdec = jnp.broadcast_to(sdec[:, None, None], (B_HEADS, 1, B_V_DIM))
    half = B_QK_DIM // 2
    pos = pos0 + jnp.arange(t, dtype=F32)
    inv_freq = ROPE_BASE ** (-jnp.arange(half, dtype=F32) / half)
    ang = pos[:, None] * inv_freq[None, :]
    cos = jnp.tile(jnp.cos(ang), (1, LANES // half))
    sin = jnp.tile(jnp.concatenate([-jnp.sin(ang), jnp.sin(ang)], axis=1), (1, LANES // B_QK_DIM))
    return cos, sin, dmask, qdec, kdec, sdec


def _retention(bqk, bv, bg, s0, gn_g, pos0):
    n, t, _ = bqk.shape
    bc = min(t, 256)
    tables = _retention_tables(t, bc, pos0)
    seq = lambda w: pl.BlockSpec((None, t, w), lambda b: (b, 0, 0))
    state = pl.BlockSpec((None, B_HEADS, B_QK_DIM, B_V_DIM), lambda b: (b, 0, 0, 0))
    return pl.pallas_call(
        functools.partial(_retention_kernel, t=t, bc=bc),
        grid=(n,),
        in_specs=[seq(2 * D_BQK), seq(D_B), seq(D_B)] + [_const_spec(tb.shape) for tb in tables]
                 + [state, _const_spec((1, D_B))],
        out_specs=[seq(D_B), state],
        out_shape=[jax.ShapeDtypeStruct((n, t, D_B), BF16),
                   jax.ShapeDtypeStruct((n, B_HEADS, B_QK_DIM, B_V_DIM), F32)],
        scratch_shapes=[pltpu.VMEM((B_HEADS, LANES, B_V_DIM), F32)],
        compiler_params=_params(("parallel",)),
        name="retention",
    )(bqk, bv, bg, *tables, s0, gn_g)


def _conv_kernel(cab_ref, buf_ref, w_ref, b_ref, lng_ref, lnb_ref, oc_ref, nc_ref, up, *, t, tt):
    hist = CONV_WIDTH - 1
    up[0:CONV_PAD, :] = jnp.zeros((CONV_PAD, C_CHANNELS), F32)
    up[CONV_PAD - hist:CONV_PAD, :] = buf_ref[...]

    def glu(i, carry):
        rows = pl.ds(pl.multiple_of(i * tt, tt), tt)
        ca = cab_ref[rows, 0:C_CHANNELS]
        cb = cab_ref[rows, C_CHANNELS:2 * C_CHANNELS]
        up[pl.ds(pl.multiple_of(CONV_PAD + i * tt, 8), tt), :] = ca * jax.nn.sigmoid(cb)
        return carry

    lax.fori_loop(0, t // tt, glu, 0)

    def tile(i, carry):
        t0 = pl.multiple_of(i * tt, tt)
        ext = up[pl.ds(t0, tt + CONV_PAD), :]
        oc_ref[pl.ds(t0, tt), :] = _conv_tile(ext, w_ref, b_ref, lng_ref, lnb_ref, tt).astype(BF16)
        return carry

    lax.fori_loop(0, t // tt, tile, 0)
    nc_ref[...] = up[CONV_PAD + t - hist:CONV_PAD + t, :]


def _conv_module(cab, buf, conv_w, conv_b, ln_g, ln_b):
    n, t, _ = cab.shape
    tt = min(t, 128)
    hist = CONV_WIDTH - 1
    return pl.pallas_call(
        functools.partial(_conv_kernel, t=t, tt=tt),
        grid=(n,),
        in_specs=[pl.BlockSpec((None, t, 2 * C_CHANNELS), lambda b: (b, 0, 0)),
                  pl.BlockSpec((None, hist, C_CHANNELS), lambda b: (b, 0, 0)),
                  _const_spec((CONV_WIDTH, C_CHANNELS))] + [_const_spec((1, C_CHANNELS))] * 3,
        out_specs=[pl.BlockSpec((None, t, C_CHANNELS), lambda b: (b, 0, 0)),
                   pl.BlockSpec((None, hist, C_CHANNELS), lambda b: (b, 0, 0))],
        out_shape=[jax.ShapeDtypeStruct((n, t, C_CHANNELS), BF16),
                   jax.ShapeDtypeStruct((n, hist, C_CHANNELS), F32)],
        scratch_shapes=[pltpu.VMEM((CONV_PAD + t, C_CHANNELS), F32)],
        compiler_params=_params(("parallel",)),
        name="conv_module",
    )(cab, buf, conv_w, conv_b, ln_g, ln_b)


def _mem_kv_kernel(mem_ref, wk_ref, wv_ref, mk_ref, mv_ref, mkh_ref, mvh_ref):
    m = mem_ref[...].astype(BF16)
    mk_ref[...] = _dot(m, wk_ref[...])
    mv_ref[...] = _dot(m, wv_ref[...])
    n_mem = mkh_ref.shape[1]
    for src_ref, dst_ref in ((mk_ref, mkh_ref), (mv_ref, mvh_ref)):
        for b in range(mkh_ref.shape[0]):
            for h in range(X_HEADS):
                dst_ref[b, :, h, :] = src_ref[b * n_mem:(b + 1) * n_mem, h * X_HEAD_DIM:(h + 1) * X_HEAD_DIM]


def _mem_kv(mem2d, wk, wv, n_mem):
    depth = wk.shape[0]
    m = mem2d.shape[0]
    tm = 512
    nb = tm // n_mem
    wspec = pl.BlockSpec((None, D_MODEL, D_MODEL), lambda l, i: (l, 0, 0))
    ospec = pl.BlockSpec((None, tm, D_MODEL), lambda l, i: (l, i, 0))
    hspec = pl.BlockSpec((None, nb, n_mem, X_HEADS, X_HEAD_DIM), lambda l, i: (l, i, 0, 0, 0))
    heads = jax.ShapeDtypeStruct((depth, m // n_mem, n_mem, X_HEADS, X_HEAD_DIM), F32)
    return pl.pallas_call(
        _mem_kv_kernel,
        grid=(depth, m // tm),
        in_specs=[pl.BlockSpec((tm, D_MODEL), lambda l, i: (i, 0)), wspec, wspec],
        out_specs=[ospec, ospec, hspec, hspec],
        out_shape=[jax.ShapeDtypeStruct((depth, m, D_MODEL), F32)] * 2 + [heads, heads],
        compiler_params=_params(("parallel", "parallel")),
        name="mem_kv",
    )(mem2d, wk, wv)


def _route(logits):
    lane = lax.broadcasted_iota(jnp.int32, logits.shape, 1).astype(F32)
    valid = lane < N_EXPERTS
    lg = jnp.where(valid, logits, NEG_INF)
    e = jnp.exp(lg - jnp.max(lg, axis=-1, keepdims=True))
    probs = e / jnp.sum(e, axis=-1, keepdims=True)
    p1 = jnp.max(probs, axis=-1, keepdims=True)
    i1 = jnp.min(jnp.where(probs == p1, lane, float(LANES)), axis=-1, keepdims=True)
    rest = jnp.where(lane == i1, -1.0, probs)
    rest = jnp.where(valid, rest, -1.0)
    p2 = jnp.max(rest, axis=-1, keepdims=True)
    i2 = jnp.min(jnp.where(rest == p2, lane, float(LANES)), axis=-1, keepdims=True)
    tot = p1 + p2
    return lane, i1, i2, p1 / tot, p2 / tot


def _sort_tile_by_expert(logits, h_hi, tri_ref, hs_ref, route_ref, meta_ref):
    tm = logits.shape[0]
    lt = jnp.transpose(logits)[0:N_EXPERTS, :]
    sub = lax.broadcasted_iota(jnp.int32, (N_EXPERTS, tm), 0).astype(F32)
    ex = jnp.exp(lt - jnp.max(lt, axis=0, keepdims=True))
    probs = ex / jnp.sum(ex, axis=0, keepdims=True)
    p1 = jnp.max(probs, axis=0, keepdims=True)
    i1 = jnp.min(jnp.where(probs == p1, sub, float(N_EXPERTS)), axis=0, keepdims=True)
    rest = jnp.where(sub == i1, -1.0, probs)
    p2 = jnp.max(rest, axis=0, keepdims=True)
    i2 = jnp.min(jnp.where(rest == p2, sub, float(N_EXPERTS)), axis=0, keepdims=True)
    w1 = p1 / (p1 + p2)
    w2 = p2 / (p1 + p2)
    oh1 = jnp.where(sub == i1, 1.0, 0.0)
    oh2 = jnp.where(sub == i2, 1.0, 0.0)
    before1 = _dot(oh1.astype(BF16), tri_ref[...])
    before2 = _dot(oh2.astype(BF16), tri_ref[...])
    cnt1 = jnp.sum(oh1, axis=1, keepdims=True)
    cnt2 = jnp.sum(oh2, axis=1, keepdims=True)
    padded = jnp.floor((cnt1 + cnt2 + (MOE_CHUNK - 1.0)) * (1.0 / MOE_CHUNK)) * MOE_CHUNK
    run = jnp.broadcast_to(padded, (N_EXPERTS, LANES))
    sub8 = lax.broadcasted_iota(jnp.int32, (N_EXPERTS, LANES), 0)
    incl = run
    for k in (1, 2, 4):
        incl = incl + jnp.where(sub8 >= k, pltpu.roll(incl, k, axis=0), 0.0)
    start = (incl - run)[:, 0:1]
    d1 = jnp.sum(oh1 * (start + before1), axis=0, keepdims=True)
    d2 = jnp.sum(oh2 * (start + cnt1 + before2), axis=0, keepdims=True)
    row = lax.broadcasted_iota(jnp.int32, (MOE_TILE_CAP, tm), 0).astype(F32)
    perm = jnp.where(row == d1, 1.0, jnp.where(row == d2, 1.0, 0.0)).astype(BF16)
    hs_ref[...] = _dot(perm, h_hi).astype(BF16)
    info = jnp.concatenate([d1, d2, w1, w2, jnp.zeros((LANES - 4, tm), F32)], axis=0)
    route_ref[...] = jnp.transpose(info)
    lane8 = lax.broadcasted_iota(jnp.int32, (N_EXPERTS, LANES), 1)
    meta_ref[...] = jnp.where(lane8 == 0, start * (1.0 / MOE_CHUNK),
                              jnp.where(lane8 == 1, padded * (1.0 / MOE_CHUNK), 0.0))


def _post_mix_kernel(*refs, mode):
    (x_ref, oa_ref, ob_ref, oc_ref, wout_ref, gmem_ref, wq_ref, mk_ref, mv_ref, wo_ref, gffn_ref) = refs[:11]
    if mode == "dense":
        wg_ref, wu_ref, wd_ref, x3_ref = refs[11:]
    elif mode == "experts":
        router_ref, x2_ref, h_ref, route_ref = refs[11:]
    else:
        router_ref, tri_ref, x2_ref, h_ref, route_ref, meta_ref = refs[11:]
    y = (_dot(oa_ref[...], wout_ref[0:D_A, :])
         + _dot(ob_ref[...], wout_ref[D_A:D_A + D_B, :])
         + _dot(oc_ref[...], wout_ref[D_A + D_B:D_MODEL, :]))
    x1 = x_ref[...] + y
    q = _dot(_rms(x1, gmem_ref[...]).astype(BF16), wq_ref[...]).astype(BF16)
    outs = []
    for h in range(X_HEADS):
        cols = slice(h * X_HEAD_DIM, (h + 1) * X_HEAD_DIM)
        s = _dot_nt(q[:, cols], mk_ref[:, cols].astype(BF16)) * (X_HEAD_DIM ** -0.5)
        e = jnp.exp(s - jnp.max(s, axis=-1, keepdims=True))
        l = jnp.sum(e, axis=-1, keepdims=True)
        outs.append((_dot(e.astype(BF16), mv_ref[:, cols].astype(BF16)) / l).astype(BF16))
    x2 = x1 + _dot(jnp.concatenate(outs, axis=1), wo_ref[...])
    hf = _rms(x2, gffn_ref[...])
    h_hi = hf.astype(BF16)
    if mode == "dense":
        acc = x2
        for c in range(D_FF // FF_CHUNK):
            cols = slice(c * FF_CHUNK, (c + 1) * FF_CHUNK)
            g = _dot(h_hi, wg_ref[:, cols])
            u = _dot(h_hi, wu_ref[:, cols])
            acc = acc + _dot((g * jax.nn.sigmoid(g) * u).astype(BF16), wd_ref[cols, :])
        x3_ref[...] = acc
        return
    x2_ref[...] = x2
    h_lo = (hf - h_hi.astype(F32)).astype(BF16)
    logits = (_dot(h_hi, router_ref[0]) + _dot(h_lo, router_ref[0])) + _dot(h_hi, router_ref[1])
    if mode == "experts":
        h_ref[...] = h_hi
        lane, i1, i2, w1, w2 = _route(logits)
        route_ref[...] = jnp.where(lane == i1, w1, jnp.where(lane == i2, w2, 0.0))
    else:
        _sort_tile_by_expert(logits, h_hi, tri_ref, h_ref, route_ref, meta_ref)


def _post_mix(x, oa, ob, oc, w_out, g_mem, wq, mk, mv, mem_base, wo, g_ffn, tail, mode):
    n, t, _ = x.shape
    tm = min(t, 512)
    tile = lambda w: pl.BlockSpec((None, tm, w), lambda b, i: (b, i, 0))
    mem = pl.BlockSpec((None,) + mk.shape[1:], lambda b, i: (mem_base + b, 0, 0))
    sq = _const_spec((D_MODEL, D_MODEL))
    vec = _const_spec((1, D_MODEL))
    in_specs = [tile(D_MODEL), tile(D_A), tile(D_B), tile(C_CHANNELS), sq, vec, sq, mem, mem, sq, vec]
    args = [x, oa, ob, oc, w_out, g_mem, wq, mk, mv, wo, g_ffn]
    out_specs = [tile(D_MODEL)]
    out_shape = [jax.ShapeDtypeStruct((n, t, D_MODEL), F32)]
    if mode == "dense":
        in_specs += [_const_spec((D_MODEL, D_FF)), _const_spec((D_MODEL, D_FF)), _const_spec((D_FF, D_MODEL))]
        args += list(tail)
        return pl.pallas_call(
            functools.partial(_post_mix_kernel, mode=mode),
            grid=(n, t // tm),
            in_specs=in_specs,
            out_specs=out_specs[0],
            out_shape=out_shape[0],
            compiler_params=_params(("parallel", "parallel")),
            name="post_mix_ffn",
        )(*args)
    in_specs.append(_const_spec((2, D_MODEL, LANES)))
    args.append(tail)
    if mode == "grouped":
        assert tm == MOE_TILE
        in_specs.append(_const_spec((tm, tm)))
        args.append(jnp.triu(jnp.ones((tm, tm), BF16), k=1))
        out_specs.append(pl.BlockSpec((None, MOE_TILE_CAP, D_MODEL), lambda b, i: (b, i, 0)))
        out_shape.append(jax.ShapeDtypeStruct((n, t // tm * MOE_TILE_CAP, D_MODEL), BF16))
    else:
        out_specs.append(tile(D_MODEL))
        out_shape.append(jax.ShapeDtypeStruct((n, t, D_MODEL), BF16))
    out_specs.append(tile(LANES))
    out_shape.append(jax.ShapeDtypeStruct((n, t, LANES), F32))
    if mode == "grouped":
        out_specs.append(pl.BlockSpec((None, N_EXPERTS, LANES), lambda b, i: (b, i, 0)))
        out_shape.append(jax.ShapeDtypeStruct((n, t // tm * N_EXPERTS, LANES), F32))
    return pl.pallas_call(
        functools.partial(_post_mix_kernel, mode=mode),
        grid=(n, t // tm),
        in_specs=in_specs,
        out_specs=out_specs,
        out_shape=out_shape,
        compiler_params=_params(("parallel", "parallel")),
        name="post_mix",
    )(*args)


def _moe_kernel(x_ref, h_ref, comb_ref, wg_ref, wu_ref, wd_ref, gfin_ref, o_ref, acc):
    e = pl.program_id(1)

    @pl.when(e == 0)
    def _():
        acc[...] = x_ref[...]

    h = h_ref[...]
    g = _dot(h, wg_ref[...])
    u = _dot(h, wu_ref[...])
    y = _dot((g * jax.nn.sigmoid(g) * u).astype(BF16), wd_ref[...])
    comb = comb_ref[...]
    lane = lax.broadcasted_iota(jnp.int32, comb.shape, 1)
    ce = jnp.sum(jnp.where(lane == e, comb, 0.0), axis=-1, keepdims=True)
    acc[...] += ce * y

    @pl.when(e == N_EXPERTS - 1)
    def _():
        o_ref[...] = _rms(acc[...], gfin_ref[...])


def _moe(x2d, h2d, comb2d, wg, wu, wd, g_fin):
    m = x2d.shape[0]
    tm = min(m, 512)
    row = lambda w: pl.BlockSpec((tm, w), lambda i, e: (i, 0))
    return pl.pallas_call(
        _moe_kernel,
        grid=(m // tm, N_EXPERTS),
        in_specs=[row(D_MODEL), row(D_MODEL), row(LANES),
                  pl.BlockSpec((None, D_MODEL, D_FF_EXPERT), lambda i, e: (e, 0, 0)),
                  pl.BlockSpec((None, D_MODEL, D_FF_EXPERT), lambda i, e: (e, 0, 0)),
                  pl.BlockSpec((None, D_FF_EXPERT, D_MODEL), lambda i, e: (e, 0, 0)),
                  _const_spec((1, D_MODEL))],
        out_specs=row(D_MODEL),
        out_shape=jax.ShapeDtypeStruct((m, D_MODEL), F32),
        scratch_shapes=[pltpu.VMEM((tm, D_MODEL), F32)],
        compiler_params=_params(("parallel", "arbitrary")),
        name="moe_experts",
    )(x2d, h2d, comb2d, wg, wu, wd, g_fin)


def _moe_plan(meta, n_tiles):
    cpb = MOE_BLOCK // MOE_CHUNK
    tile_chunks = MOE_TILE_CAP // MOE_CHUNK
    nb = -(-(n_tiles * tile_chunks + N_EXPERTS * (cpb - 1)) // cpb)
    first = meta[:, :, 0].astype(jnp.int32)
    cnt = meta[:, :, 1].astype(jnp.int32)
    tot = jnp.sum(cnt, axis=0)
    eend = jnp.cumsum((tot + cpb - 1) // cpb * cpb)
    q = jnp.arange(nb * cpb, dtype=jnp.int32)
    e_q = jnp.minimum(jnp.sum((q[:, None] >= eend[None, :]).astype(jnp.int32), axis=1), N_EXPERTS - 1)
    oh_e = (e_q[:, None] == jnp.arange(N_EXPERTS, dtype=jnp.int32)[None, :]).astype(jnp.int32)
    pick = lambda table: jnp.sum(oh_e[:, :, None] * table.T[None, :, :], axis=1)
    j = q - jnp.sum(oh_e * (eend - (tot + cpb - 1) // cpb * cpb)[None, :], axis=1)
    valid = (j < jnp.sum(oh_e * tot[None, :], axis=1)) & (q < eend[N_EXPERTS - 1])
    incl = jnp.cumsum(cnt, axis=0)
    tile_q = jnp.minimum(jnp.sum((j[:, None] >= pick(incl)).astype(jnp.int32), axis=1), n_tiles - 1)
    oh_t = (tile_q[:, None] == jnp.arange(n_tiles, dtype=jnp.int32)[None, :]).astype(jnp.int32)
    chunk = (tile_q * tile_chunks + jnp.sum(oh_t * pick(first), axis=1)
             + j - jnp.sum(oh_t * pick(incl - cnt), axis=1))
    k = jnp.cumsum(jnp.logical_not(valid).astype(jnp.int32)) - 1
    used = jnp.sum(cnt, axis=1)
    free_incl = jnp.cumsum(tile_chunks - used)
    tile_k = jnp.minimum(jnp.sum((k[:, None] >= free_incl[None, :]).astype(jnp.int32), axis=1), n_tiles - 1)
    oh_k = (tile_k[:, None] == jnp.arange(n_tiles, dtype=jnp.int32)[None, :]).astype(jnp.int32)
    in_tile = (tile_k * tile_chunks + jnp.sum(oh_k * used[None, :], axis=1)
               + k - jnp.sum(oh_k * (free_incl - (tile_chunks - used))[None, :], axis=1))
    n_free = free_incl[n_tiles - 1]
    spare = jnp.where(k < n_free, in_tile, n_tiles * tile_chunks + k - n_free)
    src = jnp.where(valid, chunk, 0)
    dst = jnp.where(valid, chunk, spare)
    blk_expert = e_q[::cpb]
    n_active = (eend[N_EXPERTS - 1] // cpb).reshape(1)
    return blk_expert, n_active, src.reshape(nb, 1, cpb), dst.reshape(nb, 1, cpb)


def _moe_grouped_kernel(be_ref, nact_ref, src_ref, src_next_ref, dst_ref, hs_hbm, wg_ref, wu_ref, wd_ref,
                        y_hbm, xbuf, ybuf, sem_in, sem_out):
    b = pl.program_id(0)
    nb = pl.num_programs(0)
    slot = b % 2
    cpb = MOE_BLOCK // MOE_CHUNK

    def chunk_rows(i):
        return pl.ds(pl.multiple_of(i * MOE_CHUNK, MOE_CHUNK), MOE_CHUNK)

    def gather(idx_ref, s):
        for c in range(cpb):
            pltpu.make_async_copy(hs_hbm.at[chunk_rows(idx_ref[0, c]), :],
                                  xbuf.at[s, c * MOE_CHUNK:(c + 1) * MOE_CHUNK, :],
                                  sem_in.at[s]).start(priority=c % 2)

    def wait_gather(s):
        pltpu.make_async_copy(hs_hbm.at[pl.ds(0, MOE_BLOCK), :], xbuf.at[s], sem_in.at[s]).wait()

    def wait_scatter(s):
        pltpu.make_async_copy(ybuf.at[s], y_hbm.at[pl.ds(0, MOE_BLOCK), :], sem_out.at[s]).wait()

    @pl.when(b == 0)
    def _():
        gather(src_ref, 0)

    @pl.when(b + 1 < nb)
    def _():
        gather(src_next_ref, 1 - slot)

    wait_gather(slot)

    @pl.when(b >= 2)
    def _():
        wait_scatter(slot)

    @pl.when(b < nact_ref[0])
    def _():
        x = xbuf[slot]
        g = _dot(x, wg_ref[...])
        u = _dot(x, wu_ref[...])
        ybuf[slot] = _dot((g * jax.nn.sigmoid(g) * u).astype(BF16), wd_ref[...]).astype(BF16)

    @pl.when(b >= nact_ref[0])
    def _():
        ybuf[slot] = jnp.zeros((MOE_BLOCK, D_MODEL), BF16)

    for c in range(cpb):
        pltpu.make_async_copy(ybuf.at[slot, c * MOE_CHUNK:(c + 1) * MOE_CHUNK, :],
                              y_hbm.at[chunk_rows(dst_ref[0, c]), :],
                              sem_out.at[slot]).start(priority=c % 2)

    @pl.when(b == nb - 1)
    def _():
        wait_scatter(slot)
        wait_scatter(1 - slot)


def _moe_grouped(hs, blk_expert, n_active, src, dst, wg, wu, wd):
    nb, _, cpb = src.shape
    assert nb >= 2
    idx_spec = lambda f: pl.BlockSpec((None, 1, cpb), f, memory_space=pltpu.SMEM)
    wspec = lambda shape: pl.BlockSpec((None,) + shape, lambda b, be, na: (be[b], 0, 0))
    return pl.pallas_call(
        _moe_grouped_kernel,
        grid_spec=pltpu.PrefetchScalarGridSpec(
            num_scalar_prefetch=2,
            grid=(nb,),
            in_specs=[idx_spec(lambda b, be, na: (b, 0, 0)),
                      idx_spec(lambda b, be, na: (jnp.minimum(b + 1, nb - 1), 0, 0)),
                      idx_spec(lambda b, be, na: (b, 0, 0)),
                      pl.BlockSpec(memory_space=pl.ANY),
                      wspec((D_MODEL, D_FF_EXPERT)), wspec((D_MODEL, D_FF_EXPERT)),
                      wspec((D_FF_EXPERT, D_MODEL))],
            out_specs=pl.BlockSpec(memory_space=pl.ANY),
            scratch_shapes=[pltpu.VMEM((2, MOE_BLOCK, D_MODEL), BF16), pltpu.VMEM((2, MOE_BLOCK, D_MODEL), BF16),
                            pltpu.SemaphoreType.DMA((2,)), pltpu.SemaphoreType.DMA((2,))]),
        out_shape=jax.ShapeDtypeStruct((nb * MOE_BLOCK, D_MODEL), BF16),
        compiler_params=_params(("arbitrary",)),
        name="moe_grouped",
    )(blk_expert, n_active, src, src, dst, hs, wg, wu, wd)


def _moe_combine_kernel(x_ref, route_ref, y_ref, gfin_ref, o_ref):
    tm = x_ref.shape[0]
    route = route_ref[...]
    d1, d2, w1, w2 = route[:, 0:1], route[:, 1:2], route[:, 2:3], route[:, 3:4]
    col = lax.broadcasted_iota(jnp.int32, (tm, MOE_TILE_CAP), 1).astype(F32)
    wm = jnp.where(col == d1, w1, jnp.where(col == d2, w2, 0.0)).astype(BF16)
    acc = x_ref[...] + _dot(wm, y_ref[...])
    o_ref[...] = _rms(acc, gfin_ref[...])


def _moe_combine(x2d, route2d, ys, g_fin):
    m = x2d.shape[0]
    tm = MOE_TILE
    return pl.pallas_call(
        _moe_combine_kernel,
        grid=(m // tm,),
        in_specs=[pl.BlockSpec((tm, D_MODEL), lambda i: (i, 0)),
                  pl.BlockSpec((tm, LANES), lambda i: (i, 0)),
                  pl.BlockSpec((MOE_TILE_CAP, D_MODEL), lambda i: (i, 0)),
                  _const_spec((1, D_MODEL))],
        out_specs=pl.BlockSpec((tm, D_MODEL), lambda i: (i, 0)),
        out_shape=jax.ShapeDtypeStruct((m, D_MODEL), F32),
        compiler_params=_params(("parallel",)),
        name="moe_combine",
    )(x2d, route2d, ys, g_fin)


def _layer(x, pos0, attn_hist, ret_state, conv_state, mk, mv, mem_base, w):
    n, t, _ = x.shape
    conv_w = (w["conv_w"], w["conv_b"], w["conv_ln_g"], w["conv_ln_b"])
    if conv_state is None:
        aq, akv, bqk, bv, bg, oc, new_conv = _in_proj_conv(x, w["norm_mix_g"], w["w_in"], *conv_w)
    else:
        aq, akv, bqk, bv, bg, cab = _in_proj(x.reshape(n * t, D_MODEL), w["norm_mix_g"], w["w_in"])
        r3 = lambda a: a.reshape(n, t, a.shape[-1])
        aq, akv, bqk, bv, bg, cab = map(r3, (aq, akv, bqk, bv, bg, cab))
        oc, new_conv = _conv_module(cab, conv_state, *conv_w)
    oa = _band_attn(aq, akv, attn_hist, w["rel_bias"])
    ob, new_s = _retention(bqk, bv, bg, ret_state, w["ret_gn_g"], pos0)
    m = n * t
    if "router" not in w:
        mode = "dense"
    elif m >= N_EXPERTS * MOE_BLOCK and t % MOE_TILE == 0:
        mode = "grouped"
    else:
        mode = "experts"
    tail = (w["ffn_g"], w["ffn_u"], w["ffn_d"]) if mode == "dense" else w["router"]
    post = _post_mix(x, oa, ob, oc, w["w_out"], w["norm_mem_g"], w["wx_q"], mk, mv, mem_base, w["wx_o"],
                     w["norm_ffn_g"], tail, mode)
    if mode == "dense":
        x3 = post
    elif mode == "experts":
        x2 = post[0].reshape(m, D_MODEL)
        x3 = _moe(x2, post[1].reshape(m, D_MODEL), post[2].reshape(m, LANES),
                  w["moe_g"], w["moe_u"], w["moe_d"], w["final_g"])
    else:
        x2 = post[0].reshape(m, D_MODEL)
        n_tiles = m // MOE_TILE
        blk_expert, n_active, src, dst = _moe_plan(post[3].reshape(n_tiles, N_EXPERTS, LANES), n_tiles)
        ys = _moe_grouped(post[1].reshape(n_tiles * MOE_TILE_CAP, D_MODEL), blk_expert, n_active, src, dst,
                          w["moe_g"], w["moe_u"], w["moe_d"])
        x3 = _moe_combine(x2, post[2].reshape(m, LANES), ys, w["final_g"])
    keep = min(A_REACH, t)
    new_kv = akv[:, t - keep:, :]
    new_k = new_kv[..., :D_A].reshape(n, keep, A_HEADS, A_HEAD_DIM)
    new_v = new_kv[..., D_A:].reshape(n, keep, A_HEADS, A_HEAD_DIM)
    return x3.reshape(n, t, D_MODEL), new_k, new_v, new_s, new_conv


def kernel(x_prompt, x_sample, cache_attn_k, cache_attn_v, state_ret, state_conv, cache_mem_k, cache_mem_v,
           mem_prompt, norm_mix_g, w_in, rel_bias, ret_gn_g, conv_w, conv_b, conv_ln_g, conv_ln_b, w_out,
           norm_mem_g, wx_q, wx_k, wx_v, wx_o, norm_ffn_g, ffn_w_gate, ffn_w_up, ffn_w_down,
           router_w, moe_w_gate, moe_w_up, moe_w_down, final_norm_g):
    depth = w_in.shape[0]
    assert depth == 2, "layer 0 dense FFN, layer 1 experts + closing norm"
    n_p, _, _ = x_prompt.shape
    n_s, t_s, _ = x_sample.shape
    n_mem = mem_prompt.shape[1]
    row = lambda a: a.reshape(1, -1).astype(F32)
    xp, xs = x_prompt, x_sample
    mem2d = mem_prompt.reshape(n_p * n_mem, D_MODEL)
    outs_p = [[] for _ in range(4)]
    outs_s = [[] for _ in range(4)]
    mk_all, mv_all, mk_heads, mv_heads = _mem_kv(mem2d, wx_k.astype(BF16), wx_v.astype(BF16), n_mem)
    mk_p = mk_all.reshape(depth * n_p, n_mem, D_MODEL)
    mv_p = mv_all.reshape(depth * n_p, n_mem, D_MODEL)
    mk_s = cache_mem_k.reshape(depth * n_s, n_mem, D_MODEL)
    mv_s = cache_mem_v.reshape(depth * n_s, n_mem, D_MODEL)
    for l in range(depth):
        w = dict(norm_mix_g=row(norm_mix_g[l]), w_in=w_in[l].astype(BF16), rel_bias=rel_bias[l],
                 ret_gn_g=row(ret_gn_g[l]), conv_w=conv_w[l].astype(F32), conv_b=row(conv_b[l]),
                 conv_ln_g=row(conv_ln_g[l]), conv_ln_b=row(conv_ln_b[l]), w_out=w_out[l].astype(BF16),
                 norm_mem_g=row(norm_mem_g[l]), wx_q=wx_q[l].astype(BF16), wx_o=wx_o[l].astype(BF16),
                 norm_ffn_g=row(norm_ffn_g[l]))
        if l % 2 == 0:
            i = l // 2
            w.update(ffn_g=ffn_w_gate[i].astype(BF16), ffn_u=ffn_w_up[i].astype(BF16),
                     ffn_d=ffn_w_down[i].astype(BF16))
        else:
            i = l // 2
            r = jnp.pad(router_w[i].astype(F32), ((0, 0), (0, LANES - N_EXPERTS)))
            r_hi = r.astype(BF16)
            r_lo = (r - r_hi.astype(F32)).astype(BF16)
            w.update(router=jnp.stack([r_hi, r_lo]),
                     moe_g=moe_w_gate[i].astype(BF16), moe_u=moe_w_up[i].astype(BF16),
                     moe_d=moe_w_down[i].astype(BF16),
                     final_g=row(final_norm_g) if l == depth - 1 else None)
        xp, k_new, v_new, s_new, c_new = _layer(
            xp, 0.0, None, jnp.zeros((n_p, B_HEADS, B_QK_DIM, B_V_DIM), F32), None, mk_p, mv_p, l * n_p, w)
        for lst, a in zip(outs_p, (k_new, v_new, s_new, c_new)):
            lst.append(a)
        hist = (cache_attn_k[l].reshape(n_s, A_REACH, D_A), cache_attn_v[l].reshape(n_s, A_REACH, D_A))
        xs, k_new, v_new, s_new, c_new = _layer(
            xs, float(PAST_LEN), hist, state_ret[l], state_conv[l], mk_s, mv_s, l * n_s, w)
        for lst, a in zip(outs_s, (k_new, v_new, s_new, c_new)):
            lst.append(a)
    st = lambda lst: jnp.stack(lst)
    return (xp, xs, st(outs_p[0]), st(outs_p[1]), st(outs_p[2]), st(outs_p[3]), mk_heads, mv_heads,
            st(outs_s[0]), st(outs_s[1]), st(outs_s[2]), st(outs_s[3]))
```

```python
import functools

import numpy as np
import jax
import jax.numpy as jnp
from jax import lax
from jax.experimental import pallas as pl
from jax.experimental.pallas import tpu as pltpu

F32 = jnp.float32
BF16 = jnp.bfloat16

D_MODEL = 1024
PAST_LEN = 2048
CHUNK = 64
EPS = 1e-6
NEG_INF = -1e30
LANES = 128

A_HEADS = 4
A_HEAD_DIM = 64
D_A = A_HEADS * A_HEAD_DIM
A_LEFT_CHUNKS = 8
A_REACH = A_LEFT_CHUNKS * CHUNK
REL_CLIP = 128
A_SCALE = A_HEAD_DIM ** -0.5

B_HEADS = 4
B_QK_DIM = 64
B_V_DIM = 128
D_BQK = B_HEADS * B_QK_DIM
D_B = B_HEADS * B_V_DIM
RET_GAMMA_EXP0 = 5.0
ROPE_BASE = 10000.0

C_CHANNELS = 256
CONV_WIDTH = 31
CONV_PAD = 32

X_HEADS = 4
X_HEAD_DIM = D_MODEL // X_HEADS

D_FF = 11 * D_MODEL // 4
FF_CHUNK = 256
N_EXPERTS = 8
TOP_K = 2
D_FF_EXPERT = D_FF // 2
MOE_TILE = 512
MOE_CHUNK = 16
MOE_TILE_CAP = 1152
MOE_BLOCK = 512

COL_QA, COL_KVA, COL_QKB, COL_VB, COL_GB, COL_CAB, D_IN = 0, 256, 768, 1280, 1792, 2304, 2816

VMEM_LIMIT = 56 * 1024 * 1024


def _params(sem):
    return pltpu.CompilerParams(dimension_semantics=sem, vmem_limit_bytes=VMEM_LIMIT)


def _rms(x, g):
    return x * lax.rsqrt(jnp.mean(x * x, axis=-1, keepdims=True) + EPS) * g


def _dot(a, b):
    return jnp.dot(a, b, preferred_element_type=F32)


def _dot_nt(a, b):
    return lax.dot_general(a, b, (((1,), (1,)), ((), ())), preferred_element_type=F32)


def _dot_tn(a, b):
    return lax.dot_general(a, b, (((0,), (0,)), ((), ())), preferred_element_type=F32)


def _const_spec(shape):
    return pl.BlockSpec(shape, lambda *_: (0,) * len(shape), pipeline_mode=pl.Buffered(1))


def _in_proj_kernel(x_ref, g_ref, w_ref, aq_ref, akv_ref, bqk_ref, bv_ref, bg_ref, cab_ref):
    h = _rms(x_ref[...], g_ref[...]).astype(BF16)
    aq_ref[...] = _dot(h, w_ref[:, COL_QA:COL_KVA]).astype(BF16)
    akv_ref[...] = _dot(h, w_ref[:, COL_KVA:COL_QKB])
    bqk_ref[...] = _dot(h, w_ref[:, COL_QKB:COL_VB]).astype(BF16)
    bv_ref[...] = _dot(h, w_ref[:, COL_VB:COL_GB]).astype(BF16)
    bg_ref[...] = _dot(h, w_ref[:, COL_GB:COL_CAB]).astype(BF16)
    cab_ref[...] = _dot(h, w_ref[:, COL_CAB:D_IN])


def _conv_tile(ext, w_ref, b_ref, lng_ref, lnb_ref, tt):
    hist = CONV_WIDTH - 1
    acc = jnp.zeros((tt, C_CHANNELS), F32)
    for b in range(8):
        rb = ext if b == 0 else pltpu.roll(ext, b, axis=0)
        for a in range(CONV_PAD // 8):
            d = 8 * a + b
            if d > hist:
                continue
            k = hist - d
            acc = acc + w_ref[k:k + 1, :] * rb[CONV_PAD - 8 * a:CONV_PAD - 8 * a + tt, :]
    y = acc + b_ref[...]
    mu = jnp.mean(y, axis=-1, keepdims=True)
    d0 = y - mu
    var = jnp.mean(d0 * d0, axis=-1, keepdims=True)
    yn = d0 * lax.rsqrt(var + EPS) * lng_ref[...] + lnb_ref[...]
    return yn * jax.nn.sigmoid(yn)


def _in_proj_conv_kernel(x_ref, g_ref, w_ref, cw_ref, cb_ref, lng_ref, lnb_ref,
                         aq_ref, akv_ref, bqk_ref, bv_ref, bg_ref, oc_ref, nc_ref, up, *, tm, tt):
    hist = CONV_WIDTH - 1

    @pl.when(pl.program_id(1) == 0)
    def _():
        up[tm:tm + CONV_PAD, :] = jnp.zeros((CONV_PAD, C_CHANNELS), F32)

    h = _rms(x_ref[...], g_ref[...]).astype(BF16)
    cab = _dot(h, w_ref[:, COL_CAB:D_IN])
    up[0:CONV_PAD, :] = up[tm:tm + CONV_PAD, :]
    up[CONV_PAD:CONV_PAD + tm, :] = cab[:, 0:C_CHANNELS] * jax.nn.sigmoid(cab[:, C_CHANNELS:2 * C_CHANNELS])
    nc_ref[...] = up[CONV_PAD + tm - hist:CONV_PAD + tm, :]
    cols = ((aq_ref, COL_QA, COL_KVA), (akv_ref, COL_KVA, COL_QKB), (bqk_ref, COL_QKB, COL_VB),
            (bv_ref, COL_VB, COL_GB), (bg_ref, COL_GB, COL_CAB))
    for i in range(tm // tt):
        ext = up[i * tt:i * tt + tt + CONV_PAD, :]
        oc_ref[i * tt:(i + 1) * tt, :] = _conv_tile(ext, cw_ref, cb_ref, lng_ref, lnb_ref, tt).astype(BF16)
        for o_ref, lo, hi in cols[i::tm // tt]:
            o_ref[...] = _dot(h, w_ref[:, lo:hi]).astype(o_ref.dtype)


def _in_proj_conv(x, g, w_in, conv_w, conv_b, ln_g, ln_b):
    n, t, _ = x.shape
    tm = min(t, 512)
    tt = min(tm, 128)
    hist = CONV_WIDTH - 1
    widths = ((D_A, BF16), (2 * D_A, F32), (2 * D_BQK, BF16), (D_B, BF16), (D_B, BF16), (C_CHANNELS, BF16))
    tile = lambda w: pl.BlockSpec((None, tm, w), lambda b, j: (b, j, 0))
    return pl.pallas_call(
        functools.partial(_in_proj_conv_kernel, tm=tm, tt=tt),
        grid=(n, t // tm),
        in_specs=[tile(D_MODEL), _const_spec((1, D_MODEL)), _const_spec((D_MODEL, D_IN)),
                  _const_spec((CONV_WIDTH, C_CHANNELS))] + [_const_spec((1, C_CHANNELS))] * 3,
        out_specs=[tile(w) for w, _ in widths] + [pl.BlockSpec((None, hist, C_CHANNELS), lambda b, j: (b, 0, 0))],
        out_shape=[jax.ShapeDtypeStruct((n, t, w), dt) for w, dt in widths]
                  + [jax.ShapeDtypeStruct((n, hist, C_CHANNELS), F32)],
        scratch_shapes=[pltpu.VMEM((CONV_PAD + tm, C_CHANNELS), F32)],
        compiler_params=_params(("parallel", "arbitrary")),
        name="in_proj_conv",
    )(x, g, w_in, conv_w, conv_b, ln_g, ln_b)


def _in_proj(x2d, g, w_in):
    m = x2d.shape[0]
    tm = min(m, 512)
    widths = ((256, BF16), (512, F32), (512, BF16), (512, BF16), (512, BF16), (512, F32))
    return pl.pallas_call(
        _in_proj_kernel,
        grid=(m // tm,),
        in_specs=[pl.BlockSpec((tm, D_MODEL), lambda i: (i, 0)),
                  _const_spec((1, D_MODEL)),
                  _const_spec((D_MODEL, D_IN))],
        out_specs=[pl.BlockSpec((tm, w), lambda i: (i, 0)) for w, _ in widths],
        out_shape=[jax.ShapeDtypeStruct((m, w), dt) for w, dt in widths],
        compiler_params=_params(("parallel",)),
        name="in_proj",
    )(x2d, g, w_in)


def _band_attn_kernel(*refs, t, tq, nsub, has_hist):
    if has_hist:
        aq_ref, akv_ref, hk_ref, hv_ref, bias_ref, o_ref, kc, vc = refs
    else:
        aq_ref, akv_ref, bias_ref, o_ref, kc, vc = refs
    j = pl.program_id(1)
    span = A_REACH + tq

    @pl.when(j == 0)
    def _():
        if has_hist:
            kc[0:A_REACH, :] = hk_ref[...].astype(BF16)
            vc[0:A_REACH, :] = hv_ref[...].astype(BF16)
        else:
            kc[0:A_REACH, :] = jnp.zeros((A_REACH, D_A), BF16)
            vc[0:A_REACH, :] = jnp.zeros((A_REACH, D_A), BF16)
        kc[A_REACH:A_REACH + t, :] = akv_ref[:, 0:D_A].astype(BF16)
        vc[A_REACH:A_REACH + t, :] = akv_ref[:, D_A:2 * D_A].astype(BF16)

    lane = lax.broadcasted_iota(jnp.int32, (tq, LANES), 1)
    col = lax.broadcasted_iota(jnp.int32, (tq, span), 1)
    for sub in range(nsub):
        t0 = pl.multiple_of((j * nsub + sub) * tq, tq)
        q = aq_ref[sub * tq:(sub + 1) * tq, :]
        outs = []
        for p in range(A_HEADS // 2):
            qp = q[:, p * LANES:(p + 1) * LANES]
            kp = kc[pl.ds(t0, span), p * LANES:(p + 1) * LANES]
            vp = vc[pl.ds(t0, span), p * LANES:(p + 1) * LANES]
            o_pair = None
            for hh in range(2):
                own = (lane < A_HEAD_DIM) if hh == 0 else (lane >= A_HEAD_DIM)
                qm = jnp.where(own, qp.astype(F32) * A_SCALE, 0.0).astype(BF16)
                s = _dot_nt(qm, kp) + bias_ref[2 * p + hh]
                if not has_hist:
                    s = jnp.where(col >= A_REACH - t0, s, NEG_INF)
                m = jnp.max(s, axis=-1, keepdims=True)
                e = jnp.exp(s - m)
                l = jnp.sum(e, axis=-1, keepdims=True)
                o = _dot(e.astype(BF16), vp) / l
                o_pair = o if hh == 0 else jnp.where(own, o, o_pair)
            outs.append(o_pair)
        o_ref[sub * tq:(sub + 1) * tq, :] = jnp.concatenate(outs, axis=1).astype(BF16)


def _band_bias(rel_bias_l, tq):
    span = A_REACH + tq
    period = span + tq
    n_far = A_REACH - REL_CLIP + 1
    far = rel_bias_l[:, 2 * REL_CLIP:]
    n_near = min(2 * REL_CLIP, span - n_far)
    near = jnp.flip(rel_bias_l[:, :2 * REL_CLIP], axis=1)[:, :n_near]
    beyond = jnp.broadcast_to(rel_bias_l[:, :1], (A_HEADS, span - n_far - n_near))
    u = jnp.concatenate([jnp.broadcast_to(far, (A_HEADS, n_far)), near, beyond,
                         jnp.broadcast_to(far, (A_HEADS, tq))], axis=1).astype(F32)
    skew = jnp.broadcast_to(u[:, None, :], (A_HEADS, tq, period)).reshape(A_HEADS, tq * period)
    toeplitz = skew[:, :tq * (period - 1)].reshape(A_HEADS, tq, period - 1)[:, :, :span]
    i = np.arange(tq)[:, None]
    j = np.arange(span)[None, :]
    in_band = (j // CHUNK >= i // CHUNK) & (j // CHUNK <= i // CHUNK + A_LEFT_CHUNKS)
    return jnp.where(in_band[None], toeplitz, NEG_INF)


def _band_attn(aq, akv, hist, rel_bias_l):
    n, t, _ = aq.shape
    tq = min(t, 256)
    nsub = 2 if t % (2 * tq) == 0 else 1
    span = A_REACH + tq
    bias = _band_bias(rel_bias_l, tq)
    has_hist = hist is not None
    in_specs = [pl.BlockSpec((None, nsub * tq, D_A), lambda b, j: (b, j, 0)),
                pl.BlockSpec((None, t, 2 * D_A), lambda b, j: (b, 0, 0))]
    args = [aq, akv]
    if has_hist:
        in_specs += [pl.BlockSpec((None, A_REACH, D_A), lambda b, j: (b, 0, 0))] * 2
        args += list(hist)
    in_specs.append(_const_spec((A_HEADS, tq, span)))
    args.append(bias)
    return pl.pallas_call(
        functools.partial(_band_attn_kernel, t=t, tq=tq, nsub=nsub, has_hist=has_hist),
        grid=(n, t // (nsub * tq)),
        in_specs=in_specs,
        out_specs=pl.BlockSpec((None, nsub * tq, D_A), lambda b, j: (b, j, 0)),
        out_shape=jax.ShapeDtypeStruct((n, t, D_A), BF16),
        scratch_shapes=[pltpu.VMEM((A_REACH + t, D_A), BF16)] * 2,
        compiler_params=_params(("parallel", "arbitrary")),
        name="band_attn",
    )(*args)


def _swap_halves(x):
    lane = lax.broadcasted_iota(jnp.int32, x.shape, 1)
    first = (lane % B_QK_DIM) < (B_QK_DIM // 2)
    return jnp.where(first, pltpu.roll(x, LANES - B_QK_DIM // 2, axis=1), pltpu.roll(x, B_QK_DIM // 2, axis=1))


def _retention_kernel(bqk_ref, bv_ref, bg_ref, cos_ref, sin_ref, dmask_ref, qdec_ref, kdec_ref, sdec_ref,
                      s0_ref, gn_ref, ob_ref, sfin_ref, st, *, t, bc):
    for h in range(B_HEADS):
        off = (h % 2) * B_QK_DIM
        st[h] = jnp.zeros((LANES, B_V_DIM), F32)
        st[h, off:off + B_QK_DIM, :] = s0_ref[h]

    lane = lax.broadcasted_iota(jnp.int32, (bc, LANES), 1)

    def chunk(c, carry):
        r0 = pl.multiple_of(c * bc, bc)
        rows = pl.ds(r0, bc)
        cs = cos_ref[rows, :]
        sn = sin_ref[rows, :]
        for p in range(B_HEADS // 2):
            qx = bqk_ref[rows, p * LANES:(p + 1) * LANES].astype(F32)
            kx = bqk_ref[rows, D_BQK + p * LANES:D_BQK + (p + 1) * LANES].astype(F32)
            qr = qx * cs + _swap_halves(qx) * sn
            kr = (kx * cs + _swap_halves(kx) * sn) * (B_QK_DIM ** -0.5)
            for hh in range(2):
                h = 2 * p + hh
                own = (lane < B_QK_DIM) if hh == 0 else (lane >= B_QK_DIM)
                qh = jnp.where(own, qr, 0.0)
                kh = jnp.where(own, kr, 0.0)
                v = bv_ref[rows, h * B_V_DIM:(h + 1) * B_V_DIM]
                att = _dot_nt(qh.astype(BF16), kh.astype(BF16)) * dmask_ref[h]
                o = (_dot(att.astype(BF16), v)
                     + _dot((qh * qdec_ref[h]).astype(BF16), st[h].astype(BF16)))
                st[h] = st[h] * sdec_ref[h] + _dot_tn((kh * kdec_ref[h]).astype(BF16), v)
                mu = jnp.mean(o, axis=-1, keepdims=True)
                d = o - mu
                var = jnp.mean(d * d, axis=-1, keepdims=True)
                on = d * lax.rsqrt(var + EPS) * gn_ref[:, h * B_V_DIM:(h + 1) * B_V_DIM]
                g = bg_ref[rows, h * B_V_DIM:(h + 1) * B_V_DIM].astype(F32)
                ob_ref[rows, h * B_V_DIM:(h + 1) * B_V_DIM] = (g * jax.nn.sigmoid(g) * on).astype(BF16)
        return carry

    lax.fori_loop(0, t // bc, chunk, 0)
    for h in range(B_HEADS):
        off = (h % 2) * B_QK_DIM
        sfin_ref[h] = st[h, off:off + B_QK_DIM, :]


def _retention_tables(t, bc, pos0):
    log_g = jnp.log(1.0 - 2.0 ** (-(RET_GAMMA_EXP0 + jnp.arange(B_HEADS, dtype=F32))))
    i = jnp.arange(bc, dtype=F32)
    diff = i[:, None] - i[None, :]
    dmask = jnp.where(diff[None] >= 0, jnp.exp(jnp.maximum(diff, 0.0)[None] * log_g[:, None, None]), 0.0)
    qdec = jnp.exp((i[None, :] + 1.0) * log_g[:, None])
    kdec = jnp.exp((bc - 1.0 - i)[None, :] * log_g[:, None])
    sdec = jnp.exp(bc * log_g)
    qdec = jnp.broadcast_to(qdec[:, :, None], (B_HEADS, bc, LANES))
    kdec = jnp.broadcast_to(kdec[:, :, None], (B_HEADS, bc, LANES))
    sdec = jnp.broadcast_to(sdec[:, None, None], (B_HEADS, 1, B_V_DIM))
    half = B_QK_DIM // 2
    pos = pos0 + jnp.arange(t, dtype=F32)
    inv_freq = ROPE_BASE ** (-jnp.arange(half, dtype=F32) / half)
    ang = pos[:, None] * inv_freq[None, :]
    cos = jnp.tile(jnp.cos(ang), (1, LANES // half))
    sin = jnp.tile(jnp.concatenate([-jnp.sin(ang), jnp.sin(ang)], axis=1), (1, LANES // B_QK_DIM))
    return cos, sin, dmask, qdec, kdec, sdec


def _retention(bqk, bv, bg, s0, gn_g, pos0):
    n, t, _ = bqk.shape
    bc = min(t, 256)
    tables = _retention_tables(t, bc, pos0)
    seq = lambda w: pl.BlockSpec((None, t, w), lambda b: (b, 0, 0))
    state = pl.BlockSpec((None, B_HEADS, B_QK_DIM, B_V_DIM), lambda b: (b, 0, 0, 0))
    return pl.pallas_call(
        functools.partial(_retention_kernel, t=t, bc=bc),
        grid=(n,),
        in_specs=[seq(2 * D_BQK), seq(D_B), seq(D_B)] + [_const_spec(tb.shape) for tb in tables]
                 + [state, _const_spec((1, D_B))],
        out_specs=[seq(D_B), state],
        out_shape=[jax.ShapeDtypeStruct((n, t, D_B), BF16),
                   jax.ShapeDtypeStruct((n, B_HEADS, B_QK_DIM, B_V_DIM), F32)],
        scratch_shapes=[pltpu.VMEM((B_HEADS, LANES, B_V_DIM), F32)],
        compiler_params=_params(("parallel",)),
        name="retention",
    )(bqk, bv, bg, *tables, s0, gn_g)


def _conv_kernel(cab_ref, buf_ref, w_ref, b_ref, lng_ref, lnb_ref, oc_ref, nc_ref, up, *, t, tt):
    hist = CONV_WIDTH - 1
    up[0:CONV_PAD, :] = jnp.zeros((CONV_PAD, C_CHANNELS), F32)
    up[CONV_PAD - hist:CONV_PAD, :] = buf_ref[...]

    def glu(i, carry):
        rows = pl.ds(pl.multiple_of(i * tt, tt), tt)
        ca = cab_ref[rows, 0:C_CHANNELS]
        cb = cab_ref[rows, C_CHANNELS:2 * C_CHANNELS]
        up[pl.ds(pl.multiple_of(CONV_PAD + i * tt, 8), tt), :] = ca * jax.nn.sigmoid(cb)
        return carry

    lax.fori_loop(0, t // tt, glu, 0)

    def tile(i, carry):
        t0 = pl.multiple_of(i * tt, tt)
        ext = up[pl.ds(t0, tt + CONV_PAD), :]
        oc_ref[pl.ds(t0, tt), :] = _conv_tile(ext, w_ref, b_ref, lng_ref, lnb_ref, tt).astype(BF16)
        return carry

    lax.fori_loop(0, t // tt, tile, 0)
    nc_ref[...] = up[CONV_PAD + t - hist:CONV_PAD + t, :]


def _conv_module(cab, buf, conv_w, conv_b, ln_g, ln_b):
    n, t, _ = cab.shape
    tt = min(t, 128)
    hist = CONV_WIDTH - 1
    return pl.pallas_call(
        functools.partial(_conv_kernel, t=t, tt=tt),
        grid=(n,),
        in_specs=[pl.BlockSpec((None, t, 2 * C_CHANNELS), lambda b: (b, 0, 0)),
                  pl.BlockSpec((None, hist, C_CHANNELS), lambda b: (b, 0, 0)),
                  _const_spec((CONV_WIDTH, C_CHANNELS))] + [_const_spec((1, C_CHANNELS))] * 3,
        out_specs=[pl.BlockSpec((None, t, C_CHANNELS), lambda b: (b, 0, 0)),
                   pl.BlockSpec((None, hist, C_CHANNELS), lambda b: (b, 0, 0))],
        out_shape=[jax.ShapeDtypeStruct((n, t, C_CHANNELS), BF16),
                   jax.ShapeDtypeStruct((n, hist, C_CHANNELS), F32)],
        scratch_shapes=[pltpu.VMEM((CONV_PAD + t, C_CHANNELS), F32)],
        compiler_params=_params(("parallel",)),
        name="conv_module",
    )(cab, buf, conv_w, conv_b, ln_g, ln_b)


def _mem_kv_kernel(mem_ref, wk_ref, wv_ref, mk_ref, mv_ref, mkh_ref, mvh_ref):
    m = mem_ref[...].astype(BF16)
    n_mem = mkh_ref.shape[1]
    for w_ref, rows_ref, heads_ref in ((wk_ref, mk_ref, mkh_ref), (wv_ref, mv_ref, mvh_ref)):
        res = _dot(m, w_ref[...])
        rows_ref[...] = res.astype(BF16)
        for b in range(heads_ref.shape[0]):
            for h in range(X_HEADS):
                heads_ref[b, :, h, :] = res[b * n_mem:(b + 1) * n_mem, h * X_HEAD_DIM:(h + 1) * X_HEAD_DIM]


def _mem_kv(mem2d, wk, wv, n_mem):
    depth = wk.shape[0]
    m = mem2d.shape[0]
    tm = 512
    nb = tm // n_mem
    wspec = pl.BlockSpec((None, D_MODEL, D_MODEL), lambda l, i: (l, 0, 0))
    ospec = pl.BlockSpec((None, tm, D_MODEL), lambda l, i: (l, i, 0))
    hspec = pl.BlockSpec((None, nb, n_mem, X_HEADS, X_HEAD_DIM), lambda l, i: (l, i, 0, 0, 0))
    heads = jax.ShapeDtypeStruct((depth, m // n_mem, n_mem, X_HEADS, X_HEAD_DIM), F32)
    return pl.pallas_call(
        _mem_kv_kernel,
        grid=(depth, m // tm),
        in_specs=[pl.BlockSpec((tm, D_MODEL), lambda l, i: (i, 0)), wspec, wspec],
        out_specs=[ospec, ospec, hspec, hspec],
        out_shape=[jax.ShapeDtypeStruct((depth, m, D_MODEL), BF16)] * 2 + [heads, heads],
        compiler_params=_params(("parallel", "parallel")),
        name="mem_kv",
    )(mem2d, wk, wv)


def _route(logits):
    lane = lax.broadcasted_iota(jnp.int32, logits.shape, 1).astype(F32)
    valid = lane < N_EXPERTS
    lg = jnp.where(valid, logits, NEG_INF)
    e = jnp.exp(lg - jnp.max(lg, axis=-1, keepdims=True))
    probs = e / jnp.sum(e, axis=-1, keepdims=True)
    p1 = jnp.max(probs, axis=-1, keepdims=True)
    i1 = jnp.min(jnp.where(probs == p1, lane, float(LANES)), axis=-1, keepdims=True)
    rest = jnp.where(lane == i1, -1.0, probs)
    rest = jnp.where(valid, rest, -1.0)
    p2 = jnp.max(rest, axis=-1, keepdims=True)
    i2 = jnp.min(jnp.where(rest == p2, lane, float(LANES)), axis=-1, keepdims=True)
    tot = p1 + p2
    return lane, i1, i2, p1 / tot, p2 / tot


def _sort_tile_by_expert(logits, h_hi, tri_ref, hs_ref, route_ref, meta_ref):
    tm = logits.shape[0]
    lt = jnp.transpose(logits)
    lt = lt[0:N_EXPERTS, :] + lt[N_EXPERTS:2 * N_EXPERTS, :]
    sub = lax.broadcasted_iota(jnp.int32, (N_EXPERTS, tm), 0).astype(F32)
    ex = jnp.exp(lt - jnp.max(lt, axis=0, keepdims=True))
    probs = ex / jnp.sum(ex, axis=0, keepdims=True)
    p1 = jnp.max(probs, axis=0, keepdims=True)
    i1 = jnp.min(jnp.where(probs == p1, sub, float(N_EXPERTS)), axis=0, keepdims=True)
    rest = jnp.where(sub == i1, -1.0, probs)
    p2 = jnp.max(rest, axis=0, keepdims=True)
    i2 = jnp.min(jnp.where(rest == p2, sub, float(N_EXPERTS)), axis=0, keepdims=True)
    w1 = p1 / (p1 + p2)
    w2 = p2 / (p1 + p2)
    oh1 = jnp.where(sub == i1, 1.0, 0.0)
    oh2 = jnp.where(sub == i2, 1.0, 0.0)
    before1 = _dot(oh1.astype(BF16), tri_ref[...])
    before2 = _dot(oh2.astype(BF16), tri_ref[...])
    cnt1 = jnp.sum(oh1, axis=1, keepdims=True)
    cnt2 = jnp.sum(oh2, axis=1, keepdims=True)
    padded = jnp.floor((cnt1 + cnt2 + (MOE_CHUNK - 1.0)) * (1.0 / MOE_CHUNK)) * MOE_CHUNK
    run = jnp.broadcast_to(padded, (N_EXPERTS, LANES))
    sub8 = lax.broadcasted_iota(jnp.int32, (N_EXPERTS, LANES), 0)
    incl = run
    for k in (1, 2, 4):
        incl = incl + jnp.where(sub8 >= k, pltpu.roll(incl, k, axis=0), 0.0)
    start = (incl - run)[:, 0:1]
    d1 = jnp.sum(oh1 * (start + before1), axis=0, keepdims=True)
    d2 = jnp.sum(oh2 * (start + cnt1 + before2), axis=0, keepdims=True)
    row = lax.broadcasted_iota(jnp.int32, (MOE_TILE_CAP, tm), 0).astype(F32)
    perm = jnp.where(row == d1, 1.0, jnp.where(row == d2, 1.0, 0.0)).astype(BF16)
    hs_ref[...] = _dot(perm, h_hi).astype(BF16)
    info = jnp.concatenate([d1, d2, w1, w2, jnp.zeros((LANES - 4, tm), F32)], axis=0)
    route_ref[...] = jnp.transpose(info)
    lane8 = lax.broadcasted_iota(jnp.int32, (N_EXPERTS, LANES), 1)
    meta_ref[...] = jnp.where(lane8 == 0, start * (1.0 / MOE_CHUNK),
                              jnp.where(lane8 == 1, padded * (1.0 / MOE_CHUNK), 0.0))


def _post_mix_kernel(*refs, mode):
    (x_ref, oa_ref, ob_ref, oc_ref, wout_ref, gmem_ref, wq_ref, mk_ref, mv_ref, wo_ref, gffn_ref) = refs[:11]
    if mode == "dense":
        wg_ref, wu_ref, wd_ref, x3_ref = refs[11:]
    elif mode == "dense_split":
        x2_ref, h_ref = refs[11:]
    elif mode == "experts":
        router_ref, x2_ref, h_ref, route_ref = refs[11:]
    else:
        router_ref, tri_ref, x2_ref, h_ref, route_ref, meta_ref = refs[11:]
    y = (_dot(oa_ref[...], wout_ref[0:D_A, :])
         + _dot(ob_ref[...], wout_ref[D_A:D_A + D_B, :])
         + _dot(oc_ref[...], wout_ref[D_A + D_B:D_MODEL, :]))
    x1 = x_ref[...] + y
    q = _dot(_rms(x1, gmem_ref[...]).astype(BF16), wq_ref[...]).astype(BF16)
    outs = []
    for h in range(X_HEADS):
        cols = slice(h * X_HEAD_DIM, (h + 1) * X_HEAD_DIM)
        s = _dot_nt(q[:, cols], mk_ref[:, cols].astype(BF16)) * (X_HEAD_DIM ** -0.5)
        e = jnp.exp(s - jnp.max(s, axis=-1, keepdims=True))
        l = jnp.sum(e, axis=-1, keepdims=True)
        outs.append((_dot(e.astype(BF16), mv_ref[:, cols].astype(BF16)) / l).astype(BF16))
    x2 = x1 + _dot(jnp.concatenate(outs, axis=1), wo_ref[...])
    hf = _rms(x2, gffn_ref[...])
    h_hi = hf.astype(BF16)
    if mode == "dense":
        acc = x2
        for c in range(D_FF // FF_CHUNK):
            cols = slice(c * FF_CHUNK, (c + 1) * FF_CHUNK)
            g = _dot(h_hi, wg_ref[:, cols])
            u = _dot(h_hi, wu_ref[:, cols])
            acc = acc + _dot((g * jax.nn.sigmoid(g) * u).astype(BF16), wd_ref[cols, :])
        x3_ref[...] = acc
        return
    x2_ref[...] = x2
    if mode == "dense_split":
        h_ref[...] = h_hi
        return
    h_lo = (hf - h_hi.astype(F32)).astype(BF16)
    logits = _dot(h_hi, router_ref[...]) + _dot(h_lo, router_ref[...])
    if mode == "experts":
        h_ref[...] = h_hi
        lane, i1, i2, w1, w2 = _route(logits + pltpu.roll(logits, LANES - N_EXPERTS, axis=1))
        route_ref[...] = jnp.where(lane == i1, w1, jnp.where(lane == i2, w2, 0.0))
    else:
        _sort_tile_by_expert(logits, h_hi, tri_ref, h_ref, route_ref, meta_ref)


def _post_mix(x, oa, ob, oc, w_out, g_mem, wq, mk, mv, mem_base, wo, g_ffn, tail, mode):
    n, t, _ = x.shape
    tm = min(t, 512)
    tile = lambda w: pl.BlockSpec((None, tm, w), lambda b, i: (b, i, 0))
    mem = pl.BlockSpec((None,) + mk.shape[1:], lambda b, i: (mem_base + b, 0, 0))
    sq = _const_spec((D_MODEL, D_MODEL))
    vec = _const_spec((1, D_MODEL))
    in_specs = [tile(D_MODEL), tile(D_A), tile(D_B), tile(C_CHANNELS), sq, vec, sq, mem, mem, sq, vec]
    args = [x, oa, ob, oc, w_out, g_mem, wq, mk, mv, wo, g_ffn]
    out_specs = [tile(D_MODEL)]
    out_shape = [jax.ShapeDtypeStruct((n, t, D_MODEL), F32)]
    if mode == "dense":
        in_specs += [_const_spec((D_MODEL, D_FF)), _const_spec((D_MODEL, D_FF)), _const_spec((D_FF, D_MODEL))]
        args += list(tail)
        return pl.pallas_call(
            functools.partial(_post_mix_kernel, mode=mode),
            grid=(n, t // tm),
            in_specs=in_specs,
            out_specs=out_specs[0],
            out_shape=out_shape[0],
            compiler_params=_params(("parallel", "parallel")),
            name="post_mix_ffn",
        )(*args)
    if mode == "dense_split":
        return pl.pallas_call(
            functools.partial(_post_mix_kernel, mode=mode),
            grid=(n, t // tm),
            in_specs=in_specs,
            out_specs=[tile(D_MODEL), tile(D_MODEL)],
            out_shape=[out_shape[0], jax.ShapeDtypeStruct((n, t, D_MODEL), BF16)],
            compiler_params=_params(("parallel", "parallel")),
            name="post_mix",
        )(*args)
    in_specs.append(_const_spec((D_MODEL, LANES)))
    args.append(tail)
    if mode == "grouped":
        assert tm == MOE_TILE
        in_specs.append(_const_spec((tm, tm)))
        args.append(jnp.triu(jnp.ones((tm, tm), BF16), k=1))
        out_specs.append(pl.BlockSpec((None, MOE_TILE_CAP, D_MODEL), lambda b, i: (b, i, 0)))
        out_shape.append(jax.ShapeDtypeStruct((n, t // tm * MOE_TILE_CAP, D_MODEL), BF16))
    else:
        out_specs.append(tile(D_MODEL))
        out_shape.append(jax.ShapeDtypeStruct((n, t, D_MODEL), BF16))
    out_specs.append(tile(LANES))
    out_shape.append(jax.ShapeDtypeStruct((n, t, LANES), F32))
    if mode == "grouped":
        out_specs.append(pl.BlockSpec((None, N_EXPERTS, LANES), lambda b, i: (b, i, 0)))
        out_shape.append(jax.ShapeDtypeStruct((n, t // tm * N_EXPERTS, LANES), F32))
    return pl.pallas_call(
        functools.partial(_post_mix_kernel, mode=mode),
        grid=(n, t // tm),
        in_specs=in_specs,
        out_specs=out_specs,
        out_shape=out_shape,
        compiler_params=_params(("parallel", "parallel")),
        name="post_mix",
    )(*args)


def _ffn_kernel(x_ref, h_ref, wg_ref, wu_ref, wd_ref, o_ref):
    h = h_ref[...]
    acc = x_ref[...]
    for c in range(D_FF // FF_CHUNK):
        cols = slice(c * FF_CHUNK, (c + 1) * FF_CHUNK)
        g = _dot(h, wg_ref[:, cols])
        u = _dot(h, wu_ref[:, cols])
        acc = acc + _dot((g * jax.nn.sigmoid(g) * u).astype(BF16), wd_ref[cols, :])
    o_ref[...] = acc


def _ffn(x2d, h2d, wg, wu, wd):
    m = x2d.shape[0]
    tm = min(m, 512)
    row = pl.BlockSpec((tm, D_MODEL), lambda i: (i, 0))
    return pl.pallas_call(
        _ffn_kernel,
        grid=(m // tm,),
        in_specs=[row, row, _const_spec((D_MODEL, D_FF)), _const_spec((D_MODEL, D_FF)),
                  _const_spec((D_FF, D_MODEL))],
        out_specs=row,
        out_shape=jax.ShapeDtypeStruct((m, D_MODEL), F32),
        compiler_params=_params(("parallel",)),
        name="ffn_dense",
    )(x2d, h2d, wg, wu, wd)


def _moe_kernel(x_ref, h_ref, comb_ref, wg_ref, wu_ref, wd_ref, gfin_ref, o_ref, acc):
    e = pl.program_id(1)

    @pl.when(e == 0)
    def _():
        acc[...] = x_ref[...]

    h = h_ref[...]
    g = _dot(h, wg_ref[...])
    u = _dot(h, wu_ref[...])
    y = _dot((g * jax.nn.sigmoid(g) * u).astype(BF16), wd_ref[...])
    comb = comb_ref[...]
    lane = lax.broadcasted_iota(jnp.int32, comb.shape, 1)
    ce = jnp.sum(jnp.where(lane == e, comb, 0.0), axis=-1, keepdims=True)
    acc[...] += ce * y

    @pl.when(e == N_EXPERTS - 1)
    def _():
        o_ref[...] = _rms(acc[...], gfin_ref[...])


def _moe(x2d, h2d, comb2d, wg, wu, wd, g_fin):
    m = x2d.shape[0]
    tm = min(m, 512)
    row = lambda w: pl.BlockSpec((tm, w), lambda i, e: (i, 0))
    return pl.pallas_call(
        _moe_kernel,
        grid=(m // tm, N_EXPERTS),
        in_specs=[row(D_MODEL), row(D_MODEL), row(LANES),
                  pl.BlockSpec((None, D_MODEL, D_FF_EXPERT), lambda i, e: (e, 0, 0)),
                  pl.BlockSpec((None, D_MODEL, D_FF_EXPERT), lambda i, e: (e, 0, 0)),
                  pl.BlockSpec((None, D_FF_EXPERT, D_MODEL), lambda i, e: (e, 0, 0)),
                  _const_spec((1, D_MODEL))],
        out_specs=row(D_MODEL),
        out_shape=jax.ShapeDtypeStruct((m, D_MODEL), F32),
        scratch_shapes=[pltpu.VMEM((tm, D_MODEL), F32)],
        compiler_params=_params(("parallel", "arbitrary")),
        name="moe_experts",
    )(x2d, h2d, comb2d, wg, wu, wd, g_fin)


def _moe_plan(meta, n_tiles):
    cpb = MOE_BLOCK // MOE_CHUNK
    tile_chunks = MOE_TILE_CAP // MOE_CHUNK
    nb = -(-(n_tiles * tile_chunks + N_EXPERTS * (cpb - 1)) // cpb)
    first = meta[:, :, 0].astype(jnp.int32)
    cnt = meta[:, :, 1].astype(jnp.int32)
    tot = jnp.sum(cnt, axis=0)
    eend = jnp.cumsum((tot + cpb - 1) // cpb * cpb)
    q = jnp.arange(nb * cpb, dtype=jnp.int32)
    e_q = jnp.minimum(jnp.sum((q[:, None] >= eend[None, :]).astype(jnp.int32), axis=1), N_EXPERTS - 1)
    oh_e = (e_q[:, None] == jnp.arange(N_EXPERTS, dtype=jnp.int32)[None, :]).astype(jnp.int32)
    pick = lambda table: jnp.sum(oh_e[:, :, None] * table.T[None, :, :], axis=1)
    j = q - jnp.sum(oh_e * (eend - (tot + cpb - 1) // cpb * cpb)[None, :], axis=1)
    valid = (j < jnp.sum(oh_e * tot[None, :], axis=1)) & (q < eend[N_EXPERTS - 1])
    incl = jnp.cumsum(cnt, axis=0)
    tile_q = jnp.minimum(jnp.sum((j[:, None] >= pick(incl)).astype(jnp.int32), axis=1), n_tiles - 1)
    oh_t = (tile_q[:, None] == jnp.arange(n_tiles, dtype=jnp.int32)[None, :]).astype(jnp.int32)
    chunk = (tile_q * tile_chunks + jnp.sum(oh_t * pick(first), axis=1)
             + j - jnp.sum(oh_t * pick(incl - cnt), axis=1))
    k = jnp.cumsum(jnp.logical_not(valid).astype(jnp.int32)) - 1
    used = jnp.sum(cnt, axis=1)
    free_incl = jnp.cumsum(tile_chunks - used)
    tile_k = jnp.minimum(jnp.sum((k[:, None] >= free_incl[None, :]).astype(jnp.int32), axis=1), n_tiles - 1)
    oh_k = (tile_k[:, None] == jnp.arange(n_tiles, dtype=jnp.int32)[None, :]).astype(jnp.int32)
    in_tile = (tile_k * tile_chunks + jnp.sum(oh_k * used[None, :], axis=1)
               + k - jnp.sum(oh_k * (free_incl - (tile_chunks - used))[None, :], axis=1))
    n_free = free_incl[n_tiles - 1]
    spare = jnp.where(k < n_free, in_tile, n_tiles * tile_chunks + k - n_free)
    src = jnp.where(valid, chunk, 0)
    dst = jnp.where(valid, chunk, spare)
    blk_expert = e_q[::cpb]
    n_active = (eend[N_EXPERTS - 1] // cpb).reshape(1)
    return blk_expert, n_active, src.reshape(nb, 1, cpb), dst.reshape(nb, 1, cpb)


def _moe_grouped_kernel(be_ref, nact_ref, src_ref, src_next_ref, dst_ref, hs_hbm, wg_ref, wu_ref, wd_ref,
                        y_hbm, xbuf, ybuf, sem_in, sem_out):
    b = pl.program_id(0)
    nb = pl.num_programs(0)
    slot = b % 2
    cpb = MOE_BLOCK // MOE_CHUNK

    def chunk_rows(i):
        return pl.ds(pl.multiple_of(i * MOE_CHUNK, MOE_CHUNK), MOE_CHUNK)

    def gather(idx_ref, s):
        for c in range(cpb):
            pltpu.make_async_copy(hs_hbm.at[chunk_rows(idx_ref[0, c]), :],
                                  xbuf.at[s, c * MOE_CHUNK:(c + 1) * MOE_CHUNK, :],
                                  sem_in.at[s]).start(priority=c % 2)

    def wait_gather(s):
        pltpu.make_async_copy(hs_hbm.at[pl.ds(0, MOE_BLOCK), :], xbuf.at[s], sem_in.at[s]).wait()

    def wait_scatter(s):
        pltpu.make_async_copy(ybuf.at[s], y_hbm.at[pl.ds(0, MOE_BLOCK), :], sem_out.at[s]).wait()

    @pl.when(b == 0)
    def _():
        gather(src_ref, 0)

    @pl.when(b + 1 < nb)
    def _():
        gather(src_next_ref, 1 - slot)

    wait_gather(slot)

    @pl.when(b >= 2)
    def _():
        wait_scatter(slot)

    @pl.when(b < nact_ref[0])
    def _():
        x = xbuf[slot]
        g = _dot(x, wg_ref[...])
        u = _dot(x, wu_ref[...])
        ybuf[slot] = _dot((g * jax.nn.sigmoid(g) * u).astype(BF16), wd_ref[...]).astype(BF16)

    @pl.when(b >= nact_ref[0])
    def _():
        ybuf[slot] = jnp.zeros((MOE_BLOCK, D_MODEL), BF16)

    for c in range(cpb):
        pltpu.make_async_copy(ybuf.at[slot, c * MOE_CHUNK:(c + 1) * MOE_CHUNK, :],
                              y_hbm.at[chunk_rows(dst_ref[0, c]), :],
                              sem_out.at[slot]).start(priority=c % 2)

    @pl.when(b == nb - 1)
    def _():
        wait_scatter(slot)
        wait_scatter(1 - slot)


def _moe_grouped(hs, blk_expert, n_active, src, dst, wg, wu, wd):
    nb, _, cpb = src.shape
    assert nb >= 2
    idx_spec = lambda f: pl.BlockSpec((None, 1, cpb), f, memory_space=pltpu.SMEM)
    wspec = lambda shape: pl.BlockSpec((None,) + shape, lambda b, be, na: (be[b], 0, 0))
    return pl.pallas_call(
        _moe_grouped_kernel,
        grid_spec=pltpu.PrefetchScalarGridSpec(
            num_scalar_prefetch=2,
            grid=(nb,),
            in_specs=[idx_spec(lambda b, be, na: (b, 0, 0)),
                      idx_spec(lambda b, be, na: (jnp.minimum(b + 1, nb - 1), 0, 0)),
                      idx_spec(lambda b, be, na: (b, 0, 0)),
                      pl.BlockSpec(memory_space=pl.ANY),
                      wspec((D_MODEL, D_FF_EXPERT)), wspec((D_MODEL, D_FF_EXPERT)),
                      wspec((D_FF_EXPERT, D_MODEL))],
            out_specs=pl.BlockSpec(memory_space=pl.ANY),
            scratch_shapes=[pltpu.VMEM((2, MOE_BLOCK, D_MODEL), BF16), pltpu.VMEM((2, MOE_BLOCK, D_MODEL), BF16),
                            pltpu.SemaphoreType.DMA((2,)), pltpu.SemaphoreType.DMA((2,))]),
        out_shape=jax.ShapeDtypeStruct((nb * MOE_BLOCK, D_MODEL), BF16),
        compiler_params=_params(("arbitrary",)),
        name="moe_grouped",
    )(blk_expert, n_active, src, src, dst, hs, wg, wu, wd)


def _moe_combine_kernel(x_ref, route_ref, y_ref, gfin_ref, o_ref):
    tm = x_ref.shape[0]
    route = route_ref[...]
    d1, d2, w1, w2 = route[:, 0:1], route[:, 1:2], route[:, 2:3], route[:, 3:4]
    col = lax.broadcasted_iota(jnp.int32, (tm, MOE_TILE_CAP), 1).astype(F32)
    wm = jnp.where(col == d1, w1, jnp.where(col == d2, w2, 0.0)).astype(BF16)
    acc = x_ref[...] + _dot(wm, y_ref[...])
    o_ref[...] = _rms(acc, gfin_ref[...])


def _moe_combine(x2d, route2d, ys, g_fin):
    m = x2d.shape[0]
    tm = MOE_TILE
    return pl.pallas_call(
        _moe_combine_kernel,
        grid=(m // tm,),
        in_specs=[pl.BlockSpec((tm, D_MODEL), lambda i: (i, 0)),
                  pl.BlockSpec((tm, LANES), lambda i: (i, 0)),
                  pl.BlockSpec((MOE_TILE_CAP, D_MODEL), lambda i: (i, 0)),
                  _const_spec((1, D_MODEL))],
        out_specs=pl.BlockSpec((tm, D_MODEL), lambda i: (i, 0)),
        out_shape=jax.ShapeDtypeStruct((m, D_MODEL), F32),
        compiler_params=_params(("parallel",)),
        name="moe_combine",
    )(x2d, route2d, ys, g_fin)


def _layer(x, pos0, attn_hist, ret_state, conv_state, mk, mv, mem_base, w):
    n, t, _ = x.shape
    conv_w = (w["conv_w"], w["conv_b"], w["conv_ln_g"], w["conv_ln_b"])
    if conv_state is None:
        aq, akv, bqk, bv, bg, oc, new_conv = _in_proj_conv(x, w["norm_mix_g"], w["w_in"], *conv_w)
    else:
        aq, akv, bqk, bv, bg, cab = _in_proj(x.reshape(n * t, D_MODEL), w["norm_mix_g"], w["w_in"])
        r3 = lambda a: a.reshape(n, t, a.shape[-1])
        aq, akv, bqk, bv, bg, cab = map(r3, (aq, akv, bqk, bv, bg, cab))
        oc, new_conv = _conv_module(cab, conv_state, *conv_w)
    oa = _band_attn(aq, akv, attn_hist, w["rel_bias"])
    ob, new_s = _retention(bqk, bv, bg, ret_state, w["ret_gn_g"], pos0)
    m = n * t
    if "router" not in w:
        mode = "dense" if t >= MOE_TILE else "dense_split"
    elif m >= N_EXPERTS * MOE_BLOCK and t % MOE_TILE == 0:
        mode = "grouped"
    else:
        mode = "experts"
    ffn_w = (w.get("ffn_g"), w.get("ffn_u"), w.get("ffn_d"))
    tail = ffn_w if mode == "dense" else w.get("router")
    post = _post_mix(x, oa, ob, oc, w["w_out"], w["norm_mem_g"], w["wx_q"], mk, mv, mem_base, w["wx_o"],
                     w["norm_ffn_g"], tail, mode)
    if mode == "dense":
        x3 = post
    elif mode == "dense_split":
        x3 = _ffn(post[0].reshape(m, D_MODEL), post[1].reshape(m, D_MODEL), *ffn_w)
    elif mode == "experts":
        x2 = post[0].reshape(m, D_MODEL)
        x3 = _moe(x2, post[1].reshape(m, D_MODEL), post[2].reshape(m, LANES),
                  w["moe_g"], w["moe_u"], w["moe_d"], w["final_g"])
    else:
        x2 = post[0].reshape(m, D_MODEL)
        n_tiles = m // MOE_TILE
        blk_expert, n_active, src, dst = _moe_plan(post[3].reshape(n_tiles, N_EXPERTS, LANES), n_tiles)
        ys = _moe_grouped(post[1].reshape(n_tiles * MOE_TILE_CAP, D_MODEL), blk_expert, n_active, src, dst,
                          w["moe_g"], w["moe_u"], w["moe_d"])
        x3 = _moe_combine(x2, post[2].reshape(m, LANES), ys, w["final_g"])
    keep = min(A_REACH, t)
    new_kv = akv[:, t - keep:, :]
    new_k = new_kv[..., :D_A].reshape(n, keep, A_HEADS, A_HEAD_DIM)
    new_v = new_kv[..., D_A:].reshape(n, keep, A_HEADS, A_HEAD_DIM)
    return x3.reshape(n, t, D_MODEL), new_k, new_v, new_s, new_conv


def kernel(x_prompt, x_sample, cache_attn_k, cache_attn_v, state_ret, state_conv, cache_mem_k, cache_mem_v,
           mem_prompt, norm_mix_g, w_in, rel_bias, ret_gn_g, conv_w, conv_b, conv_ln_g, conv_ln_b, w_out,
           norm_mem_g, wx_q, wx_k, wx_v, wx_o, norm_ffn_g, ffn_w_gate, ffn_w_up, ffn_w_down,
           router_w, moe_w_gate, moe_w_up, moe_w_down, final_norm_g):
    depth = w_in.shape[0]
    assert depth == 2, "layer 0 dense FFN, layer 1 experts + closing norm"
    n_p, _, _ = x_prompt.shape
    n_s, t_s, _ = x_sample.shape
    n_mem = mem_prompt.shape[1]
    row = lambda a: a.reshape(1, -1).astype(F32)
    xp, xs = x_prompt, x_sample
    mem2d = mem_prompt.reshape(n_p * n_mem, D_MODEL)
    outs_p = [[] for _ in range(4)]
    outs_s = [[] for _ in range(4)]
    mk_all, mv_all, mk_heads, mv_heads = _mem_kv(mem2d, wx_k.astype(BF16), wx_v.astype(BF16), n_mem)
    mk_p = mk_all.reshape(depth * n_p, n_mem, D_MODEL)
    mv_p = mv_all.reshape(depth * n_p, n_mem, D_MODEL)
    mk_s = cache_mem_k.reshape(depth * n_s, n_mem, D_MODEL)
    mv_s = cache_mem_v.reshape(depth * n_s, n_mem, D_MODEL)
    for l in range(depth):
        w = dict(norm_mix_g=row(norm_mix_g[l]), w_in=w_in[l].astype(BF16), rel_bias=rel_bias[l],
                 ret_gn_g=row(ret_gn_g[l]), conv_w=conv_w[l].astype(F32), conv_b=row(conv_b[l]),
                 conv_ln_g=row(conv_ln_g[l]), conv_ln_b=row(conv_ln_b[l]), w_out=w_out[l].astype(BF16),
                 norm_mem_g=row(norm_mem_g[l]), wx_q=wx_q[l].astype(BF16), wx_o=wx_o[l].astype(BF16),
                 norm_ffn_g=row(norm_ffn_g[l]))
        if l % 2 == 0:
            i = l // 2
            w.update(ffn_g=ffn_w_gate[i].astype(BF16), ffn_u=ffn_w_up[i].astype(BF16),
                     ffn_d=ffn_w_down[i].astype(BF16))
        else:
            i = l // 2
            r = router_w[i].astype(F32)
            r_hi = r.astype(BF16)
            r_lo = (r - r_hi.astype(F32)).astype(BF16)
            w.update(router=jnp.pad(jnp.concatenate([r_hi, r_lo], axis=1),
                                    ((0, 0), (0, LANES - 2 * N_EXPERTS))),
                     moe_g=moe_w_gate[i].astype(BF16), moe_u=moe_w_up[i].astype(BF16),
                     moe_d=moe_w_down[i].astype(BF16),
                     final_g=row(final_norm_g) if l == depth - 1 else None)
        xp, k_new, v_new, s_new, c_new = _layer(
            xp, 0.0, None, jnp.zeros((n_p, B_HEADS, B_QK_DIM, B_V_DIM), F32), None, mk_p, mv_p, l * n_p, w)
        for lst, a in zip(outs_p, (k_new, v_new, s_new, c_new)):
            lst.append(a)
        hist = (cache_attn_k[l].reshape(n_s, A_REACH, D_A), cache_attn_v[l].reshape(n_s, A_REACH, D_A))
        xs, k_new, v_new, s_new, c_new = _layer(
            xs, float(PAST_LEN), hist, state_ret[l], state_conv[l], mk_s, mv_s, l * n_s, w)
        for lst, a in zip(outs_s, (k_new, v_new, s_new, c_new)):
            lst.append(a)
    st = lambda lst: jnp.stack(lst)
    return (xp, xs, st(outs_p[0]), st(outs_p[1]), st(outs_p[2]), st(outs_p[3]), mk_heads, mv_heads,
            st(outs_s[0]), st(outs_s[1]), st(outs_s[2]), st(outs_s[3]))
```

```python
import functools

import numpy as np
import jax
import jax.numpy as jnp
from jax import lax
from jax.experimental import pallas as pl
from jax.experimental.pallas import tpu as pltpu

F32 = jnp.float32
BF16 = jnp.bfloat16

D_MODEL = 1024
PAST_LEN = 2048
CHUNK = 64
EPS = 1e-6
NEG_INF = -1e30
LANES = 128

A_HEADS = 4
A_HEAD_DIM = 64
D_A = A_HEADS * A_HEAD_DIM
A_LEFT_CHUNKS = 8
A_REACH = A_LEFT_CHUNKS * CHUNK
REL_CLIP = 128
A_SCALE = A_HEAD_DIM ** -0.5

B_HEADS = 4
B_QK_DIM = 64
B_V_DIM = 128
D_BQK = B_HEADS * B_QK_DIM
D_B = B_HEADS * B_V_DIM
RET_GAMMA_EXP0 = 5.0
ROPE_BASE = 10000.0

C_CHANNELS = 256
CONV_WIDTH = 31
CONV_PAD = 32

X_HEADS = 4
X_HEAD_DIM = D_MODEL // X_HEADS

D_FF = 11 * D_MODEL // 4
FF_CHUNK = 256
N_EXPERTS = 8
TOP_K = 2
D_FF_EXPERT = D_FF // 2
MOE_TILE = 512
MOE_CHUNK = 16
MOE_TILE_CAP = 1152
MOE_BLOCK = 512

COL_QA, COL_KVA, COL_QKB, COL_VB, COL_GB, COL_CAB, D_IN = 0, 256, 768, 1280, 1792, 2304, 2816

VMEM_LIMIT = 56 * 1024 * 1024


def _params(sem):
    return pltpu.CompilerParams(dimension_semantics=sem, vmem_limit_bytes=VMEM_LIMIT)


def _rms(x, g):
    return x * lax.rsqrt(jnp.mean(x * x, axis=-1, keepdims=True) + EPS) * g


def _dot(a, b):
    return jnp.dot(a, b, preferred_element_type=F32)


def _dot_nt(a, b):
    return lax.dot_general(a, b, (((1,), (1,)), ((), ())), preferred_element_type=F32)


def _dot_tn(a, b):
    return lax.dot_general(a, b, (((0,), (0,)), ((), ())), preferred_element_type=F32)


def _const_spec(shape):
    return pl.BlockSpec(shape, lambda *_: (0,) * len(shape), pipeline_mode=pl.Buffered(1))


def _in_proj_kernel(x_ref, g_ref, w_ref, aq_ref, akv_ref, bqk_ref, bv_ref, bg_ref, cab_ref):
    h = _rms(x_ref[...], g_ref[...]).astype(BF16)
    aq_ref[...] = _dot(h, w_ref[:, COL_QA:COL_KVA]).astype(BF16)
    akv_ref[...] = _dot(h, w_ref[:, COL_KVA:COL_QKB])
    bqk_ref[...] = _dot(h, w_ref[:, COL_QKB:COL_VB]).astype(BF16)
    bv_ref[...] = _dot(h, w_ref[:, COL_VB:COL_GB]).astype(BF16)
    bg_ref[...] = _dot(h, w_ref[:, COL_GB:COL_CAB]).astype(BF16)
    cab_ref[...] = _dot(h, w_ref[:, COL_CAB:D_IN])


def _conv_tile(ext, w_ref, b_ref, lng_ref, lnb_ref, tt):
    hist = CONV_WIDTH - 1
    acc = jnp.zeros((tt, C_CHANNELS), F32)
    for b in range(8):
        rb = ext if b == 0 else pltpu.roll(ext, b, axis=0)
        for a in range(CONV_PAD // 8):
            d = 8 * a + b
            if d > hist:
                continue
            k = hist - d
            acc = acc + w_ref[k:k + 1, :] * rb[CONV_PAD - 8 * a:CONV_PAD - 8 * a + tt, :]
    y = acc + b_ref[...]
    mu = jnp.mean(y, axis=-1, keepdims=True)
    d0 = y - mu
    var = jnp.mean(d0 * d0, axis=-1, keepdims=True)
    yn = d0 * lax.rsqrt(var + EPS) * lng_ref[...] + lnb_ref[...]
    return yn * jax.nn.sigmoid(yn)


def _in_proj_conv_kernel(x_ref, g_ref, w_ref, cw_ref, cb_ref, lng_ref, lnb_ref,
                         aq_ref, akv_ref, bqk_ref, bv_ref, bg_ref, oc_ref, nc_ref, kt_ref, vt_ref, up,
                         *, tm, tt):
    hist = CONV_WIDTH - 1

    @pl.when(pl.program_id(1) == 0)
    def _():
        up[tm:tm + CONV_PAD, :] = jnp.zeros((CONV_PAD, C_CHANNELS), F32)

    h = _rms(x_ref[...], g_ref[...]).astype(BF16)
    cab = _dot(h, w_ref[:, COL_CAB:D_IN])
    up[0:CONV_PAD, :] = up[tm:tm + CONV_PAD, :]
    up[CONV_PAD:CONV_PAD + tm, :] = cab[:, 0:C_CHANNELS] * jax.nn.sigmoid(cab[:, C_CHANNELS:2 * C_CHANNELS])
    nc_ref[...] = up[CONV_PAD + tm - hist:CONV_PAD + tm, :]
    cols = ((aq_ref, COL_QA, COL_KVA), (akv_ref, COL_KVA, COL_QKB), (bqk_ref, COL_QKB, COL_VB),
            (bv_ref, COL_VB, COL_GB), (bg_ref, COL_GB, COL_CAB))
    for i in range(tm // tt):
        ext = up[i * tt:i * tt + tt + CONV_PAD, :]
        oc_ref[i * tt:(i + 1) * tt, :] = _conv_tile(ext, cw_ref, cb_ref, lng_ref, lnb_ref, tt).astype(BF16)
        for o_ref, lo, hi in cols[i::tm // tt]:
            res = _dot(h, w_ref[:, lo:hi])
            o_ref[...] = res.astype(BF16)
            if o_ref is akv_ref:
                kt_ref[...] = res[:, 0:D_A]
                vt_ref[...] = res[:, D_A:2 * D_A]


def _in_proj_conv(x, g, w_in, conv_w, conv_b, ln_g, ln_b):
    n, t, _ = x.shape
    tm = min(t, 512)
    assert tm == A_REACH and t % tm == 0
    tt = min(tm, 128)
    hist = CONV_WIDTH - 1
    widths = (D_A, 2 * D_A, 2 * D_BQK, D_B, D_B, C_CHANNELS)
    tile = lambda w: pl.BlockSpec((None, tm, w), lambda b, j: (b, j, 0))
    last = lambda rows, w: pl.BlockSpec((None, rows, w), lambda b, j: (b, 0, 0))
    return pl.pallas_call(
        functools.partial(_in_proj_conv_kernel, tm=tm, tt=tt),
        grid=(n, t // tm),
        in_specs=[tile(D_MODEL), _const_spec((1, D_MODEL)), _const_spec((D_MODEL, D_IN)),
                  _const_spec((CONV_WIDTH, C_CHANNELS))] + [_const_spec((1, C_CHANNELS))] * 3,
        out_specs=[tile(w) for w in widths] + [last(hist, C_CHANNELS), last(tm, D_A), last(tm, D_A)],
        out_shape=[jax.ShapeDtypeStruct((n, t, w), BF16) for w in widths]
                  + [jax.ShapeDtypeStruct((n, hist, C_CHANNELS), F32)]
                  + [jax.ShapeDtypeStruct((n, tm, D_A), F32)] * 2,
        scratch_shapes=[pltpu.VMEM((CONV_PAD + tm, C_CHANNELS), F32)],
        compiler_params=_params(("parallel", "arbitrary")),
        name="in_proj_conv",
    )(x, g, w_in, conv_w, conv_b, ln_g, ln_b)


def _in_proj(x2d, g, w_in):
    m = x2d.shape[0]
    tm = min(m, 512)
    widths = ((256, BF16), (512, F32), (512, BF16), (512, BF16), (512, BF16), (512, F32))
    return pl.pallas_call(
        _in_proj_kernel,
        grid=(m // tm,),
        in_specs=[pl.BlockSpec((tm, D_MODEL), lambda i: (i, 0)),
                  _const_spec((1, D_MODEL)),
                  _const_spec((D_MODEL, D_IN))],
        out_specs=[pl.BlockSpec((tm, w), lambda i: (i, 0)) for w, _ in widths],
        out_shape=[jax.ShapeDtypeStruct((m, w), dt) for w, dt in widths],
        compiler_params=_params(("parallel",)),
        name="in_proj",
    )(x2d, g, w_in)


def _band_attn_kernel(*refs, t, tq, nsub, has_hist):
    if has_hist:
        aq_ref, akv_ref, hk_ref, hv_ref, bias_ref, o_ref, kc, vc = refs
    else:
        aq_ref, akv_ref, bias_ref, o_ref, kc, vc = refs
    j = pl.program_id(1)
    span = A_REACH + tq

    @pl.when(j == 0)
    def _():
        if has_hist:
            kc[0:A_REACH, :] = hk_ref[...].astype(BF16)
            vc[0:A_REACH, :] = hv_ref[...].astype(BF16)
        else:
            kc[0:A_REACH, :] = jnp.zeros((A_REACH, D_A), BF16)
            vc[0:A_REACH, :] = jnp.zeros((A_REACH, D_A), BF16)
        kc[A_REACH:A_REACH + t, :] = akv_ref[:, 0:D_A].astype(BF16)
        vc[A_REACH:A_REACH + t, :] = akv_ref[:, D_A:2 * D_A].astype(BF16)

    lane = lax.broadcasted_iota(jnp.int32, (tq, LANES), 1)
    col = lax.broadcasted_iota(jnp.int32, (tq, span), 1)
    for sub in range(nsub):
        t0 = pl.multiple_of((j * nsub + sub) * tq, tq)
        q = aq_ref[sub * tq:(sub + 1) * tq, :]
        outs = []
        for p in range(A_HEADS // 2):
            qp = q[:, p * LANES:(p + 1) * LANES]
            kp = kc[pl.ds(t0, span), p * LANES:(p + 1) * LANES]
            vp = vc[pl.ds(t0, span), p * LANES:(p + 1) * LANES]
            o_pair = None
            for hh in range(2):
                own = (lane < A_HEAD_DIM) if hh == 0 else (lane >= A_HEAD_DIM)
                qm = jnp.where(own, qp.astype(F32) * A_SCALE, 0.0).astype(BF16)
                s = _dot_nt(qm, kp) + bias_ref[2 * p + hh]
                if not has_hist:
                    s = jnp.where(col >= A_REACH - t0, s, NEG_INF)
                m = jnp.max(s, axis=-1, keepdims=True)
                e = jnp.exp(s - m)
                l = jnp.sum(e, axis=-1, keepdims=True)
                o = _dot(e.astype(BF16), vp) / l
                o_pair = o if hh == 0 else jnp.where(own, o, o_pair)
            outs.append(o_pair)
        o_ref[sub * tq:(sub + 1) * tq, :] = jnp.concatenate(outs, axis=1).astype(BF16)


def _band_bias(rel_bias_l, tq):
    span = A_REACH + tq
    period = span + tq
    n_far = A_REACH - REL_CLIP + 1
    far = rel_bias_l[:, 2 * REL_CLIP:]
    n_near = min(2 * REL_CLIP, span - n_far)
    near = jnp.flip(rel_bias_l[:, :2 * REL_CLIP], axis=1)[:, :n_near]
    beyond = jnp.broadcast_to(rel_bias_l[:, :1], (A_HEADS, span - n_far - n_near))
    u = jnp.concatenate([jnp.broadcast_to(far, (A_HEADS, n_far)), near, beyond,
                         jnp.broadcast_to(far, (A_HEADS, tq))], axis=1).astype(F32)
    skew = jnp.broadcast_to(u[:, None, :], (A_HEADS, tq, period)).reshape(A_HEADS, tq * period)
    toeplitz = skew[:, :tq * (period - 1)].reshape(A_HEADS, tq, period - 1)[:, :, :span]
    i = np.arange(tq)[:, None]
    j = np.arange(span)[None, :]
    in_band = (j // CHUNK >= i // CHUNK) & (j // CHUNK <= i // CHUNK + A_LEFT_CHUNKS)
    return jnp.where(in_band[None], toeplitz, NEG_INF)


def _band_attn(aq, akv, hist, rel_bias_l):
    n, t, _ = aq.shape
    tq = min(t, 256)
    nsub = 2 if t % (2 * tq) == 0 else 1
    span = A_REACH + tq
    bias = _band_bias(rel_bias_l, tq)
    has_hist = hist is not None
    in_specs = [pl.BlockSpec((None, nsub * tq, D_A), lambda b, j: (b, j, 0)),
                pl.BlockSpec((None, t, 2 * D_A), lambda b, j: (b, 0, 0))]
    args = [aq, akv]
    if has_hist:
        in_specs += [pl.BlockSpec((None, A_REACH, D_A), lambda b, j: (b, 0, 0))] * 2
        args += list(hist)
    in_specs.append(_const_spec((A_HEADS, tq, span)))
    args.append(bias)
    return pl.pallas_call(
        functools.partial(_band_attn_kernel, t=t, tq=tq, nsub=nsub, has_hist=has_hist),
        grid=(n, t // (nsub * tq)),
        in_specs=in_specs,
        out_specs=pl.BlockSpec((None, nsub * tq, D_A), lambda b, j: (b, j, 0)),
        out_shape=jax.ShapeDtypeStruct((n, t, D_A), BF16),
        scratch_shapes=[pltpu.VMEM((A_REACH + t, D_A), BF16)] * 2,
        compiler_params=_params(("parallel", "arbitrary")),
        name="band_attn",
    )(*args)


def _swap_halves(x):
    lane = lax.broadcasted_iota(jnp.int32, x.shape, 1)
    first = (lane % B_QK_DIM) < (B_QK_DIM // 2)
    return jnp.where(first, pltpu.roll(x, LANES - B_QK_DIM // 2, axis=1), pltpu.roll(x, B_QK_DIM // 2, axis=1))


def _retention_kernel(bqk_ref, bv_ref, bg_ref, cos_ref, sin_ref, dmask_ref, qdec_ref, kdec_ref, sdec_ref,
                      s0_ref, gn_ref, ob_ref, sfin_ref, st, *, t, bc):
    for h in range(B_HEADS):
        off = (h % 2) * B_QK_DIM
        st[h] = jnp.zeros((LANES, B_V_DIM), F32)
        st[h, off:off + B_QK_DIM, :] = s0_ref[h]

    lane = lax.broadcasted_iota(jnp.int32, (bc, LANES), 1)

    def chunk(c, carry):
        r0 = pl.multiple_of(c * bc, bc)
        rows = pl.ds(r0, bc)
        cs = cos_ref[rows, :]
        sn = sin_ref[rows, :]
        for p in range(B_HEADS // 2):
            qx = bqk_ref[rows, p * LANES:(p + 1) * LANES].astype(F32)
            kx = bqk_ref[rows, D_BQK + p * LANES:D_BQK + (p + 1) * LANES].astype(F32)
            qr = qx * cs + _swap_halves(qx) * sn
            kr = (kx * cs + _swap_halves(kx) * sn) * (B_QK_DIM ** -0.5)
            for hh in range(2):
                h = 2 * p + hh
                own = (lane < B_QK_DIM) if hh == 0 else (lane >= B_QK_DIM)
                qh = jnp.where(own, qr, 0.0)
                kh = jnp.where(own, kr, 0.0)
                v = bv_ref[rows, h * B_V_DIM:(h + 1) * B_V_DIM]
                att = _dot_nt(qh.astype(BF16), kh.astype(BF16)) * dmask_ref[h]
                o = (_dot(att.astype(BF16), v)
                     + _dot((qh * qdec_ref[h]).astype(BF16), st[h].astype(BF16)))
                st[h] = st[h] * sdec_ref[h] + _dot_tn((kh * kdec_ref[h]).astype(BF16), v)
                mu = jnp.mean(o, axis=-1, keepdims=True)
                d = o - mu
                var = jnp.mean(d * d, axis=-1, keepdims=True)
                on = d * lax.rsqrt(var + EPS) * gn_ref[:, h * B_V_DIM:(h + 1) * B_V_DIM]
                g = bg_ref[rows, h * B_V_DIM:(h + 1) * B_V_DIM].astype(F32)
                ob_ref[rows, h * B_V_DIM:(h + 1) * B_V_DIM] = (g * jax.nn.sigmoid(g) * on).astype(BF16)
        return carry

    lax.fori_loop(0, t // bc, chunk, 0)
    for h in range(B_HEADS):
        off = (h % 2) * B_QK_DIM
        sfin_ref[h] = st[h, off:off + B_QK_DIM, :]


def _retention_tables(t, bc, pos0):
    log_g = jnp.log(1.0 - 2.0 ** (-(RET_GAMMA_EXP0 + jnp.arange(B_HEADS, dtype=F32))))
    i = jnp.arange(bc, dtype=F32)
    diff = i[:, None] - i[None, :]
    dmask = jnp.where(diff[None] >= 0, jnp.exp(jnp.maximum(diff, 0.0)[None] * log_g[:, None, None]), 0.0)
    qdec = jnp.exp((i[None, :] + 1.0) * log_g[:, None])
    kdec = jnp.exp((bc - 1.0 - i)[None, :] * log_g[:, None])
    sdec = jnp.exp(bc * log_g)
    qdec = jnp.broadcast_to(qdec[:, :, None], (B_HEADS, bc, LANES))
    kdec = jnp.broadcast_to(kdec[:, :, None], (B_HEADS, bc, LANES))
    sdec = jnp.broadcast_to(sdec[:, None, None], (B_HEADS, 1, B_V_DIM))
    half = B_QK_DIM // 2
    pos = pos0 + jnp.arange(t, dtype=F32)
    inv_freq = ROPE_BASE ** (-jnp.arange(half, dtype=F32) / half)
    ang = pos[:, None] * inv_freq[None, :]
    cos = jnp.tile(jnp.cos(ang), (1, LANES // half))
    sin = jnp.tile(jnp.concatenate([-jnp.sin(ang), jnp.sin(ang)], axis=1), (1, LANES // B_QK_DIM))
    return cos, sin, dmask, qdec, kdec, sdec


def _retention(bqk, bv, bg, s0, gn_g, pos0):
    n, t, _ = bqk.shape
    bc = min(t, 256)
    tables = _retention_tables(t, bc, pos0)
    seq = lambda w: pl.BlockSpec((None, t, w), lambda b: (b, 0, 0))
    state = pl.BlockSpec((None, B_HEADS, B_QK_DIM, B_V_DIM), lambda b: (b, 0, 0, 0))
    return pl.pallas_call(
        functools.partial(_retention_kernel, t=t, bc=bc),
        grid=(n,),
        in_specs=[seq(2 * D_BQK), seq(D_B), seq(D_B)] + [_const_spec(tb.shape) for tb in tables]
                 + [state, _const_spec((1, D_B))],
        out_specs=[seq(D_B), state],
        out_shape=[jax.ShapeDtypeStruct((n, t, D_B), BF16),
                   jax.ShapeDtypeStruct((n, B_HEADS, B_QK_DIM, B_V_DIM), F32)],
        scratch_shapes=[pltpu.VMEM((B_HEADS, LANES, B_V_DIM), F32)],
        compiler_params=_params(("parallel",)),
        name="retention",
    )(bqk, bv, bg, *tables, s0, gn_g)


def _conv_kernel(cab_ref, buf_ref, w_ref, b_ref, lng_ref, lnb_ref, oc_ref, nc_ref, up, *, t, tt):
    hist = CONV_WIDTH - 1
    up[0:CONV_PAD, :] = jnp.zeros((CONV_PAD, C_CHANNELS), F32)
    up[CONV_PAD - hist:CONV_PAD, :] = buf_ref[...]

    def glu(i, carry):
        rows = pl.ds(pl.multiple_of(i * tt, tt), tt)
        ca = cab_ref[rows, 0:C_CHANNELS]
        cb = cab_ref[rows, C_CHANNELS:2 * C_CHANNELS]
        up[pl.ds(pl.multiple_of(CONV_PAD + i * tt, 8), tt), :] = ca * jax.nn.sigmoid(cb)
        return carry

    lax.fori_loop(0, t // tt, glu, 0)

    def tile(i, carry):
        t0 = pl.multiple_of(i * tt, tt)
        ext = up[pl.ds(t0, tt + CONV_PAD), :]
        oc_ref[pl.ds(t0, tt), :] = _conv_tile(ext, w_ref, b_ref, lng_ref, lnb_ref, tt).astype(BF16)
        return carry

    lax.fori_loop(0, t // tt, tile, 0)
    nc_ref[...] = up[CONV_PAD + t - hist:CONV_PAD + t, :]


def _conv_module(cab, buf, conv_w, conv_b, ln_g, ln_b):
    n, t, _ = cab.shape
    tt = min(t, 128)
    hist = CONV_WIDTH - 1
    return pl.pallas_call(
        functools.partial(_conv_kernel, t=t, tt=tt),
        grid=(n,),
        in_specs=[pl.BlockSpec((None, t, 2 * C_CHANNELS), lambda b: (b, 0, 0)),
                  pl.BlockSpec((None, hist, C_CHANNELS), lambda b: (b, 0, 0)),
                  _const_spec((CONV_WIDTH, C_CHANNELS))] + [_const_spec((1, C_CHANNELS))] * 3,
        out_specs=[pl.BlockSpec((None, t, C_CHANNELS), lambda b: (b, 0, 0)),
                   pl.BlockSpec((None, hist, C_CHANNELS), lambda b: (b, 0, 0))],
        out_shape=[jax.ShapeDtypeStruct((n, t, C_CHANNELS), BF16),
                   jax.ShapeDtypeStruct((n, hist, C_CHANNELS), F32)],
        scratch_shapes=[pltpu.VMEM((CONV_PAD + t, C_CHANNELS), F32)],
        compiler_params=_params(("parallel",)),
        name="conv_module",
    )(cab, buf, conv_w, conv_b, ln_g, ln_b)


def _mem_kv_kernel(mem_ref, wk_ref, wv_ref, mk_ref, mv_ref, mkh_ref, mvh_ref):
    m = mem_ref[...].astype(BF16)
    n_mem = mkh_ref.shape[1]
    for w_ref, rows_ref, heads_ref in ((wk_ref, mk_ref, mkh_ref), (wv_ref, mv_ref, mvh_ref)):
        res = _dot(m, w_ref[...])
        rows_ref[...] = res.astype(BF16)
        for b in range(heads_ref.shape[0]):
            for h in range(X_HEADS):
                heads_ref[b, :, h, :] = res[b * n_mem:(b + 1) * n_mem, h * X_HEAD_DIM:(h + 1) * X_HEAD_DIM]


def _mem_kv(mem2d, wk, wv, n_mem):
    depth = wk.shape[0]
    m = mem2d.shape[0]
    tm = 512
    nb = tm // n_mem
    wspec = pl.BlockSpec((None, D_MODEL, D_MODEL), lambda l, i: (l, 0, 0))
    ospec = pl.BlockSpec((None, tm, D_MODEL), lambda l, i: (l, i, 0))
    hspec = pl.BlockSpec((None, nb, n_mem, X_HEADS, X_HEAD_DIM), lambda l, i: (l, i, 0, 0, 0))
    heads = jax.ShapeDtypeStruct((depth, m // n_mem, n_mem, X_HEADS, X_HEAD_DIM), F32)
    return pl.pallas_call(
        _mem_kv_kernel,
        grid=(depth, m // tm),
        in_specs=[pl.BlockSpec((tm, D_MODEL), lambda l, i: (i, 0)), wspec, wspec],
        out_specs=[ospec, ospec, hspec, hspec],
        out_shape=[jax.ShapeDtypeStruct((depth, m, D_MODEL), BF16)] * 2 + [heads, heads],
        compiler_params=_params(("parallel", "parallel")),
        name="mem_kv",
    )(mem2d, wk, wv)


def _route(logits):
    lane = lax.broadcasted_iota(jnp.int32, logits.shape, 1).astype(F32)
    valid = lane < N_EXPERTS
    lg = jnp.where(valid, logits, NEG_INF)
    e = jnp.exp(lg - jnp.max(lg, axis=-1, keepdims=True))
    probs = e / jnp.sum(e, axis=-1, keepdims=True)
    p1 = jnp.max(probs, axis=-1, keepdims=True)
    i1 = jnp.min(jnp.where(probs == p1, lane, float(LANES)), axis=-1, keepdims=True)
    rest = jnp.where(lane == i1, -1.0, probs)
    rest = jnp.where(valid, rest, -1.0)
    p2 = jnp.max(rest, axis=-1, keepdims=True)
    i2 = jnp.min(jnp.where(rest == p2, lane, float(LANES)), axis=-1, keepdims=True)
    tot = p1 + p2
    return lane, i1, i2, p1 / tot, p2 / tot


def _sort_tile_by_expert(logits, h_hi, tri_ref, hs_ref, route_ref, meta_ref):
    tm = logits.shape[0]
    lt = jnp.transpose(logits)
    lt = lt[0:N_EXPERTS, :] + lt[N_EXPERTS:2 * N_EXPERTS, :]
    sub = lax.broadcasted_iota(jnp.int32, (N_EXPERTS, tm), 0).astype(F32)
    ex = jnp.exp(lt - jnp.max(lt, axis=0, keepdims=True))
    probs = ex / jnp.sum(ex, axis=0, keepdims=True)
    p1 = jnp.max(probs, axis=0, keepdims=True)
    i1 = jnp.min(jnp.where(probs == p1, sub, float(N_EXPERTS)), axis=0, keepdims=True)
    rest = jnp.where(sub == i1, -1.0, probs)
    p2 = jnp.max(rest, axis=0, keepdims=True)
    i2 = jnp.min(jnp.where(rest == p2, sub, float(N_EXPERTS)), axis=0, keepdims=True)
    w1 = p1 / (p1 + p2)
    w2 = p2 / (p1 + p2)
    oh1 = jnp.where(sub == i1, 1.0, 0.0)
    oh2 = jnp.where(sub == i2, 1.0, 0.0)
    before1 = _dot(oh1.astype(BF16), tri_ref[...])
    before2 = _dot(oh2.astype(BF16), tri_ref[...])
    cnt1 = jnp.sum(oh1, axis=1, keepdims=True)
    cnt2 = jnp.sum(oh2, axis=1, keepdims=True)
    padded = jnp.floor((cnt1 + cnt2 + (MOE_CHUNK - 1.0)) * (1.0 / MOE_CHUNK)) * MOE_CHUNK
    run = jnp.broadcast_to(padded, (N_EXPERTS, LANES))
    sub8 = lax.broadcasted_iota(jnp.int32, (N_EXPERTS, LANES), 0)
    incl = run
    for k in (1, 2, 4):
        incl = incl + jnp.where(sub8 >= k, pltpu.roll(incl, k, axis=0), 0.0)
    start = (incl - run)[:, 0:1]
    d1 = jnp.sum(oh1 * (start + before1), axis=0, keepdims=True)
    d2 = jnp.sum(oh2 * (start + cnt1 + before2), axis=0, keepdims=True)
    row = lax.broadcasted_iota(jnp.int32, (MOE_TILE_CAP, tm), 0).astype(F32)
    perm = jnp.where(row == d1, 1.0, jnp.where(row == d2, 1.0, 0.0)).astype(BF16)
    hs_ref[...] = _dot(perm, h_hi).astype(BF16)
    info = jnp.concatenate([d1, d2, w1, w2, jnp.zeros((LANES - 4, tm), F32)], axis=0)
    route_ref[...] = jnp.transpose(info)
    lane8 = lax.broadcasted_iota(jnp.int32, (N_EXPERTS, LANES), 1)
    meta_ref[...] = jnp.where(lane8 == 0, start * (1.0 / MOE_CHUNK),
                              jnp.where(lane8 == 1, padded * (1.0 / MOE_CHUNK), 0.0))


def _post_mix_kernel(*refs, mode):
    (x_ref, oa_ref, ob_ref, oc_ref, wout_ref, gmem_ref, wq_ref, mk_ref, mv_ref, wo_ref, gffn_ref) = refs[:11]
    if mode == "dense":
        wg_ref, wu_ref, wd_ref, x3_ref = refs[11:]
    elif mode == "dense_split":
        x2_ref, h_ref = refs[11:]
    elif mode == "experts":
        router_ref, x2_ref, h_ref, route_ref = refs[11:]
    else:
        router_ref, tri_ref, x2_ref, h_ref, route_ref, meta_ref = refs[11:]
    y = (_dot(oa_ref[...], wout_ref[0:D_A, :])
         + _dot(ob_ref[...], wout_ref[D_A:D_A + D_B, :])
         + _dot(oc_ref[...], wout_ref[D_A + D_B:D_MODEL, :]))
    x1 = x_ref[...] + y
    q = _dot(_rms(x1, gmem_ref[...]).astype(BF16), wq_ref[...]).astype(BF16)
    outs = []
    for h in range(X_HEADS):
        cols = slice(h * X_HEAD_DIM, (h + 1) * X_HEAD_DIM)
        s = _dot_nt(q[:, cols], mk_ref[:, cols].astype(BF16)) * (X_HEAD_DIM ** -0.5)
        e = jnp.exp(s - jnp.max(s, axis=-1, keepdims=True))
        l = jnp.sum(e, axis=-1, keepdims=True)
        outs.append((_dot(e.astype(BF16), mv_ref[:, cols].astype(BF16)) / l).astype(BF16))
    x2 = x1 + _dot(jnp.concatenate(outs, axis=1), wo_ref[...])
    hf = _rms(x2, gffn_ref[...])
    h_hi = hf.astype(BF16)
    if mode == "dense":
        acc = x2
        for c in range(D_FF // FF_CHUNK):
            cols = slice(c * FF_CHUNK, (c + 1) * FF_CHUNK)
            g = _dot(h_hi, wg_ref[:, cols])
            u = _dot(h_hi, wu_ref[:, cols])
            acc = acc + _dot((g * jax.nn.sigmoid(g) * u).astype(BF16), wd_ref[cols, :])
        x3_ref[...] = acc
        return
    x2_ref[...] = x2
    if mode == "dense_split":
        h_ref[...] = h_hi
        return
    h_lo = (hf - h_hi.astype(F32)).astype(BF16)
    logits = _dot(h_hi, router_ref[...]) + _dot(h_lo, router_ref[...])
    if mode == "experts":
        h_ref[...] = h_hi
        lane, i1, i2, w1, w2 = _route(logits + pltpu.roll(logits, LANES - N_EXPERTS, axis=1))
        route_ref[...] = jnp.where(lane == i1, w1, jnp.where(lane == i2, w2, 0.0))
    else:
        _sort_tile_by_expert(logits, h_hi, tri_ref, h_ref, route_ref, meta_ref)


def _post_mix(x, oa, ob, oc, w_out, g_mem, wq, mk, mv, mem_base, wo, g_ffn, tail, mode):
    n, t, _ = x.shape
    tm = min(t, 512)
    tile = lambda w: pl.BlockSpec((None, tm, w), lambda b, i: (b, i, 0))
    mem = pl.BlockSpec((None,) + mk.shape[1:], lambda b, i: (mem_base + b, 0, 0))
    sq = _const_spec((D_MODEL, D_MODEL))
    vec = _const_spec((1, D_MODEL))
    in_specs = [tile(D_MODEL), tile(D_A), tile(D_B), tile(C_CHANNELS), sq, vec, sq, mem, mem, sq, vec]
    args = [x, oa, ob, oc, w_out, g_mem, wq, mk, mv, wo, g_ffn]
    out_specs = [tile(D_MODEL)]
    out_shape = [jax.ShapeDtypeStruct((n, t, D_MODEL), F32)]
    if mode == "dense":
        in_specs += [_const_spec((D_MODEL, D_FF)), _const_spec((D_MODEL, D_FF)), _const_spec((D_FF, D_MODEL))]
        args += list(tail)
        return pl.pallas_call(
            functools.partial(_post_mix_kernel, mode=mode),
            grid=(n, t // tm),
            in_specs=in_specs,
            out_specs=out_specs[0],
            out_shape=out_shape[0],
            compiler_params=_params(("parallel", "parallel")),
            name="post_mix_ffn",
        )(*args)
    if mode == "dense_split":
        return pl.pallas_call(
            functools.partial(_post_mix_kernel, mode=mode),
            grid=(n, t // tm),
            in_specs=in_specs,
            out_specs=[tile(D_MODEL), tile(D_MODEL)],
            out_shape=[out_shape[0], jax.ShapeDtypeStruct((n, t, D_MODEL), BF16)],
            compiler_params=_params(("parallel", "parallel")),
            name="post_mix",
        )(*args)
    in_specs.append(_const_spec((D_MODEL, LANES)))
    args.append(tail)
    if mode == "grouped":
        assert tm == MOE_TILE
        in_specs.append(_const_spec((tm, tm)))
        args.append(jnp.triu(jnp.ones((tm, tm), BF16), k=1))
        out_specs.append(pl.BlockSpec((None, MOE_TILE_CAP, D_MODEL), lambda b, i: (b, i, 0)))
        out_shape.append(jax.ShapeDtypeStruct((n, t // tm * MOE_TILE_CAP, D_MODEL), BF16))
    else:
        out_specs.append(tile(D_MODEL))
        out_shape.append(jax.ShapeDtypeStruct((n, t, D_MODEL), BF16))
    out_specs.append(tile(LANES))
    out_shape.append(jax.ShapeDtypeStruct((n, t, LANES), F32))
    if mode == "grouped":
        out_specs.append(pl.BlockSpec((None, N_EXPERTS, LANES), lambda b, i: (b, i, 0)))
        out_shape.append(jax.ShapeDtypeStruct((n, t // tm * N_EXPERTS, LANES), F32))
    return pl.pallas_call(
        functools.partial(_post_mix_kernel, mode=mode),
        grid=(n, t // tm),
        in_specs=in_specs,
        out_specs=out_specs,
        out_shape=out_shape,
        compiler_params=_params(("parallel", "parallel")),
        name="post_mix",
    )(*args)


def _ffn_kernel(x_ref, h_ref, wg_ref, wu_ref, wd_ref, o_ref):
    h = h_ref[...]
    acc = x_ref[...]
    for c in range(D_FF // FF_CHUNK):
        cols = slice(c * FF_CHUNK, (c + 1) * FF_CHUNK)
        g = _dot(h, wg_ref[:, cols])
        u = _dot(h, wu_ref[:, cols])
        acc = acc + _dot((g * jax.nn.sigmoid(g) * u).astype(BF16), wd_ref[cols, :])
    o_ref[...] = acc


def _ffn(x2d, h2d, wg, wu, wd):
    m = x2d.shape[0]
    tm = min(m, 512)
    row = pl.BlockSpec((tm, D_MODEL), lambda i: (i, 0))
    return pl.pallas_call(
        _ffn_kernel,
        grid=(m // tm,),
        in_specs=[row, row, _const_spec((D_MODEL, D_FF)), _const_spec((D_MODEL, D_FF)),
                  _const_spec((D_FF, D_MODEL))],
        out_specs=row,
        out_shape=jax.ShapeDtypeStruct((m, D_MODEL), F32),
        compiler_params=_params(("parallel",)),
        name="ffn_dense",
    )(x2d, h2d, wg, wu, wd)


def _moe_kernel(x_ref, h_ref, comb_ref, wg_ref, wu_ref, wd_ref, gfin_ref, o_ref, acc):
    e = pl.program_id(1)

    @pl.when(e == 0)
    def _():
        acc[...] = x_ref[...]

    h = h_ref[...]
    g = _dot(h, wg_ref[...])
    u = _dot(h, wu_ref[...])
    y = _dot((g * jax.nn.sigmoid(g) * u).astype(BF16), wd_ref[...])
    comb = comb_ref[...]
    lane = lax.broadcasted_iota(jnp.int32, comb.shape, 1)
    ce = jnp.sum(jnp.where(lane == e, comb, 0.0), axis=-1, keepdims=True)
    acc[...] += ce * y

    @pl.when(e == N_EXPERTS - 1)
    def _():
        o_ref[...] = _rms(acc[...], gfin_ref[...])


def _moe(x2d, h2d, comb2d, wg, wu, wd, g_fin):
    m = x2d.shape[0]
    tm = min(m, 512)
    row = lambda w: pl.BlockSpec((tm, w), lambda i, e: (i, 0))
    return pl.pallas_call(
        _moe_kernel,
        grid=(m // tm, N_EXPERTS),
        in_specs=[row(D_MODEL), row(D_MODEL), row(LANES),
                  pl.BlockSpec((None, D_MODEL, D_FF_EXPERT), lambda i, e: (e, 0, 0)),
                  pl.BlockSpec((None, D_MODEL, D_FF_EXPERT), lambda i, e: (e, 0, 0)),
                  pl.BlockSpec((None, D_FF_EXPERT, D_MODEL), lambda i, e: (e, 0, 0)),
                  _const_spec((1, D_MODEL))],
        out_specs=row(D_MODEL),
        out_shape=jax.ShapeDtypeStruct((m, D_MODEL), F32),
        scratch_shapes=[pltpu.VMEM((tm, D_MODEL), F32)],
        compiler_params=_params(("parallel", "arbitrary")),
        name="moe_experts",
    )(x2d, h2d, comb2d, wg, wu, wd, g_fin)


def _moe_plan(meta, n_tiles):
    cpb = MOE_BLOCK // MOE_CHUNK
    tile_chunks = MOE_TILE_CAP // MOE_CHUNK
    nb = -(-(n_tiles * tile_chunks + N_EXPERTS * (cpb - 1)) // cpb)
    first = meta[:, :, 0].astype(jnp.int32)
    cnt = meta[:, :, 1].astype(jnp.int32)
    tot = jnp.sum(cnt, axis=0)
    eend = jnp.cumsum((tot + cpb - 1) // cpb * cpb)
    q = jnp.arange(nb * cpb, dtype=jnp.int32)
    e_q = jnp.minimum(jnp.sum((q[:, None] >= eend[None, :]).astype(jnp.int32), axis=1), N_EXPERTS - 1)
    oh_e = (e_q[:, None] == jnp.arange(N_EXPERTS, dtype=jnp.int32)[None, :]).astype(jnp.int32)
    pick = lambda table: jnp.sum(oh_e[:, :, None] * table.T[None, :, :], axis=1)
    j = q - jnp.sum(oh_e * (eend - (tot + cpb - 1) // cpb * cpb)[None, :], axis=1)
    valid = (j < jnp.sum(oh_e * tot[None, :], axis=1)) & (q < eend[N_EXPERTS - 1])
    incl = jnp.cumsum(cnt, axis=0)
    tile_q = jnp.minimum(jnp.sum((j[:, None] >= pick(incl)).astype(jnp.int32), axis=1), n_tiles - 1)
    oh_t = (tile_q[:, None] == jnp.arange(n_tiles, dtype=jnp.int32)[None, :]).astype(jnp.int32)
    chunk = (tile_q * tile_chunks + jnp.sum(oh_t * pick(first), axis=1)
             + j - jnp.sum(oh_t * pick(incl - cnt), axis=1))
    k = jnp.cumsum(jnp.logical_not(valid).astype(jnp.int32)) - 1
    used = jnp.sum(cnt, axis=1)
    free_incl = jnp.cumsum(tile_chunks - used)
    tile_k = jnp.minimum(jnp.sum((k[:, None] >= free_incl[None, :]).astype(jnp.int32), axis=1), n_tiles - 1)
    oh_k = (tile_k[:, None] == jnp.arange(n_tiles, dtype=jnp.int32)[None, :]).astype(jnp.int32)
    in_tile = (tile_k * tile_chunks + jnp.sum(oh_k * used[None, :], axis=1)
               + k - jnp.sum(oh_k * (free_incl - (tile_chunks - used))[None, :], axis=1))
    n_free = free_incl[n_tiles - 1]
    spare = jnp.where(k < n_free, in_tile, n_tiles * tile_chunks + k - n_free)
    src = jnp.where(valid, chunk, 0)
    dst = jnp.where(valid, chunk, spare)
    blk_expert = e_q[::cpb]
    n_active = (eend[N_EXPERTS - 1] // cpb).reshape(1)
    return blk_expert, n_active, src.reshape(nb, 1, cpb), dst.reshape(nb, 1, cpb)


def _moe_grouped_kernel(be_ref, nact_ref, src_ref, src_next_ref, dst_ref, hs_hbm, wg_ref, wu_ref, wd_ref,
                        y_hbm, xbuf, ybuf, sem_in, sem_out):
    b = pl.program_id(0)
    nb = pl.num_programs(0)
    slot = b % 2
    cpb = MOE_BLOCK // MOE_CHUNK

    def chunk_rows(i):
        return pl.ds(pl.multiple_of(i * MOE_CHUNK, MOE_CHUNK), MOE_CHUNK)

    def gather(idx_ref, s):
        for c in range(cpb):
            pltpu.make_async_copy(hs_hbm.at[chunk_rows(idx_ref[0, c]), :],
                                  xbuf.at[s, c * MOE_CHUNK:(c + 1) * MOE_CHUNK, :],
                                  sem_in.at[s]).start(priority=c % 2)

    def wait_gather(s):
        pltpu.make_async_copy(hs_hbm.at[pl.ds(0, MOE_BLOCK), :], xbuf.at[s], sem_in.at[s]).wait()

    def wait_scatter(s):
        pltpu.make_async_copy(ybuf.at[s], y_hbm.at[pl.ds(0, MOE_BLOCK), :], sem_out.at[s]).wait()

    @pl.when(b == 0)
    def _():
        gather(src_ref, 0)

    @pl.when(b + 1 < nb)
    def _():
        gather(src_next_ref, 1 - slot)

    wait_gather(slot)

    @pl.when(b >= 2)
    def _():
        wait_scatter(slot)

    @pl.when(b < nact_ref[0])
    def _():
        x = xbuf[slot]
        g = _dot(x, wg_ref[...])
        u = _dot(x, wu_ref[...])
        ybuf[slot] = _dot((g * jax.nn.sigmoid(g) * u).astype(BF16), wd_ref[...]).astype(BF16)

    @pl.when(b >= nact_ref[0])
    def _():
        ybuf[slot] = jnp.zeros((MOE_BLOCK, D_MODEL), BF16)

    for c in range(cpb):
        pltpu.make_async_copy(ybuf.at[slot, c * MOE_CHUNK:(c + 1) * MOE_CHUNK, :],
                              y_hbm.at[chunk_rows(dst_ref[0, c]), :],
                              sem_out.at[slot]).start(priority=c % 2)

    @pl.when(b == nb - 1)
    def _():
        wait_scatter(slot)
        wait_scatter(1 - slot)


def _moe_grouped(hs, blk_expert, n_active, src, dst, wg, wu, wd):
    nb, _, cpb = src.shape
    assert nb >= 2
    idx_spec = lambda f: pl.BlockSpec((None, 1, cpb), f, memory_space=pltpu.SMEM)
    wspec = lambda shape: pl.BlockSpec((None,) + shape, lambda b, be, na: (be[b], 0, 0))
    return pl.pallas_call(
        _moe_grouped_kernel,
        grid_spec=pltpu.PrefetchScalarGridSpec(
            num_scalar_prefetch=2,
            grid=(nb,),
            in_specs=[idx_spec(lambda b, be, na: (b, 0, 0)),
                      idx_spec(lambda b, be, na: (jnp.minimum(b + 1, nb - 1), 0, 0)),
                      idx_spec(lambda b, be, na: (b, 0, 0)),
                      pl.BlockSpec(memory_space=pl.ANY),
                      wspec((D_MODEL, D_FF_EXPERT)), wspec((D_MODEL, D_FF_EXPERT)),
                      wspec((D_FF_EXPERT, D_MODEL))],
            out_specs=pl.BlockSpec(memory_space=pl.ANY),
            scratch_shapes=[pltpu.VMEM((2, MOE_BLOCK, D_MODEL), BF16), pltpu.VMEM((2, MOE_BLOCK, D_MODEL), BF16),
                            pltpu.SemaphoreType.DMA((2,)), pltpu.SemaphoreType.DMA((2,))]),
        out_shape=jax.ShapeDtypeStruct((nb * MOE_BLOCK, D_MODEL), BF16),
        compiler_params=_params(("arbitrary",)),
        name="moe_grouped",
    )(blk_expert, n_active, src, src, dst, hs, wg, wu, wd)


def _moe_combine_kernel(x_ref, route_ref, y_ref, gfin_ref, o_ref):
    tm = x_ref.shape[0]
    route = route_ref[...]
    d1, d2, w1, w2 = route[:, 0:1], route[:, 1:2], route[:, 2:3], route[:, 3:4]
    col = lax.broadcasted_iota(jnp.int32, (tm, MOE_TILE_CAP), 1).astype(F32)
    wm = jnp.where(col == d1, w1, jnp.where(col == d2, w2, 0.0)).astype(BF16)
    acc = x_ref[...] + _dot(wm, y_ref[...])
    o_ref[...] = _rms(acc, gfin_ref[...])


def _moe_combine(x2d, route2d, ys, g_fin):
    m = x2d.shape[0]
    tm = MOE_TILE
    return pl.pallas_call(
        _moe_combine_kernel,
        grid=(m // tm,),
        in_specs=[pl.BlockSpec((tm, D_MODEL), lambda i: (i, 0)),
                  pl.BlockSpec((tm, LANES), lambda i: (i, 0)),
                  pl.BlockSpec((MOE_TILE_CAP, D_MODEL), lambda i: (i, 0)),
                  _const_spec((1, D_MODEL))],
        out_specs=pl.BlockSpec((tm, D_MODEL), lambda i: (i, 0)),
        out_shape=jax.ShapeDtypeStruct((m, D_MODEL), F32),
        compiler_params=_params(("parallel",)),
        name="moe_combine",
    )(x2d, route2d, ys, g_fin)


def _layer(x, pos0, attn_hist, ret_state, conv_state, mk, mv, mem_base, w):
    n, t, _ = x.shape
    conv_w = (w["conv_w"], w["conv_b"], w["conv_ln_g"], w["conv_ln_b"])
    keep = min(A_REACH, t)
    if conv_state is None:
        aq, akv, bqk, bv, bg, oc, new_conv, k_keep, v_keep = _in_proj_conv(
            x, w["norm_mix_g"], w["w_in"], *conv_w)
    else:
        aq, akv, bqk, bv, bg, cab = _in_proj(x.reshape(n * t, D_MODEL), w["norm_mix_g"], w["w_in"])
        r3 = lambda a: a.reshape(n, t, a.shape[-1])
        aq, akv, bqk, bv, bg, cab = map(r3, (aq, akv, bqk, bv, bg, cab))
        oc, new_conv = _conv_module(cab, conv_state, *conv_w)
        k_keep, v_keep = akv[:, t - keep:, :D_A], akv[:, t - keep:, D_A:]
    oa = _band_attn(aq, akv, attn_hist, w["rel_bias"])
    ob, new_s = _retention(bqk, bv, bg, ret_state, w["ret_gn_g"], pos0)
    m = n * t
    if "router" not in w:
        mode = "dense" if t >= MOE_TILE else "dense_split"
    elif m >= N_EXPERTS * MOE_BLOCK and t % MOE_TILE == 0:
        mode = "grouped"
    else:
        mode = "experts"
    ffn_w = (w.get("ffn_g"), w.get("ffn_u"), w.get("ffn_d"))
    tail = ffn_w if mode == "dense" else w.get("router")
    post = _post_mix(x, oa, ob, oc, w["w_out"], w["norm_mem_g"], w["wx_q"], mk, mv, mem_base, w["wx_o"],
                     w["norm_ffn_g"], tail, mode)
    if mode == "dense":
        x3 = post
    elif mode == "dense_split":
        x3 = _ffn(post[0].reshape(m, D_MODEL), post[1].reshape(m, D_MODEL), *ffn_w)
    elif mode == "experts":
        x2 = post[0].reshape(m, D_MODEL)
        x3 = _moe(x2, post[1].reshape(m, D_MODEL), post[2].reshape(m, LANES),
                  w["moe_g"], w["moe_u"], w["moe_d"], w["final_g"])
    else:
        x2 = post[0].reshape(m, D_MODEL)
        n_tiles = m // MOE_TILE
        blk_expert, n_active, src, dst = _moe_plan(post[3].reshape(n_tiles, N_EXPERTS, LANES), n_tiles)
        ys = _moe_grouped(post[1].reshape(n_tiles * MOE_TILE_CAP, D_MODEL), blk_expert, n_active, src, dst,
                          w["moe_g"], w["moe_u"], w["moe_d"])
        x3 = _moe_combine(x2, post[2].reshape(m, LANES), ys, w["final_g"])
    new_k = k_keep.reshape(n, keep, A_HEADS, A_HEAD_DIM)
    new_v = v_keep.reshape(n, keep, A_HEADS, A_HEAD_DIM)
    return x3.reshape(n, t, D_MODEL), new_k, new_v, new_s, new_conv


def kernel(x_prompt, x_sample, cache_attn_k, cache_attn_v, state_ret, state_conv, cache_mem_k, cache_mem_v,
           mem_prompt, norm_mix_g, w_in, rel_bias, ret_gn_g, conv_w, conv_b, conv_ln_g, conv_ln_b, w_out,
           norm_mem_g, wx_q, wx_k, wx_v, wx_o, norm_ffn_g, ffn_w_gate, ffn_w_up, ffn_w_down,
           router_w, moe_w_gate, moe_w_up, moe_w_down, final_norm_g):
    depth = w_in.shape[0]
    assert depth == 2, "layer 0 dense FFN, layer 1 experts + closing norm"
    n_p, _, _ = x_prompt.shape
    n_s, t_s, _ = x_sample.shape
    n_mem = mem_prompt.shape[1]
    row = lambda a: a.reshape(1, -1).astype(F32)
    xp, xs = x_prompt, x_sample
    mem2d = mem_prompt.reshape(n_p * n_mem, D_MODEL)
    outs_p = [[] for _ in range(4)]
    outs_s = [[] for _ in range(4)]
    mk_all, mv_all, mk_heads, mv_heads = _mem_kv(mem2d, wx_k.astype(BF16), wx_v.astype(BF16), n_mem)
    mk_p = mk_all.reshape(depth * n_p, n_mem, D_MODEL)
    mv_p = mv_all.reshape(depth * n_p, n_mem, D_MODEL)
    mk_s = cache_mem_k.reshape(depth * n_s, n_mem, D_MODEL)
    mv_s = cache_mem_v.reshape(depth * n_s, n_mem, D_MODEL)
    for l in range(depth):
        w = dict(norm_mix_g=row(norm_mix_g[l]), w_in=w_in[l].astype(BF16), rel_bias=rel_bias[l],
                 ret_gn_g=row(ret_gn_g[l]), conv_w=conv_w[l].astype(F32), conv_b=row(conv_b[l]),
                 conv_ln_g=row(conv_ln_g[l]), conv_ln_b=row(conv_ln_b[l]), w_out=w_out[l].astype(BF16),
                 norm_mem_g=row(norm_mem_g[l]), wx_q=wx_q[l].astype(BF16), wx_o=wx_o[l].astype(BF16),
                 norm_ffn_g=row(norm_ffn_g[l]))
        if l % 2 == 0:
            i = l // 2
            w.update(ffn_g=ffn_w_gate[i].astype(BF16), ffn_u=ffn_w_up[i].astype(BF16),
                     ffn_d=ffn_w_down[i].astype(BF16))
        else:
            i = l // 2
            r = router_w[i].astype(F32)
            r_hi = r.astype(BF16)
            r_lo = (r - r_hi.astype(F32)).astype(BF16)
            w.update(router=jnp.pad(jnp.concatenate([r_hi, r_lo], axis=1),
                                    ((0, 0), (0, LANES - 2 * N_EXPERTS))),
                     moe_g=moe_w_gate[i].astype(BF16), moe_u=moe_w_up[i].astype(BF16),
                     moe_d=moe_w_down[i].astype(BF16),
                     final_g=row(final_norm_g) if l == depth - 1 else None)
        xp, k_new, v_new, s_new, c_new = _layer(
            xp, 0.0, None, jnp.zeros((n_p, B_HEADS, B_QK_DIM, B_V_DIM), F32), None, mk_p, mv_p, l * n_p, w)
        for lst, a in zip(outs_p, (k_new, v_new, s_new, c_new)):
            lst.append(a)
        hist = (cache_attn_k[l].reshape(n_s, A_REACH, D_A), cache_attn_v[l].reshape(n_s, A_REACH, D_A))
        xs, k_new, v_new, s_new, c_new = _layer(
            xs, float(PAST_LEN), hist, state_ret[l], state_conv[l], mk_s, mv_s, l * n_s, w)
        for lst, a in zip(outs_s, (k_new, v_new, s_new, c_new)):
            lst.append(a)
    st = lambda lst: jnp.stack(lst)
    return (xp, xs, st(outs_p[0]), st(outs_p[1]), st(outs_p[2]), st(outs_p[3]), mk_heads, mv_heads,
            st(outs_s[0]), st(outs_s[1]), st(outs_s[2]), st(outs_s[3]))
```

```python
import functools

import numpy as np
import jax
import jax.numpy as jnp
from jax import lax
from jax.experimental import pallas as pl
from jax.experimental.pallas import tpu as pltpu

F32 = jnp.float32
BF16 = jnp.bfloat16

D_MODEL = 1024
PAST_LEN = 2048
CHUNK = 64
EPS = 1e-6
NEG_INF = -1e30
LANES = 128

A_HEADS = 4
A_HEAD_DIM = 64
D_A = A_HEADS * A_HEAD_DIM
A_LEFT_CHUNKS = 8
A_REACH = A_LEFT_CHUNKS * CHUNK
REL_CLIP = 128
A_SCALE = A_HEAD_DIM ** -0.5

B_HEADS = 4
B_QK_DIM = 64
B_V_DIM = 128
D_BQK = B_HEADS * B_QK_DIM
D_B = B_HEADS * B_V_DIM
RET_GAMMA_EXP0 = 5.0
ROPE_BASE = 10000.0

C_CHANNELS = 256
CONV_WIDTH = 31
CONV_PAD = 32

X_HEADS = 4
X_HEAD_DIM = D_MODEL // X_HEADS

D_FF = 11 * D_MODEL // 4
FF_CHUNK = 256
N_EXPERTS = 8
TOP_K = 2
D_FF_EXPERT = D_FF // 2
MOE_TILE = 512
MOE_CHUNK = 16
MOE_TILE_CAP = 1152
MOE_BLOCK = 512

COL_QA, COL_KVA, COL_QKB, COL_VB, COL_GB, COL_CAB, D_IN = 0, 256, 768, 1280, 1792, 2304, 2816

VMEM_LIMIT = 56 * 1024 * 1024


def _params(sem):
    return pltpu.CompilerParams(dimension_semantics=sem, vmem_limit_bytes=VMEM_LIMIT)


def _rms(x, g):
    return x * lax.rsqrt(jnp.mean(x * x, axis=-1, keepdims=True) + EPS) * g


def _dot(a, b):
    return jnp.dot(a, b, preferred_element_type=F32)


def _dot_nt(a, b):
    return lax.dot_general(a, b, (((1,), (1,)), ((), ())), preferred_element_type=F32)


def _dot_tn(a, b):
    return lax.dot_general(a, b, (((0,), (0,)), ((), ())), preferred_element_type=F32)


def _const_spec(shape):
    return pl.BlockSpec(shape, lambda *_: (0,) * len(shape), pipeline_mode=pl.Buffered(1))


def _in_proj_kernel(x_ref, g_ref, w_ref, aq_ref, akv_ref, bqk_ref, bv_ref, bg_ref, cab_ref):
    h = _rms(x_ref[...], g_ref[...]).astype(BF16)
    aq_ref[...] = _dot(h, w_ref[:, COL_QA:COL_KVA]).astype(BF16)
    akv_ref[...] = _dot(h, w_ref[:, COL_KVA:COL_QKB])
    bqk_ref[...] = _dot(h, w_ref[:, COL_QKB:COL_VB]).astype(BF16)
    bv_ref[...] = _dot(h, w_ref[:, COL_VB:COL_GB]).astype(BF16)
    bg_ref[...] = _dot(h, w_ref[:, COL_GB:COL_CAB]).astype(BF16)
    cab_ref[...] = _dot(h, w_ref[:, COL_CAB:D_IN])


def _conv_tile(ext, w_ref, b_ref, lng_ref, lnb_ref, tt):
    hist = CONV_WIDTH - 1
    acc = jnp.zeros((tt, C_CHANNELS), F32)
    for b in range(8):
        rb = ext if b == 0 else pltpu.roll(ext, b, axis=0)
        for a in range(CONV_PAD // 8):
            d = 8 * a + b
            if d > hist:
                continue
            k = hist - d
            acc = acc + w_ref[k:k + 1, :] * rb[CONV_PAD - 8 * a:CONV_PAD - 8 * a + tt, :]
    y = acc + b_ref[...]
    mu = jnp.mean(y, axis=-1, keepdims=True)
    d0 = y - mu
    var = jnp.mean(d0 * d0, axis=-1, keepdims=True)
    yn = d0 * lax.rsqrt(var + EPS) * lng_ref[...] + lnb_ref[...]
    return yn * jax.nn.sigmoid(yn)


def _in_proj_conv_kernel(x_ref, g_ref, w_ref, cw_ref, cb_ref, lng_ref, lnb_ref,
                         aq_ref, akv_ref, bqk_ref, bv_ref, bg_ref, oc_ref, nc_ref, kt_ref, vt_ref, up,
                         *, tm, tt):
    hist = CONV_WIDTH - 1

    @pl.when(pl.program_id(1) == 0)
    def _():
        up[tm:tm + CONV_PAD, :] = jnp.zeros((CONV_PAD, C_CHANNELS), F32)

    h = _rms(x_ref[...], g_ref[...]).astype(BF16)
    cab = _dot(h, w_ref[:, COL_CAB:D_IN])
    up[0:CONV_PAD, :] = up[tm:tm + CONV_PAD, :]
    up[CONV_PAD:CONV_PAD + tm, :] = cab[:, 0:C_CHANNELS] * jax.nn.sigmoid(cab[:, C_CHANNELS:2 * C_CHANNELS])
    nc_ref[...] = up[CONV_PAD + tm - hist:CONV_PAD + tm, :]
    cols = ((aq_ref, COL_QA, COL_KVA), (akv_ref, COL_KVA, COL_QKB), (bqk_ref, COL_QKB, COL_VB),
            (bv_ref, COL_VB, COL_GB), (bg_ref, COL_GB, COL_CAB))
    for i in range(tm // tt):
        ext = up[i * tt:i * tt + tt + CONV_PAD, :]
        oc_ref[i * tt:(i + 1) * tt, :] = _conv_tile(ext, cw_ref, cb_ref, lng_ref, lnb_ref, tt).astype(BF16)
        for o_ref, lo, hi in cols[i::tm // tt]:
            res = _dot(h, w_ref[:, lo:hi])
            o_ref[...] = res.astype(BF16)
            if o_ref is akv_ref:
                kt_ref[...] = res[:, 0:D_A]
                vt_ref[...] = res[:, D_A:2 * D_A]


def _in_proj_conv(x, g, w_in, conv_w, conv_b, ln_g, ln_b):
    n, t, _ = x.shape
    tm = min(t, 512)
    assert tm == A_REACH and t % tm == 0
    tt = min(tm, 128)
    hist = CONV_WIDTH - 1
    widths = (D_A, 2 * D_A, 2 * D_BQK, D_B, D_B, C_CHANNELS)
    tile = lambda w: pl.BlockSpec((None, tm, w), lambda b, j: (b, j, 0))
    last = lambda rows, w: pl.BlockSpec((None, rows, w), lambda b, j: (b, 0, 0))
    return pl.pallas_call(
        functools.partial(_in_proj_conv_kernel, tm=tm, tt=tt),
        grid=(n, t // tm),
        in_specs=[tile(D_MODEL), _const_spec((1, D_MODEL)), _const_spec((D_MODEL, D_IN)),
                  _const_spec((CONV_WIDTH, C_CHANNELS))] + [_const_spec((1, C_CHANNELS))] * 3,
        out_specs=[tile(w) for w in widths] + [last(hist, C_CHANNELS), last(tm, D_A), last(tm, D_A)],
        out_shape=[jax.ShapeDtypeStruct((n, t, w), BF16) for w in widths]
                  + [jax.ShapeDtypeStruct((n, hist, C_CHANNELS), F32)]
                  + [jax.ShapeDtypeStruct((n, tm, D_A), F32)] * 2,
        scratch_shapes=[pltpu.VMEM((CONV_PAD + tm, C_CHANNELS), F32)],
        compiler_params=_params(("parallel", "arbitrary")),
        name="in_proj_conv",
    )(x, g, w_in, conv_w, conv_b, ln_g, ln_b)


def _in_proj(x2d, g, w_in):
    m = x2d.shape[0]
    tm = min(m, 512)
    widths = ((256, BF16), (512, F32), (512, BF16), (512, BF16), (512, BF16), (512, F32))
    return pl.pallas_call(
        _in_proj_kernel,
        grid=(m // tm,),
        in_specs=[pl.BlockSpec((tm, D_MODEL), lambda i: (i, 0)),
                  _const_spec((1, D_MODEL)),
                  _const_spec((D_MODEL, D_IN))],
        out_specs=[pl.BlockSpec((tm, w), lambda i: (i, 0)) for w, _ in widths],
        out_shape=[jax.ShapeDtypeStruct((m, w), dt) for w, dt in widths],
        compiler_params=_params(("parallel",)),
        name="in_proj",
    )(x2d, g, w_in)


def _band_attn_kernel(*refs, t, tq, nsub, has_hist):
    if has_hist:
        aq_ref, akv_ref, hk_ref, hv_ref, bias_ref, o_ref, kc, vc = refs
    else:
        aq_ref, akv_ref, bias_ref, o_ref, kc, vc = refs
    j = pl.program_id(1)
    span = A_REACH + tq

    @pl.when(j == 0)
    def _():
        if has_hist:
            kc[0:A_REACH, :] = hk_ref[...].astype(BF16)
            vc[0:A_REACH, :] = hv_ref[...].astype(BF16)
        else:
            kc[0:A_REACH, :] = jnp.zeros((A_REACH, D_A), BF16)
            vc[0:A_REACH, :] = jnp.zeros((A_REACH, D_A), BF16)
        kc[A_REACH:A_REACH + t, :] = akv_ref[:, 0:D_A].astype(BF16)
        vc[A_REACH:A_REACH + t, :] = akv_ref[:, D_A:2 * D_A].astype(BF16)

    lane = lax.broadcasted_iota(jnp.int32, (tq, LANES), 1)
    col = lax.broadcasted_iota(jnp.int32, (tq, span), 1)
    for sub in range(nsub):
        t0 = pl.multiple_of((j * nsub + sub) * tq, tq)
        q = aq_ref[sub * tq:(sub + 1) * tq, :]
        outs = []
        for p in range(A_HEADS // 2):
            qp = q[:, p * LANES:(p + 1) * LANES]
            kp = kc[pl.ds(t0, span), p * LANES:(p + 1) * LANES]
            vp = vc[pl.ds(t0, span), p * LANES:(p + 1) * LANES]
            o_pair = None
            for hh in range(2):
                own = (lane < A_HEAD_DIM) if hh == 0 else (lane >= A_HEAD_DIM)
                qm = jnp.where(own, qp.astype(F32) * A_SCALE, 0.0).astype(BF16)
                s = _dot_nt(qm, kp) + bias_ref[2 * p + hh]
                if not has_hist:
                    s = jnp.where(col >= A_REACH - t0, s, NEG_INF)
                m = jnp.max(s, axis=-1, keepdims=True)
                e = jnp.exp(s - m)
                l = jnp.sum(e, axis=-1, keepdims=True)
                o = _dot(e.astype(BF16), vp) / l
                o_pair = o if hh == 0 else jnp.where(own, o, o_pair)
            outs.append(o_pair)
        o_ref[sub * tq:(sub + 1) * tq, :] = jnp.concatenate(outs, axis=1).astype(BF16)


def _band_bias(rel_bias_l, tq):
    span = A_REACH + tq
    period = span + tq
    n_far = A_REACH - REL_CLIP + 1
    far = rel_bias_l[:, 2 * REL_CLIP:]
    n_near = min(2 * REL_CLIP, span - n_far)
    near = jnp.flip(rel_bias_l[:, :2 * REL_CLIP], axis=1)[:, :n_near]
    beyond = jnp.broadcast_to(rel_bias_l[:, :1], (A_HEADS, span - n_far - n_near))
    u = jnp.concatenate([jnp.broadcast_to(far, (A_HEADS, n_far)), near, beyond,
                         jnp.broadcast_to(far, (A_HEADS, tq))], axis=1).astype(F32)
    skew = jnp.broadcast_to(u[:, None, :], (A_HEADS, tq, period)).reshape(A_HEADS, tq * period)
    toeplitz = skew[:, :tq * (period - 1)].reshape(A_HEADS, tq, period - 1)[:, :, :span]
    i = np.arange(tq)[:, None]
    j = np.arange(span)[None, :]
    in_band = (j // CHUNK >= i // CHUNK) & (j // CHUNK <= i // CHUNK + A_LEFT_CHUNKS)
    return jnp.where(in_band[None], toeplitz, NEG_INF)


def _band_attn(aq, akv, hist, rel_bias_l):
    n, t, _ = aq.shape
    tq = min(t, 256)
    nsub = 8 if t % (8 * tq) == 0 else 1
    span = A_REACH + tq
    bias = _band_bias(rel_bias_l, tq)
    has_hist = hist is not None
    in_specs = [pl.BlockSpec((None, nsub * tq, D_A), lambda b, j: (b, j, 0)),
                pl.BlockSpec((None, t, 2 * D_A), lambda b, j: (b, 0, 0))]
    args = [aq, akv]
    if has_hist:
        in_specs += [pl.BlockSpec((None, A_REACH, D_A), lambda b, j: (b, 0, 0))] * 2
        args += list(hist)
    in_specs.append(_const_spec((A_HEADS, tq, span)))
    args.append(bias)
    return pl.pallas_call(
        functools.partial(_band_attn_kernel, t=t, tq=tq, nsub=nsub, has_hist=has_hist),
        grid=(n, t // (nsub * tq)),
        in_specs=in_specs,
        out_specs=pl.BlockSpec((None, nsub * tq, D_A), lambda b, j: (b, j, 0)),
        out_shape=jax.ShapeDtypeStruct((n, t, D_A), BF16),
        scratch_shapes=[pltpu.VMEM((A_REACH + t, D_A), BF16)] * 2,
        compiler_params=_params(("parallel", "arbitrary")),
        name="band_attn",
    )(*args)


def _swap_halves(x):
    lane = lax.broadcasted_iota(jnp.int32, x.shape, 1)
    first = (lane % B_QK_DIM) < (B_QK_DIM // 2)
    return jnp.where(first, pltpu.roll(x, LANES - B_QK_DIM // 2, axis=1), pltpu.roll(x, B_QK_DIM // 2, axis=1))


def _retention_kernel(bqk_ref, bv_ref, bg_ref, cos_ref, sin_ref, dmask_ref, qdec_ref, kdec_ref, sdec_ref,
                      s0_ref, gn_ref, ob_ref, sfin_ref, st, *, t, bc):
    for h in range(B_HEADS):
        off = (h % 2) * B_QK_DIM
        st[h] = jnp.zeros((LANES, B_V_DIM), F32)
        st[h, off:off + B_QK_DIM, :] = s0_ref[h]

    lane = lax.broadcasted_iota(jnp.int32, (bc, LANES), 1)

    def chunk(c, carry):
        r0 = pl.multiple_of(c * bc, bc)
        rows = pl.ds(r0, bc)
        cs = cos_ref[rows, :]
        sn = sin_ref[rows, :]
        for p in range(B_HEADS // 2):
            qx = bqk_ref[rows, p * LANES:(p + 1) * LANES].astype(F32)
            kx = bqk_ref[rows, D_BQK + p * LANES:D_BQK + (p + 1) * LANES].astype(F32)
            qr = qx * cs + _swap_halves(qx) * sn
            kr = (kx * cs + _swap_halves(kx) * sn) * (B_QK_DIM ** -0.5)
            for hh in range(2):
                h = 2 * p + hh
                own = (lane < B_QK_DIM) if hh == 0 else (lane >= B_QK_DIM)
                qh = jnp.where(own, qr, 0.0)
                kh = jnp.where(own, kr, 0.0)
                v = bv_ref[rows, h * B_V_DIM:(h + 1) * B_V_DIM]
                att = _dot_nt(qh.astype(BF16), kh.astype(BF16)) * dmask_ref[h]
                o = (_dot(att.astype(BF16), v)
                     + _dot((qh * qdec_ref[h]).astype(BF16), st[h].astype(BF16)))
                st[h] = st[h] * sdec_ref[h] + _dot_tn((kh * kdec_ref[h]).astype(BF16), v)
                mu = jnp.mean(o, axis=-1, keepdims=True)
                d = o - mu
                var = jnp.mean(d * d, axis=-1, keepdims=True)
                on = d * lax.rsqrt(var + EPS) * gn_ref[:, h * B_V_DIM:(h + 1) * B_V_DIM]
                g = bg_ref[rows, h * B_V_DIM:(h + 1) * B_V_DIM].astype(F32)
                ob_ref[rows, h * B_V_DIM:(h + 1) * B_V_DIM] = (g * jax.nn.sigmoid(g) * on).astype(BF16)
        return carry

    lax.fori_loop(0, t // bc, chunk, 0)
    for h in range(B_HEADS):
        off = (h % 2) * B_QK_DIM
        sfin_ref[h] = st[h, off:off + B_QK_DIM, :]


def _retention_tables(t, bc, pos0):
    log_g = jnp.log(1.0 - 2.0 ** (-(RET_GAMMA_EXP0 + jnp.arange(B_HEADS, dtype=F32))))
    i = jnp.arange(bc, dtype=F32)
    diff = i[:, None] - i[None, :]
    dmask = jnp.where(diff[None] >= 0, jnp.exp(jnp.maximum(diff, 0.0)[None] * log_g[:, None, None]), 0.0)
    qdec = jnp.exp((i[None, :] + 1.0) * log_g[:, None])
    kdec = jnp.exp((bc - 1.0 - i)[None, :] * log_g[:, None])
    sdec = jnp.exp(bc * log_g)
    qdec = jnp.broadcast_to(qdec[:, :, None], (B_HEADS, bc, LANES))
    kdec = jnp.broadcast_to(kdec[:, :, None], (B_HEADS, bc, LANES))
    sdec = jnp.broadcast_to(sdec[:, None, None], (B_HEADS, 1, B_V_DIM))
    half = B_QK_DIM // 2
    pos = pos0 + jnp.arange(t, dtype=F32)
    inv_freq = ROPE_BASE ** (-jnp.arange(half, dtype=F32) / half)
    ang = pos[:, None] * inv_freq[None, :]
    cos = jnp.tile(jnp.cos(ang), (1, LANES // half))
    sin = jnp.tile(jnp.concatenate([-jnp.sin(ang), jnp.sin(ang)], axis=1), (1, LANES // B_QK_DIM))
    return cos, sin, dmask, qdec, kdec, sdec


def _retention(bqk, bv, bg, s0, gn_g, pos0):
    n, t, _ = bqk.shape
    bc = min(t, 256)
    tables = _retention_tables(t, bc, pos0)
    seq = lambda w: pl.BlockSpec((None, t, w), lambda b: (b, 0, 0))
    state = pl.BlockSpec((None, B_HEADS, B_QK_DIM, B_V_DIM), lambda b: (b, 0, 0, 0))
    return pl.pallas_call(
        functools.partial(_retention_kernel, t=t, bc=bc),
        grid=(n,),
        in_specs=[seq(2 * D_BQK), seq(D_B), seq(D_B)] + [_const_spec(tb.shape) for tb in tables]
                 + [state, _const_spec((1, D_B))],
        out_specs=[seq(D_B), state],
        out_shape=[jax.ShapeDtypeStruct((n, t, D_B), BF16),
                   jax.ShapeDtypeStruct((n, B_HEADS, B_QK_DIM, B_V_DIM), F32)],
        scratch_shapes=[pltpu.VMEM((B_HEADS, LANES, B_V_DIM), F32)],
        compiler_params=_params(("parallel",)),
        name="retention",
    )(bqk, bv, bg, *tables, s0, gn_g)


def _conv_kernel(cab_ref, buf_ref, w_ref, b_ref, lng_ref, lnb_ref, oc_ref, nc_ref, up, *, t, tt):
    hist = CONV_WIDTH - 1
    up[0:CONV_PAD, :] = jnp.zeros((CONV_PAD, C_CHANNELS), F32)
    up[CONV_PAD - hist:CONV_PAD, :] = buf_ref[...]

    def glu(i, carry):
        rows = pl.ds(pl.multiple_of(i * tt, tt), tt)
        ca = cab_ref[rows, 0:C_CHANNELS]
        cb = cab_ref[rows, C_CHANNELS:2 * C_CHANNELS]
        up[pl.ds(pl.multiple_of(CONV_PAD + i * tt, 8), tt), :] = ca * jax.nn.sigmoid(cb)
        return carry

    lax.fori_loop(0, t // tt, glu, 0)

    def tile(i, carry):
        t0 = pl.multiple_of(i * tt, tt)
        ext = up[pl.ds(t0, tt + CONV_PAD), :]
        oc_ref[pl.ds(t0, tt), :] = _conv_tile(ext, w_ref, b_ref, lng_ref, lnb_ref, tt).astype(BF16)
        return carry

    lax.fori_loop(0, t // tt, tile, 0)
    nc_ref[...] = up[CONV_PAD + t - hist:CONV_PAD + t, :]


def _conv_module(cab, buf, conv_w, conv_b, ln_g, ln_b):
    n, t, _ = cab.shape
    tt = min(t, 128)
    hist = CONV_WIDTH - 1
    return pl.pallas_call(
        functools.partial(_conv_kernel, t=t, tt=tt),
        grid=(n,),
        in_specs=[pl.BlockSpec((None, t, 2 * C_CHANNELS), lambda b: (b, 0, 0)),
                  pl.BlockSpec((None, hist, C_CHANNELS), lambda b: (b, 0, 0)),
                  _const_spec((CONV_WIDTH, C_CHANNELS))] + [_const_spec((1, C_CHANNELS))] * 3,
        out_specs=[pl.BlockSpec((None, t, C_CHANNELS), lambda b: (b, 0, 0)),
                   pl.BlockSpec((None, hist, C_CHANNELS), lambda b: (b, 0, 0))],
        out_shape=[jax.ShapeDtypeStruct((n, t, C_CHANNELS), BF16),
                   jax.ShapeDtypeStruct((n, hist, C_CHANNELS), F32)],
        scratch_shapes=[pltpu.VMEM((CONV_PAD + t, C_CHANNELS), F32)],
        compiler_params=_params(("parallel",)),
        name="conv_module",
    )(cab, buf, conv_w, conv_b, ln_g, ln_b)


def _mem_kv_kernel(mem_ref, wk_ref, wv_ref, mk_ref, mv_ref, mkh_ref, mvh_ref):
    m = mem_ref[...].astype(BF16)
    n_mem = mkh_ref.shape[1]
    for w_ref, rows_ref, heads_ref in ((wk_ref, mk_ref, mkh_ref), (wv_ref, mv_ref, mvh_ref)):
        res = _dot(m, w_ref[...])
        rows_ref[...] = res.astype(BF16)
        for b in range(heads_ref.shape[0]):
            for h in range(X_HEADS):
                heads_ref[b, :, h, :] = res[b * n_mem:(b + 1) * n_mem, h * X_HEAD_DIM:(h + 1) * X_HEAD_DIM]


def _mem_kv(mem2d, wk, wv, n_mem):
    depth = wk.shape[0]
    m = mem2d.shape[0]
    tm = 512
    nb = tm // n_mem
    wspec = pl.BlockSpec((None, D_MODEL, D_MODEL), lambda l, i: (l, 0, 0))
    ospec = pl.BlockSpec((None, tm, D_MODEL), lambda l, i: (l, i, 0))
    hspec = pl.BlockSpec((None, nb, n_mem, X_HEADS, X_HEAD_DIM), lambda l, i: (l, i, 0, 0, 0))
    heads = jax.ShapeDtypeStruct((depth, m // n_mem, n_mem, X_HEADS, X_HEAD_DIM), F32)
    return pl.pallas_call(
        _mem_kv_kernel,
        grid=(depth, m // tm),
        in_specs=[pl.BlockSpec((tm, D_MODEL), lambda l, i: (i, 0)), wspec, wspec],
        out_specs=[ospec, ospec, hspec, hspec],
        out_shape=[jax.ShapeDtypeStruct((depth, m, D_MODEL), BF16)] * 2 + [heads, heads],
        compiler_params=_params(("parallel", "parallel")),
        name="mem_kv",
    )(mem2d, wk, wv)


def _route(logits):
    lane = lax.broadcasted_iota(jnp.int32, logits.shape, 1).astype(F32)
    valid = lane < N_EXPERTS
    lg = jnp.where(valid, logits, NEG_INF)
    e = jnp.exp(lg - jnp.max(lg, axis=-1, keepdims=True))
    probs = e / jnp.sum(e, axis=-1, keepdims=True)
    p1 = jnp.max(probs, axis=-1, keepdims=True)
    i1 = jnp.min(jnp.where(probs == p1, lane, float(LANES)), axis=-1, keepdims=True)
    rest = jnp.where(lane == i1, -1.0, probs)
    rest = jnp.where(valid, rest, -1.0)
    p2 = jnp.max(rest, axis=-1, keepdims=True)
    i2 = jnp.min(jnp.where(rest == p2, lane, float(LANES)), axis=-1, keepdims=True)
    tot = p1 + p2
    return lane, i1, i2, p1 / tot, p2 / tot


def _sort_tile_by_expert(logits, h_hi, tri_ref, hs_ref, route_ref, meta_ref):
    tm = logits.shape[0]
    lt = jnp.transpose(logits)
    lt = lt[0:N_EXPERTS, :] + lt[N_EXPERTS:2 * N_EXPERTS, :]
    sub = lax.broadcasted_iota(jnp.int32, (N_EXPERTS, tm), 0).astype(F32)
    ex = jnp.exp(lt - jnp.max(lt, axis=0, keepdims=True))
    probs = ex / jnp.sum(ex, axis=0, keepdims=True)
    p1 = jnp.max(probs, axis=0, keepdims=True)
    i1 = jnp.min(jnp.where(probs == p1, sub, float(N_EXPERTS)), axis=0, keepdims=True)
    rest = jnp.where(sub == i1, -1.0, probs)
    p2 = jnp.max(rest, axis=0, keepdims=True)
    i2 = jnp.min(jnp.where(rest == p2, sub, float(N_EXPERTS)), axis=0, keepdims=True)
    w1 = p1 / (p1 + p2)
    w2 = p2 / (p1 + p2)
    oh1 = jnp.where(sub == i1, 1.0, 0.0)
    oh2 = jnp.where(sub == i2, 1.0, 0.0)
    before1 = _dot(oh1.astype(BF16), tri_ref[...])
    before2 = _dot(oh2.astype(BF16), tri_ref[...])
    cnt1 = jnp.sum(oh1, axis=1, keepdims=True)
    cnt2 = jnp.sum(oh2, axis=1, keepdims=True)
    padded = jnp.floor((cnt1 + cnt2 + (MOE_CHUNK - 1.0)) * (1.0 / MOE_CHUNK)) * MOE_CHUNK
    run = jnp.broadcast_to(padded, (N_EXPERTS, LANES))
    sub8 = lax.broadcasted_iota(jnp.int32, (N_EXPERTS, LANES), 0)
    incl = run
    for k in (1, 2, 4):
        incl = incl + jnp.where(sub8 >= k, pltpu.roll(incl, k, axis=0), 0.0)
    start = (incl - run)[:, 0:1]
    d1 = jnp.sum(oh1 * (start + before1), axis=0, keepdims=True)
    d2 = jnp.sum(oh2 * (start + cnt1 + before2), axis=0, keepdims=True)
    row = lax.broadcasted_iota(jnp.int32, (MOE_TILE_CAP, tm), 0).astype(F32)
    perm = jnp.where(row == d1, 1.0, jnp.where(row == d2, 1.0, 0.0)).astype(BF16)
    hs_ref[...] = _dot(perm, h_hi).astype(BF16)
    info = jnp.concatenate([d1, d2, w1, w2, jnp.zeros((LANES - 4, tm), F32)], axis=0)
    route_ref[...] = jnp.transpose(info)
    lane8 = lax.broadcasted_iota(jnp.int32, (N_EXPERTS, LANES), 1)
    meta_ref[...] = jnp.where(lane8 == 0, start * (1.0 / MOE_CHUNK),
                              jnp.where(lane8 == 1, padded * (1.0 / MOE_CHUNK), 0.0))


def _post_mix_kernel(*refs, mode):
    (x_ref, oa_ref, ob_ref, oc_ref, wout_ref, gmem_ref, wq_ref, mk_ref, mv_ref, wo_ref, gffn_ref) = refs[:11]
    if mode == "dense":
        wg_ref, wu_ref, wd_ref, x3_ref = refs[11:]
    elif mode == "dense_split":
        x2_ref, h_ref = refs[11:]
    elif mode == "experts":
        router_ref, x2_ref, h_ref, route_ref = refs[11:]
    else:
        router_ref, tri_ref, x2_ref, h_ref, route_ref, meta_ref = refs[11:]
    y = (_dot(oa_ref[...], wout_ref[0:D_A, :])
         + _dot(ob_ref[...], wout_ref[D_A:D_A + D_B, :])
         + _dot(oc_ref[...], wout_ref[D_A + D_B:D_MODEL, :]))
    x1 = x_ref[...] + y
    q = _dot(_rms(x1, gmem_ref[...]).astype(BF16), wq_ref[...]).astype(BF16)
    outs = []
    for h in range(X_HEADS):
        cols = slice(h * X_HEAD_DIM, (h + 1) * X_HEAD_DIM)
        s = _dot_nt(q[:, cols], mk_ref[:, cols].astype(BF16)) * (X_HEAD_DIM ** -0.5)
        e = jnp.exp(s - jnp.max(s, axis=-1, keepdims=True))
        l = jnp.sum(e, axis=-1, keepdims=True)
        outs.append((_dot(e.astype(BF16), mv_ref[:, cols].astype(BF16)) / l).astype(BF16))
    x2 = x1 + _dot(jnp.concatenate(outs, axis=1), wo_ref[...])
    hf = _rms(x2, gffn_ref[...])
    h_hi = hf.astype(BF16)
    if mode == "dense":
        acc = x2
        for c in range(D_FF // FF_CHUNK):
            cols = slice(c * FF_CHUNK, (c + 1) * FF_CHUNK)
            g = _dot(h_hi, wg_ref[:, cols])
            u = _dot(h_hi, wu_ref[:, cols])
            acc = acc + _dot((g * jax.nn.sigmoid(g) * u).astype(BF16), wd_ref[cols, :])
        x3_ref[...] = acc
        return
    x2_ref[...] = x2
    if mode == "dense_split":
        h_ref[...] = h_hi
        return
    h_lo = (hf - h_hi.astype(F32)).astype(BF16)
    logits = _dot(h_hi, router_ref[...]) + _dot(h_lo, router_ref[...])
    if mode == "experts":
        h_ref[...] = h_hi
        lane, i1, i2, w1, w2 = _route(logits + pltpu.roll(logits, LANES - N_EXPERTS, axis=1))
        route_ref[...] = jnp.where(lane == i1, w1, jnp.where(lane == i2, w2, 0.0))
    else:
        _sort_tile_by_expert(logits, h_hi, tri_ref, h_ref, route_ref, meta_ref)


def _post_mix(x, oa, ob, oc, w_out, g_mem, wq, mk, mv, mem_base, wo, g_ffn, tail, mode):
    n, t, _ = x.shape
    tm = min(t, 512)
    tile = lambda w: pl.BlockSpec((None, tm, w), lambda b, i: (b, i, 0))
    mem = pl.BlockSpec((None,) + mk.shape[1:], lambda b, i: (mem_base + b, 0, 0))
    sq = _const_spec((D_MODEL, D_MODEL))
    vec = _const_spec((1, D_MODEL))
    in_specs = [tile(D_MODEL), tile(D_A), tile(D_B), tile(C_CHANNELS), sq, vec, sq, mem, mem, sq, vec]
    args = [x, oa, ob, oc, w_out, g_mem, wq, mk, mv, wo, g_ffn]
    out_specs = [tile(D_MODEL)]
    out_shape = [jax.ShapeDtypeStruct((n, t, D_MODEL), F32)]
    if mode == "dense":
        in_specs += [_const_spec((D_MODEL, D_FF)), _const_spec((D_MODEL, D_FF)), _const_spec((D_FF, D_MODEL))]
        args += list(tail)
        return pl.pallas_call(
            functools.partial(_post_mix_kernel, mode=mode),
            grid=(n, t // tm),
            in_specs=in_specs,
            out_specs=out_specs[0],
            out_shape=out_shape[0],
            compiler_params=_params(("parallel", "parallel")),
            name="post_mix_ffn",
        )(*args)
    if mode == "dense_split":
        return pl.pallas_call(
            functools.partial(_post_mix_kernel, mode=mode),
            grid=(n, t // tm),
            in_specs=in_specs,
            out_specs=[tile(D_MODEL), tile(D_MODEL)],
            out_shape=[out_shape[0], jax.ShapeDtypeStruct((n, t, D_MODEL), BF16)],
            compiler_params=_params(("parallel", "parallel")),
            name="post_mix",
        )(*args)
    in_specs.append(_const_spec((D_MODEL, LANES)))
    args.append(tail)
    if mode == "grouped":
        assert tm == MOE_TILE
        in_specs.append(_const_spec((tm, tm)))
        args.append(jnp.triu(jnp.ones((tm, tm), BF16), k=1))
        out_specs.append(pl.BlockSpec((None, MOE_TILE_CAP, D_MODEL), lambda b, i: (b, i, 0)))
        out_shape.append(jax.ShapeDtypeStruct((n, t // tm * MOE_TILE_CAP, D_MODEL), BF16))
    else:
        out_specs.append(tile(D_MODEL))
        out_shape.append(jax.ShapeDtypeStruct((n, t, D_MODEL), BF16))
    out_specs.append(tile(LANES))
    out_shape.append(jax.ShapeDtypeStruct((n, t, LANES), F32))
    if mode == "grouped":
        out_specs.append(pl.BlockSpec((None, N_EXPERTS, LANES), lambda b, i: (b, i, 0)))
        out_shape.append(jax.ShapeDtypeStruct((n, t // tm * N_EXPERTS, LANES), F32))
    return pl.pallas_call(
        functools.partial(_post_mix_kernel, mode=mode),
        grid=(n, t // tm),
        in_specs=in_specs,
        out_specs=out_specs,
        out_shape=out_shape,
        compiler_params=_params(("parallel", "parallel")),
        name="post_mix",
    )(*args)


def _ffn_kernel(x_ref, h_ref, wg_ref, wu_ref, wd_ref, o_ref):
    h = h_ref[...]
    acc = x_ref[...]
    for c in range(D_FF // FF_CHUNK):
        cols = slice(c * FF_CHUNK, (c + 1) * FF_CHUNK)
        g = _dot(h, wg_ref[:, cols])
        u = _dot(h, wu_ref[:, cols])
        acc = acc + _dot((g * jax.nn.sigmoid(g) * u).astype(BF16), wd_ref[cols, :])
    o_ref[...] = acc


def _ffn(x2d, h2d, wg, wu, wd):
    m = x2d.shape[0]
    tm = min(m, 512)
    row = pl.BlockSpec((tm, D_MODEL), lambda i: (i, 0))
    return pl.pallas_call(
        _ffn_kernel,
        grid=(m // tm,),
        in_specs=[row, row, _const_spec((D_MODEL, D_FF)), _const_spec((D_MODEL, D_FF)),
                  _const_spec((D_FF, D_MODEL))],
        out_specs=row,
        out_shape=jax.ShapeDtypeStruct((m, D_MODEL), F32),
        compiler_params=_params(("parallel",)),
        name="ffn_dense",
    )(x2d, h2d, wg, wu, wd)


def _moe_kernel(x_ref, h_ref, comb_ref, wg_ref, wu_ref, wd_ref, gfin_ref, o_ref, acc):
    e = pl.program_id(1)

    @pl.when(e == 0)
    def _():
        acc[...] = x_ref[...]

    h = h_ref[...]
    g = _dot(h, wg_ref[...])
    u = _dot(h, wu_ref[...])
    y = _dot((g * jax.nn.sigmoid(g) * u).astype(BF16), wd_ref[...])
    comb = comb_ref[...]
    lane = lax.broadcasted_iota(jnp.int32, comb.shape, 1)
    ce = jnp.sum(jnp.where(lane == e, comb, 0.0), axis=-1, keepdims=True)
    acc[...] += ce * y

    @pl.when(e == N_EXPERTS - 1)
    def _():
        o_ref[...] = _rms(acc[...], gfin_ref[...])


def _moe(x2d, h2d, comb2d, wg, wu, wd, g_fin):
    m = x2d.shape[0]
    tm = min(m, 512)
    row = lambda w: pl.BlockSpec((tm, w), lambda i, e: (i, 0))
    return pl.pallas_call(
        _moe_kernel,
        grid=(m // tm, N_EXPERTS),
        in_specs=[row(D_MODEL), row(D_MODEL), row(LANES),
                  pl.BlockSpec((None, D_MODEL, D_FF_EXPERT), lambda i, e: (e, 0, 0)),
                  pl.BlockSpec((None, D_MODEL, D_FF_EXPERT), lambda i, e: (e, 0, 0)),
                  pl.BlockSpec((None, D_FF_EXPERT, D_MODEL), lambda i, e: (e, 0, 0)),
                  _const_spec((1, D_MODEL))],
        out_specs=row(D_MODEL),
        out_shape=jax.ShapeDtypeStruct((m, D_MODEL), F32),
        scratch_shapes=[pltpu.VMEM((tm, D_MODEL), F32)],
        compiler_params=_params(("parallel", "arbitrary")),
        name="moe_experts",
    )(x2d, h2d, comb2d, wg, wu, wd, g_fin)


def _moe_plan(meta, n_tiles):
    cpb = MOE_BLOCK // MOE_CHUNK
    tile_chunks = MOE_TILE_CAP // MOE_CHUNK
    nb = -(-(n_tiles * tile_chunks + N_EXPERTS * (cpb - 1)) // cpb)
    first = meta[:, :, 0].astype(jnp.int32)
    cnt = meta[:, :, 1].astype(jnp.int32)
    tot = jnp.sum(cnt, axis=0)
    eend = jnp.cumsum((tot + cpb - 1) // cpb * cpb)
    q = jnp.arange(nb * cpb, dtype=jnp.int32)
    e_q = jnp.minimum(jnp.sum((q[:, None] >= eend[None, :]).astype(jnp.int32), axis=1), N_EXPERTS - 1)
    oh_e = (e_q[:, None] == jnp.arange(N_EXPERTS, dtype=jnp.int32)[None, :]).astype(jnp.int32)
    pick = lambda table: jnp.sum(oh_e[:, :, None] * table.T[None, :, :], axis=1)
    j = q - jnp.sum(oh_e * (eend - (tot + cpb - 1) // cpb * cpb)[None, :], axis=1)
    valid = (j < jnp.sum(oh_e * tot[None, :], axis=1)) & (q < eend[N_EXPERTS - 1])
    incl = jnp.cumsum(cnt, axis=0)
    tile_q = jnp.minimum(jnp.sum((j[:, None] >= pick(incl)).astype(jnp.int32), axis=1), n_tiles - 1)
    oh_t = (tile_q[:, None] == jnp.arange(n_tiles, dtype=jnp.int32)[None, :]).astype(jnp.int32)
    chunk = (tile_q * tile_chunks + jnp.sum(oh_t * pick(first), axis=1)
             + j - jnp.sum(oh_t * pick(incl - cnt), axis=1))
    k = jnp.cumsum(jnp.logical_not(valid).astype(jnp.int32)) - 1
    used = jnp.sum(cnt, axis=1)
    free_incl = jnp.cumsum(tile_chunks - used)
    tile_k = jnp.minimum(jnp.sum((k[:, None] >= free_incl[None, :]).astype(jnp.int32), axis=1), n_tiles - 1)
    oh_k = (tile_k[:, None] == jnp.arange(n_tiles, dtype=jnp.int32)[None, :]).astype(jnp.int32)
    in_tile = (tile_k * tile_chunks + jnp.sum(oh_k * used[None, :], axis=1)
               + k - jnp.sum(oh_k * (free_incl - (tile_chunks - used))[None, :], axis=1))
    n_free = free_incl[n_tiles - 1]
    spare = jnp.where(k < n_free, in_tile, n_tiles * tile_chunks + k - n_free)
    src = jnp.where(valid, chunk, 0)
    dst = jnp.where(valid, chunk, spare)
    blk_expert = e_q[::cpb]
    n_active = (eend[N_EXPERTS - 1] // cpb).reshape(1)
    return blk_expert, n_active, src.reshape(nb, 1, cpb), dst.reshape(nb, 1, cpb)


def _moe_grouped_kernel(be_ref, nact_ref, src_ref, src_next_ref, dst_ref, hs_hbm, wg_ref, wu_ref, wd_ref,
                        y_hbm, xbuf, ybuf, sem_in, sem_out):
    b = pl.program_id(0)
    nb = pl.num_programs(0)
    slot = b % 2
    cpb = MOE_BLOCK // MOE_CHUNK

    def chunk_rows(i):
        return pl.ds(pl.multiple_of(i * MOE_CHUNK, MOE_CHUNK), MOE_CHUNK)

    def gather(idx_ref, s):
        for c in range(cpb):
            pltpu.make_async_copy(hs_hbm.at[chunk_rows(idx_ref[0, c]), :],
                                  xbuf.at[s, c * MOE_CHUNK:(c + 1) * MOE_CHUNK, :],
                                  sem_in.at[s]).start(priority=c % 2)

    def wait_gather(s):
        pltpu.make_async_copy(hs_hbm.at[pl.ds(0, MOE_BLOCK), :], xbuf.at[s], sem_in.at[s]).wait()

    def wait_scatter(s):
        pltpu.make_async_copy(ybuf.at[s], y_hbm.at[pl.ds(0, MOE_BLOCK), :], sem_out.at[s]).wait()

    @pl.when(b == 0)
    def _():
        gather(src_ref, 0)

    @pl.when(b + 1 < nb)
    def _():
        gather(src_next_ref, 1 - slot)

    wait_gather(slot)

    @pl.when(b >= 2)
    def _():
        wait_scatter(slot)

    @pl.when(b < nact_ref[0])
    def _():
        x = xbuf[slot]
        g = _dot(x, wg_ref[...])
        u = _dot(x, wu_ref[...])
        ybuf[slot] = _dot((g * jax.nn.sigmoid(g) * u).astype(BF16), wd_ref[...]).astype(BF16)

    @pl.when(b >= nact_ref[0])
    def _():
        ybuf[slot] = jnp.zeros((MOE_BLOCK, D_MODEL), BF16)

    for c in range(cpb):
        pltpu.make_async_copy(ybuf.at[slot, c * MOE_CHUNK:(c + 1) * MOE_CHUNK, :],
                              y_hbm.at[chunk_rows(dst_ref[0, c]), :],
                              sem_out.at[slot]).start(priority=c % 2)

    @pl.when(b == nb - 1)
    def _():
        wait_scatter(slot)
        wait_scatter(1 - slot)


def _moe_grouped(hs, blk_expert, n_active, src, dst, wg, wu, wd):
    nb, _, cpb = src.shape
    assert nb >= 2
    idx_spec = lambda f: pl.BlockSpec((None, 1, cpb), f, memory_space=pltpu.SMEM)
    wspec = lambda shape: pl.BlockSpec((None,) + shape, lambda b, be, na: (be[b], 0, 0))
    return pl.pallas_call(
        _moe_grouped_kernel,
        grid_spec=pltpu.PrefetchScalarGridSpec(
            num_scalar_prefetch=2,
            grid=(nb,),
            in_specs=[idx_spec(lambda b, be, na: (b, 0, 0)),
                      idx_spec(lambda b, be, na: (jnp.minimum(b + 1, nb - 1), 0, 0)),
                      idx_spec(lambda b, be, na: (b, 0, 0)),
                      pl.BlockSpec(memory_space=pl.ANY),
                      wspec((D_MODEL, D_FF_EXPERT)), wspec((D_MODEL, D_FF_EXPERT)),
                      wspec((D_FF_EXPERT, D_MODEL))],
            out_specs=pl.BlockSpec(memory_space=pl.ANY),
            scratch_shapes=[pltpu.VMEM((2, MOE_BLOCK, D_MODEL), BF16), pltpu.VMEM((2, MOE_BLOCK, D_MODEL), BF16),
                            pltpu.SemaphoreType.DMA((2,)), pltpu.SemaphoreType.DMA((2,))]),
        out_shape=jax.ShapeDtypeStruct((nb * MOE_BLOCK, D_MODEL), BF16),
        compiler_params=_params(("arbitrary",)),
        name="moe_grouped",
    )(blk_expert, n_active, src, src, dst, hs, wg, wu, wd)


def _moe_combine_kernel(x_ref, route_ref, y_ref, gfin_ref, o_ref):
    tm = x_ref.shape[0]
    route = route_ref[...]
    d1, d2, w1, w2 = route[:, 0:1], route[:, 1:2], route[:, 2:3], route[:, 3:4]
    col = lax.broadcasted_iota(jnp.int32, (tm, MOE_TILE_CAP), 1).astype(F32)
    wm = jnp.where(col == d1, w1, jnp.where(col == d2, w2, 0.0)).astype(BF16)
    acc = x_ref[...] + _dot(wm, y_ref[...])
    o_ref[...] = _rms(acc, gfin_ref[...])


def _moe_combine(x2d, route2d, ys, g_fin):
    m = x2d.shape[0]
    tm = MOE_TILE
    return pl.pallas_call(
        _moe_combine_kernel,
        grid=(m // tm,),
        in_specs=[pl.BlockSpec((tm, D_MODEL), lambda i: (i, 0)),
                  pl.BlockSpec((tm, LANES), lambda i: (i, 0)),
                  pl.BlockSpec((MOE_TILE_CAP, D_MODEL), lambda i: (i, 0)),
                  _const_spec((1, D_MODEL))],
        out_specs=pl.BlockSpec((tm, D_MODEL), lambda i: (i, 0)),
        out_shape=jax.ShapeDtypeStruct((m, D_MODEL), F32),
        compiler_params=_params(("parallel",)),
        name="moe_combine",
    )(x2d, route2d, ys, g_fin)


def _layer(x, pos0, attn_hist, ret_state, conv_state, mk, mv, mem_base, w):
    n, t, _ = x.shape
    conv_w = (w["conv_w"], w["conv_b"], w["conv_ln_g"], w["conv_ln_b"])
    keep = min(A_REACH, t)
    if conv_state is None:
        aq, akv, bqk, bv, bg, oc, new_conv, k_keep, v_keep = _in_proj_conv(
            x, w["norm_mix_g"], w["w_in"], *conv_w)
    else:
        aq, akv, bqk, bv, bg, cab = _in_proj(x.reshape(n * t, D_MODEL), w["norm_mix_g"], w["w_in"])
        r3 = lambda a: a.reshape(n, t, a.shape[-1])
        aq, akv, bqk, bv, bg, cab = map(r3, (aq, akv, bqk, bv, bg, cab))
        oc, new_conv = _conv_module(cab, conv_state, *conv_w)
        k_keep, v_keep = akv[:, t - keep:, :D_A], akv[:, t - keep:, D_A:]
    oa = _band_attn(aq, akv, attn_hist, w["rel_bias"])
    ob, new_s = _retention(bqk, bv, bg, ret_state, w["ret_gn_g"], pos0)
    m = n * t
    if "router" not in w:
        mode = "dense" if t >= MOE_TILE else "dense_split"
    elif m >= N_EXPERTS * MOE_BLOCK and t % MOE_TILE == 0:
        mode = "grouped"
    else:
        mode = "experts"
    ffn_w = (w.get("ffn_g"), w.get("ffn_u"), w.get("ffn_d"))
    tail = ffn_w if mode == "dense" else w.get("router")
    post = _post_mix(x, oa, ob, oc, w["w_out"], w["norm_mem_g"], w["wx_q"], mk, mv, mem_base, w["wx_o"],
                     w["norm_ffn_g"], tail, mode)
    if mode == "dense":
        x3 = post
    elif mode == "dense_split":
        x3 = _ffn(post[0].reshape(m, D_MODEL), post[1].reshape(m, D_MODEL), *ffn_w)
    elif mode == "experts":
        x2 = post[0].reshape(m, D_MODEL)
        x3 = _moe(x2, post[1].reshape(m, D_MODEL), post[2].reshape(m, LANES),
                  w["moe_g"], w["moe_u"], w["moe_d"], w["final_g"])
    else:
        x2 = post[0].reshape(m, D_MODEL)
        n_tiles = m // MOE_TILE
        blk_expert, n_active, src, dst = _moe_plan(post[3].reshape(n_tiles, N_EXPERTS, LANES), n_tiles)
        ys = _moe_grouped(post[1].reshape(n_tiles * MOE_TILE_CAP, D_MODEL), blk_expert, n_active, src, dst,
                          w["moe_g"], w["moe_u"], w["moe_d"])
        x3 = _moe_combine(x2, post[2].reshape(m, LANES), ys, w["final_g"])
    new_k = k_keep.reshape(n, keep, A_HEADS, A_HEAD_DIM)
    new_v = v_keep.reshape(n, keep, A_HEADS, A_HEAD_DIM)
    return x3.reshape(n, t, D_MODEL), new_k, new_v, new_s, new_conv


def kernel(x_prompt, x_sample, cache_attn_k, cache_attn_v, state_ret, state_conv, cache_mem_k, cache_mem_v,
           mem_prompt, norm_mix_g, w_in, rel_bias, ret_gn_g, conv_w, conv_b, conv_ln_g, conv_ln_b, w_out,
           norm_mem_g, wx_q, wx_k, wx_v, wx_o, norm_ffn_g, ffn_w_gate, ffn_w_up, ffn_w_down,
           router_w, moe_w_gate, moe_w_up, moe_w_down, final_norm_g):
    depth = w_in.shape[0]
    assert depth == 2, "layer 0 dense FFN, layer 1 experts + closing norm"
    n_p, _, _ = x_prompt.shape
    n_s, t_s, _ = x_sample.shape
    n_mem = mem_prompt.shape[1]
    row = lambda a: a.reshape(1, -1).astype(F32)
    xp, xs = x_prompt, x_sample
    mem2d = mem_prompt.reshape(n_p * n_mem, D_MODEL)
    outs_p = [[] for _ in range(4)]
    outs_s = [[] for _ in range(4)]
    mk_all, mv_all, mk_heads, mv_heads = _mem_kv(mem2d, wx_k.astype(BF16), wx_v.astype(BF16), n_mem)
    mk_p = mk_all.reshape(depth * n_p, n_mem, D_MODEL)
    mv_p = mv_all.reshape(depth * n_p, n_mem, D_MODEL)
    mk_s = cache_mem_k.reshape(depth * n_s, n_mem, D_MODEL)
    mv_s = cache_mem_v.reshape(depth * n_s, n_mem, D_MODEL)
    for l in range(depth):
        w = dict(norm_mix_g=row(norm_mix_g[l]), w_in=w_in[l].astype(BF16), rel_bias=rel_bias[l],
                 ret_gn_g=row(ret_gn_g[l]), conv_w=conv_w[l].astype(F32), conv_b=row(conv_b[l]),
                 conv_ln_g=row(conv_ln_g[l]), conv_ln_b=row(conv_ln_b[l]), w_out=w_out[l].astype(BF16),
                 norm_mem_g=row(norm_mem_g[l]), wx_q=wx_q[l].astype(BF16), wx_o=wx_o[l].astype(BF16),
                 norm_ffn_g=row(norm_ffn_g[l]))
        if l % 2 == 0:
            i = l // 2
            w.update(ffn_g=ffn_w_gate[i].astype(BF16), ffn_u=ffn_w_up[i].astype(BF16),
                     ffn_d=ffn_w_down[i].astype(BF16))
        else:
            i = l // 2
            r = router_w[i].astype(F32)
            r_hi = r.astype(BF16)
            r_lo = (r - r_hi.astype(F32)).astype(BF16)
            w.update(router=jnp.pad(jnp.concatenate([r_hi, r_lo], axis=1),
                                    ((0, 0), (0, LANES - 2 * N_EXPERTS))),
                     moe_g=moe_w_gate[i].astype(BF16), moe_u=moe_w_up[i].astype(BF16),
                     moe_d=moe_w_down[i].astype(BF16),
                     final_g=row(final_norm_g) if l == depth - 1 else None)
        xp, k_new, v_new, s_new, c_new = _layer(
            xp, 0.0, None, jnp.zeros((n_p, B_HEADS, B_QK_DIM, B_V_DIM), F32), None, mk_p, mv_p, l * n_p, w)
        for lst, a in zip(outs_p, (k_new, v_new, s_new, c_new)):
            lst.append(a)
        hist = (cache_attn_k[l].reshape(n_s, A_REACH, D_A), cache_attn_v[l].reshape(n_s, A_REACH, D_A))
        xs, k_new, v_new, s_new, c_new = _layer(
            xs, float(PAST_LEN), hist, state_ret[l], state_conv[l], mk_s, mv_s, l * n_s, w)
        for lst, a in zip(outs_s, (k_new, v_new, s_new, c_new)):
            lst.append(a)
    st = lambda lst: jnp.stack(lst)
    return (xp, xs, st(outs_p[0]), st(outs_p[1]), st(outs_p[2]), st(outs_p[3]), mk_heads, mv_heads,
            st(outs_s[0]), st(outs_s[1]), st(outs_s[2]), st(outs_s[3]))
```

```python
import functools

import numpy as np
import jax
import jax.numpy as jnp
from jax import lax
from jax.experimental import pallas as pl
from jax.experimental.pallas import tpu as pltpu

F32 = jnp.float32
BF16 = jnp.bfloat16

D_MODEL = 1024
PAST_LEN = 2048
CHUNK = 64
EPS = 1e-6
NEG_INF = -1e30
LANES = 128

A_HEADS = 4
A_HEAD_DIM = 64
D_A = A_HEADS * A_HEAD_DIM
A_LEFT_CHUNKS = 8
A_REACH = A_LEFT_CHUNKS * CHUNK
REL_CLIP = 128
A_SCALE = A_HEAD_DIM ** -0.5

B_HEADS = 4
B_QK_DIM = 64
B_V_DIM = 128
D_BQK = B_HEADS * B_QK_DIM
D_B = B_HEADS * B_V_DIM
RET_GAMMA_EXP0 = 5.0
ROPE_BASE = 10000.0

C_CHANNELS = 256
CONV_WIDTH = 31
CONV_PAD = 32

X_HEADS = 4
X_HEAD_DIM = D_MODEL // X_HEADS

D_FF = 11 * D_MODEL // 4
FF_CHUNK = 256
N_EXPERTS = 8
TOP_K = 2
D_FF_EXPERT = D_FF // 2
MOE_TILE = 512
MOE_CHUNK = 16
MOE_TILE_CAP = 1152
MOE_BLOCK = 512

COL_QA, COL_KVA, COL_QKB, COL_VB, COL_GB, COL_CAB, D_IN = 0, 256, 768, 1280, 1792, 2304, 2816

VMEM_LIMIT = 56 * 1024 * 1024


def _params(sem):
    return pltpu.CompilerParams(dimension_semantics=sem, vmem_limit_bytes=VMEM_LIMIT)


def _rms(x, g):
    return x * lax.rsqrt(jnp.mean(x * x, axis=-1, keepdims=True) + EPS) * g


def _dot(a, b):
    return jnp.dot(a, b, preferred_element_type=F32)


def _dot_nt(a, b):
    return lax.dot_general(a, b, (((1,), (1,)), ((), ())), preferred_element_type=F32)


def _dot_tn(a, b):
    return lax.dot_general(a, b, (((0,), (0,)), ((), ())), preferred_element_type=F32)


def _const_spec(shape):
    return pl.BlockSpec(shape, lambda *_: (0,) * len(shape), pipeline_mode=pl.Buffered(1))


def _in_proj_kernel(x_ref, g_ref, w_ref, aq_ref, akv_ref, bqk_ref, bv_ref, bg_ref, cab_ref):
    h = _rms(x_ref[...], g_ref[...]).astype(BF16)
    aq_ref[...] = _dot(h, w_ref[:, COL_QA:COL_KVA]).astype(BF16)
    akv_ref[...] = _dot(h, w_ref[:, COL_KVA:COL_QKB])
    bqk_ref[...] = _dot(h, w_ref[:, COL_QKB:COL_VB]).astype(BF16)
    bv_ref[...] = _dot(h, w_ref[:, COL_VB:COL_GB]).astype(BF16)
    bg_ref[...] = _dot(h, w_ref[:, COL_GB:COL_CAB]).astype(BF16)
    cab_ref[...] = _dot(h, w_ref[:, COL_CAB:D_IN])


def _conv_tile(ext, w_ref, b_ref, lng_ref, lnb_ref, tt):
    hist = CONV_WIDTH - 1
    acc = jnp.zeros((tt, C_CHANNELS), F32)
    for b in range(8):
        rb = ext if b == 0 else pltpu.roll(ext, b, axis=0)
        for a in range(CONV_PAD // 8):
            d = 8 * a + b
            if d > hist:
                continue
            k = hist - d
            acc = acc + w_ref[k:k + 1, :] * rb[CONV_PAD - 8 * a:CONV_PAD - 8 * a + tt, :]
    y = acc + b_ref[...]
    mu = jnp.mean(y, axis=-1, keepdims=True)
    d0 = y - mu
    var = jnp.mean(d0 * d0, axis=-1, keepdims=True)
    yn = d0 * lax.rsqrt(var + EPS) * lng_ref[...] + lnb_ref[...]
    return yn * jax.nn.sigmoid(yn)


def _in_proj_conv_kernel(x_ref, g_ref, w_ref, cw_ref, cb_ref, lng_ref, lnb_ref,
                         aq_ref, akv_ref, bqk_ref, bv_ref, bg_ref, oc_ref, nc_ref, kt_ref, vt_ref, up,
                         *, tm, tt):
    hist = CONV_WIDTH - 1

    @pl.when(pl.program_id(1) == 0)
    def _():
        up[tm:tm + CONV_PAD, :] = jnp.zeros((CONV_PAD, C_CHANNELS), F32)

    h = _rms(x_ref[...], g_ref[...]).astype(BF16)
    cab = _dot(h, w_ref[:, COL_CAB:D_IN])
    up[0:CONV_PAD, :] = up[tm:tm + CONV_PAD, :]
    up[CONV_PAD:CONV_PAD + tm, :] = cab[:, 0:C_CHANNELS] * jax.nn.sigmoid(cab[:, C_CHANNELS:2 * C_CHANNELS])
    nc_ref[...] = up[CONV_PAD + tm - hist:CONV_PAD + tm, :]
    cols = ((aq_ref, COL_QA, COL_KVA), (akv_ref, COL_KVA, COL_QKB), (bqk_ref, COL_QKB, COL_VB),
            (bv_ref, COL_VB, COL_GB), (bg_ref, COL_GB, COL_CAB))
    for i in range(tm // tt):
        ext = up[i * tt:i * tt + tt + CONV_PAD, :]
        oc_ref[i * tt:(i + 1) * tt, :] = _conv_tile(ext, cw_ref, cb_ref, lng_ref, lnb_ref, tt).astype(BF16)
        for o_ref, lo, hi in cols[i::tm // tt]:
            res = _dot(h, w_ref[:, lo:hi])
            o_ref[...] = res.astype(BF16)
            if o_ref is akv_ref:
                kt_ref[...] = res[:, 0:D_A]
                vt_ref[...] = res[:, D_A:2 * D_A]


def _in_proj_conv(x, g, w_in, conv_w, conv_b, ln_g, ln_b):
    n, t, _ = x.shape
    tm = min(t, 512)
    assert tm == A_REACH and t % tm == 0
    tt = min(tm, 128)
    hist = CONV_WIDTH - 1
    widths = (D_A, 2 * D_A, 2 * D_BQK, D_B, D_B, C_CHANNELS)
    tile = lambda w: pl.BlockSpec((None, tm, w), lambda b, j: (b, j, 0))
    last = lambda rows, w: pl.BlockSpec((None, rows, w), lambda b, j: (b, 0, 0))
    return pl.pallas_call(
        functools.partial(_in_proj_conv_kernel, tm=tm, tt=tt),
        grid=(n, t // tm),
        in_specs=[tile(D_MODEL), _const_spec((1, D_MODEL)), _const_spec((D_MODEL, D_IN)),
                  _const_spec((CONV_WIDTH, C_CHANNELS))] + [_const_spec((1, C_CHANNELS))] * 3,
        out_specs=[tile(w) for w in widths] + [last(hist, C_CHANNELS), last(tm, D_A), last(tm, D_A)],
        out_shape=[jax.ShapeDtypeStruct((n, t, w), BF16) for w in widths]
                  + [jax.ShapeDtypeStruct((n, hist, C_CHANNELS), F32)]
                  + [jax.ShapeDtypeStruct((n, tm, D_A), F32)] * 2,
        scratch_shapes=[pltpu.VMEM((CONV_PAD + tm, C_CHANNELS), F32)],
        compiler_params=_params(("parallel", "arbitrary")),
        name="in_proj_conv",
    )(x, g, w_in, conv_w, conv_b, ln_g, ln_b)


def _in_proj(x2d, g, w_in):
    m = x2d.shape[0]
    tm = min(m, 512)
    widths = ((256, BF16), (512, F32), (512, BF16), (512, BF16), (512, BF16), (512, F32))
    return pl.pallas_call(
        _in_proj_kernel,
        grid=(m // tm,),
        in_specs=[pl.BlockSpec((tm, D_MODEL), lambda i: (i, 0)),
                  _const_spec((1, D_MODEL)),
                  _const_spec((D_MODEL, D_IN))],
        out_specs=[pl.BlockSpec((tm, w), lambda i: (i, 0)) for w, _ in widths],
        out_shape=[jax.ShapeDtypeStruct((m, w), dt) for w, dt in widths],
        compiler_params=_params(("parallel",)),
        name="in_proj",
    )(x2d, g, w_in)


def _band_attn_kernel(*refs, t, tq, nsub, has_hist):
    if has_hist:
        aq_ref, akv_ref, hk_ref, hv_ref, bias_ref, o_ref, kc, vc = refs
    else:
        aq_ref, akv_ref, bias_ref, o_ref, kc, vc = refs
    j = pl.program_id(1)
    span = A_REACH + tq

    @pl.when(j == 0)
    def _():
        if has_hist:
            kc[0:A_REACH, :] = hk_ref[...].astype(BF16)
            vc[0:A_REACH, :] = hv_ref[...].astype(BF16)
        else:
            kc[0:A_REACH, :] = jnp.zeros((A_REACH, D_A), BF16)
            vc[0:A_REACH, :] = jnp.zeros((A_REACH, D_A), BF16)
        kc[A_REACH:A_REACH + t, :] = akv_ref[:, 0:D_A].astype(BF16)
        vc[A_REACH:A_REACH + t, :] = akv_ref[:, D_A:2 * D_A].astype(BF16)

    lane = lax.broadcasted_iota(jnp.int32, (tq, LANES), 1)
    col = lax.broadcasted_iota(jnp.int32, (tq, span), 1)
    for sub in range(nsub):
        t0 = pl.multiple_of((j * nsub + sub) * tq, tq)
        q = aq_ref[sub * tq:(sub + 1) * tq, :]
        outs = []
        for p in range(A_HEADS // 2):
            qp = q[:, p * LANES:(p + 1) * LANES]
            kp = kc[pl.ds(t0, span), p * LANES:(p + 1) * LANES]
            vp = vc[pl.ds(t0, span), p * LANES:(p + 1) * LANES]
            o_pair = None
            for hh in range(2):
                own = (lane < A_HEAD_DIM) if hh == 0 else (lane >= A_HEAD_DIM)
                qm = jnp.where(own, qp.astype(F32) * A_SCALE, 0.0).astype(BF16)
                s = _dot_nt(qm, kp) + bias_ref[2 * p + hh]
                if not has_hist:
                    s = jnp.where(col >= A_REACH - t0, s, NEG_INF)
                m = jnp.max(s, axis=-1, keepdims=True)
                e = jnp.exp(s - m)
                l = jnp.sum(e, axis=-1, keepdims=True)
                o = _dot(e.astype(BF16), vp) / l
                o_pair = o if hh == 0 else jnp.where(own, o, o_pair)
            outs.append(o_pair)
        o_ref[sub * tq:(sub + 1) * tq, :] = jnp.concatenate(outs, axis=1).astype(BF16)


def _band_bias(rel_bias_l, tq):
    span = A_REACH + tq
    period = span + tq
    n_far = A_REACH - REL_CLIP + 1
    far = rel_bias_l[:, 2 * REL_CLIP:]
    n_near = min(2 * REL_CLIP, span - n_far)
    near = jnp.flip(rel_bias_l[:, :2 * REL_CLIP], axis=1)[:, :n_near]
    beyond = jnp.broadcast_to(rel_bias_l[:, :1], (A_HEADS, span - n_far - n_near))
    u = jnp.concatenate([jnp.broadcast_to(far, (A_HEADS, n_far)), near, beyond,
                         jnp.broadcast_to(far, (A_HEADS, tq))], axis=1).astype(F32)
    skew = jnp.broadcast_to(u[:, None, :], (A_HEADS, tq, period)).reshape(A_HEADS, tq * period)
    toeplitz = skew[:, :tq * (period - 1)].reshape(A_HEADS, tq, period - 1)[:, :, :span]
    i = np.arange(tq)[:, None]
    j = np.arange(span)[None, :]
    in_band = (j // CHUNK >= i // CHUNK) & (j // CHUNK <= i // CHUNK + A_LEFT_CHUNKS)
    return jnp.where(in_band[None], toeplitz, NEG_INF)


def _band_attn(aq, akv, hist, rel_bias_l):
    n, t, _ = aq.shape
    tq = min(t, 256)
    nsub = 8 if t % (8 * tq) == 0 else 1
    span = A_REACH + tq
    bias = _band_bias(rel_bias_l, tq)
    has_hist = hist is not None
    in_specs = [pl.BlockSpec((None, nsub * tq, D_A), lambda b, j: (b, j, 0)),
                pl.BlockSpec((None, t, 2 * D_A), lambda b, j: (b, 0, 0))]
    args = [aq, akv]
    if has_hist:
        in_specs += [pl.BlockSpec((None, A_REACH, D_A), lambda b, j: (b, 0, 0))] * 2
        args += list(hist)
    in_specs.append(_const_spec((A_HEADS, tq, span)))
    args.append(bias)
    return pl.pallas_call(
        functools.partial(_band_attn_kernel, t=t, tq=tq, nsub=nsub, has_hist=has_hist),
        grid=(n, t // (nsub * tq)),
        in_specs=in_specs,
        out_specs=pl.BlockSpec((None, nsub * tq, D_A), lambda b, j: (b, j, 0)),
        out_shape=jax.ShapeDtypeStruct((n, t, D_A), BF16),
        scratch_shapes=[pltpu.VMEM((A_REACH + t, D_A), BF16)] * 2,
        compiler_params=_params(("parallel", "arbitrary")),
        name="band_attn",
    )(*args)


def _swap_halves(x):
    lane = lax.broadcasted_iota(jnp.int32, x.shape, 1)
    first = (lane % B_QK_DIM) < (B_QK_DIM // 2)
    return jnp.where(first, pltpu.roll(x, LANES - B_QK_DIM // 2, axis=1), pltpu.roll(x, B_QK_DIM // 2, axis=1))


def _retention_kernel(bqk_ref, bv_ref, bg_ref, cos_ref, sin_ref, dmask_ref, qdec_ref, kdec_ref, sdec_ref,
                      s0_ref, gn_ref, ob_ref, sfin_ref, st, *, t, bc):
    for h in range(B_HEADS):
        off = (h % 2) * B_QK_DIM
        st[h] = jnp.zeros((LANES, B_V_DIM), F32)
        st[h, off:off + B_QK_DIM, :] = s0_ref[h]

    lane = lax.broadcasted_iota(jnp.int32, (bc, LANES), 1)

    def chunk(c, carry):
        r0 = pl.multiple_of(c * bc, bc)
        rows = pl.ds(r0, bc)
        cs = cos_ref[rows, :]
        sn = sin_ref[rows, :]
        for p in range(B_HEADS // 2):
            qx = bqk_ref[rows, p * LANES:(p + 1) * LANES].astype(F32)
            kx = bqk_ref[rows, D_BQK + p * LANES:D_BQK + (p + 1) * LANES].astype(F32)
            qr = qx * cs + _swap_halves(qx) * sn
            kr = (kx * cs + _swap_halves(kx) * sn) * (B_QK_DIM ** -0.5)
            for hh in range(2):
                h = 2 * p + hh
                own = (lane < B_QK_DIM) if hh == 0 else (lane >= B_QK_DIM)
                qh = jnp.where(own, qr, 0.0)
                kh = kr
                v = bv_ref[rows, h * B_V_DIM:(h + 1) * B_V_DIM]
                att = _dot_nt(qh.astype(BF16), kh.astype(BF16)) * dmask_ref[h]
                o = (_dot(att.astype(BF16), v)
                     + _dot((qh * qdec_ref[h]).astype(BF16), st[h].astype(BF16)))
                st[h] = st[h] * sdec_ref[h] + _dot_tn((kh * kdec_ref[h]).astype(BF16), v)
                mu = jnp.mean(o, axis=-1, keepdims=True)
                d = o - mu
                var = jnp.mean(d * d, axis=-1, keepdims=True)
                on = d * lax.rsqrt(var + EPS) * gn_ref[:, h * B_V_DIM:(h + 1) * B_V_DIM]
                g = bg_ref[rows, h * B_V_DIM:(h + 1) * B_V_DIM].astype(F32)
                ob_ref[rows, h * B_V_DIM:(h + 1) * B_V_DIM] = (g * jax.nn.sigmoid(g) * on).astype(BF16)
        return carry

    lax.fori_loop(0, t // bc, chunk, 0)
    for h in range(B_HEADS):
        off = (h % 2) * B_QK_DIM
        sfin_ref[h] = st[h, off:off + B_QK_DIM, :]


def _retention_tables(t, bc, pos0):
    log_g = jnp.log(1.0 - 2.0 ** (-(RET_GAMMA_EXP0 + jnp.arange(B_HEADS, dtype=F32))))
    i = jnp.arange(bc, dtype=F32)
    diff = i[:, None] - i[None, :]
    dmask = jnp.where(diff[None] >= 0, jnp.exp(jnp.maximum(diff, 0.0)[None] * log_g[:, None, None]), 0.0)
    qdec = jnp.exp((i[None, :] + 1.0) * log_g[:, None])
    kdec = jnp.exp((bc - 1.0 - i)[None, :] * log_g[:, None])
    sdec = jnp.exp(bc * log_g)
    qdec = jnp.broadcast_to(qdec[:, :, None], (B_HEADS, bc, LANES))
    kdec = jnp.broadcast_to(kdec[:, :, None], (B_HEADS, bc, LANES))
    sdec = jnp.broadcast_to(sdec[:, None, None], (B_HEADS, 1, B_V_DIM))
    half = B_QK_DIM // 2
    pos = pos0 + jnp.arange(t, dtype=F32)
    inv_freq = ROPE_BASE ** (-jnp.arange(half, dtype=F32) / half)
    ang = pos[:, None] * inv_freq[None, :]
    cos = jnp.tile(jnp.cos(ang), (1, LANES // half))
    sin = jnp.tile(jnp.concatenate([-jnp.sin(ang), jnp.sin(ang)], axis=1), (1, LANES // B_QK_DIM))
    return cos, sin, dmask, qdec, kdec, sdec


def _retention(bqk, bv, bg, s0, gn_g, pos0):
    n, t, _ = bqk.shape
    bc = min(t, 256)
    tables = _retention_tables(t, bc, pos0)
    seq = lambda w: pl.BlockSpec((None, t, w), lambda b: (b, 0, 0))
    state = pl.BlockSpec((None, B_HEADS, B_QK_DIM, B_V_DIM), lambda b: (b, 0, 0, 0))
    return pl.pallas_call(
        functools.partial(_retention_kernel, t=t, bc=bc),
        grid=(n,),
        in_specs=[seq(2 * D_BQK), seq(D_B), seq(D_B)] + [_const_spec(tb.shape) for tb in tables]
                 + [state, _const_spec((1, D_B))],
        out_specs=[seq(D_B), state],
        out_shape=[jax.ShapeDtypeStruct((n, t, D_B), BF16),
                   jax.ShapeDtypeStruct((n, B_HEADS, B_QK_DIM, B_V_DIM), F32)],
        scratch_shapes=[pltpu.VMEM((B_HEADS, LANES, B_V_DIM), F32)],
        compiler_params=_params(("parallel",)),
        name="retention",
    )(bqk, bv, bg, *tables, s0, gn_g)


def _conv_kernel(cab_ref, buf_ref, w_ref, b_ref, lng_ref, lnb_ref, oc_ref, nc_ref, up, *, t, tt):
    hist = CONV_WIDTH - 1
    up[0:CONV_PAD, :] = jnp.zeros((CONV_PAD, C_CHANNELS), F32)
    up[CONV_PAD - hist:CONV_PAD, :] = buf_ref[...]

    def glu(i, carry):
        rows = pl.ds(pl.multiple_of(i * tt, tt), tt)
        ca = cab_ref[rows, 0:C_CHANNELS]
        cb = cab_ref[rows, C_CHANNELS:2 * C_CHANNELS]
        up[pl.ds(pl.multiple_of(CONV_PAD + i * tt, 8), tt), :] = ca * jax.nn.sigmoid(cb)
        return carry

    lax.fori_loop(0, t // tt, glu, 0)

    def tile(i, carry):
        t0 = pl.multiple_of(i * tt, tt)
        ext = up[pl.ds(t0, tt + CONV_PAD), :]
        oc_ref[pl.ds(t0, tt), :] = _conv_tile(ext, w_ref, b_ref, lng_ref, lnb_ref, tt).astype(BF16)
        return carry

    lax.fori_loop(0, t // tt, tile, 0)
    nc_ref[...] = up[CONV_PAD + t - hist:CONV_PAD + t, :]


def _conv_module(cab, buf, conv_w, conv_b, ln_g, ln_b):
    n, t, _ = cab.shape
    tt = min(t, 128)
    hist = CONV_WIDTH - 1
    return pl.pallas_call(
        functools.partial(_conv_kernel, t=t, tt=tt),
        grid=(n,),
        in_specs=[pl.BlockSpec((None, t, 2 * C_CHANNELS), lambda b: (b, 0, 0)),
                  pl.BlockSpec((None, hist, C_CHANNELS), lambda b: (b, 0, 0)),
                  _const_spec((CONV_WIDTH, C_CHANNELS))] + [_const_spec((1, C_CHANNELS))] * 3,
        out_specs=[pl.BlockSpec((None, t, C_CHANNELS), lambda b: (b, 0, 0)),
                   pl.BlockSpec((None, hist, C_CHANNELS), lambda b: (b, 0, 0))],
        out_shape=[jax.ShapeDtypeStruct((n, t, C_CHANNELS), BF16),
                   jax.ShapeDtypeStruct((n, hist, C_CHANNELS), F32)],
        scratch_shapes=[pltpu.VMEM((CONV_PAD + t, C_CHANNELS), F32)],
        compiler_params=_params(("parallel",)),
        name="conv_module",
    )(cab, buf, conv_w, conv_b, ln_g, ln_b)


def _mem_kv_kernel(mem_ref, wk_ref, wv_ref, mk_ref, mv_ref, mkh_ref, mvh_ref):
    m = mem_ref[...].astype(BF16)
    n_mem = mkh_ref.shape[1]
    for w_ref, rows_ref, heads_ref in ((wk_ref, mk_ref, mkh_ref), (wv_ref, mv_ref, mvh_ref)):
        res = _dot(m, w_ref[...])
        rows_ref[...] = res.astype(BF16)
        for b in range(heads_ref.shape[0]):
            for h in range(X_HEADS):
                heads_ref[b, :, h, :] = res[b * n_mem:(b + 1) * n_mem, h * X_HEAD_DIM:(h + 1) * X_HEAD_DIM]


def _mem_kv(mem2d, wk, wv, n_mem):
    depth = wk.shape[0]
    m = mem2d.shape[0]
    tm = 512
    nb = tm // n_mem
    wspec = pl.BlockSpec((None, D_MODEL, D_MODEL), lambda l, i: (l, 0, 0))
    ospec = pl.BlockSpec((None, tm, D_MODEL), lambda l, i: (l, i, 0))
    hspec = pl.BlockSpec((None, nb, n_mem, X_HEADS, X_HEAD_DIM), lambda l, i: (l, i, 0, 0, 0))
    heads = jax.ShapeDtypeStruct((depth, m // n_mem, n_mem, X_HEADS, X_HEAD_DIM), F32)
    return pl.pallas_call(
        _mem_kv_kernel,
        grid=(depth, m // tm),
        in_specs=[pl.BlockSpec((tm, D_MODEL), lambda l, i: (i, 0)), wspec, wspec],
        out_specs=[ospec, ospec, hspec, hspec],
        out_shape=[jax.ShapeDtypeStruct((depth, m, D_MODEL), BF16)] * 2 + [heads, heads],
        compiler_params=_params(("parallel", "parallel")),
        name="mem_kv",
    )(mem2d, wk, wv)


def _route(logits):
    lane = lax.broadcasted_iota(jnp.int32, logits.shape, 1).astype(F32)
    valid = lane < N_EXPERTS
    lg = jnp.where(valid, logits, NEG_INF)
    e = jnp.exp(lg - jnp.max(lg, axis=-1, keepdims=True))
    probs = e / jnp.sum(e, axis=-1, keepdims=True)
    p1 = jnp.max(probs, axis=-1, keepdims=True)
    i1 = jnp.min(jnp.where(probs == p1, lane, float(LANES)), axis=-1, keepdims=True)
    rest = jnp.where(lane == i1, -1.0, probs)
    rest = jnp.where(valid, rest, -1.0)
    p2 = jnp.max(rest, axis=-1, keepdims=True)
    i2 = jnp.min(jnp.where(rest == p2, lane, float(LANES)), axis=-1, keepdims=True)
    tot = p1 + p2
    return lane, i1, i2, p1 / tot, p2 / tot


def _sort_tile_by_expert(logits, h_hi, tri_ref, hs_ref, route_ref, meta_ref):
    tm = logits.shape[0]
    lt = jnp.transpose(logits)
    lt = lt[0:N_EXPERTS, :] + lt[N_EXPERTS:2 * N_EXPERTS, :]
    sub = lax.broadcasted_iota(jnp.int32, (N_EXPERTS, tm), 0).astype(F32)
    ex = jnp.exp(lt - jnp.max(lt, axis=0, keepdims=True))
    probs = ex / jnp.sum(ex, axis=0, keepdims=True)
    p1 = jnp.max(probs, axis=0, keepdims=True)
    i1 = jnp.min(jnp.where(probs == p1, sub, float(N_EXPERTS)), axis=0, keepdims=True)
    rest = jnp.where(sub == i1, -1.0, probs)
    p2 = jnp.max(rest, axis=0, keepdims=True)
    i2 = jnp.min(jnp.where(rest == p2, sub, float(N_EXPERTS)), axis=0, keepdims=True)
    w1 = p1 / (p1 + p2)
    w2 = p2 / (p1 + p2)
    oh1 = jnp.where(sub == i1, 1.0, 0.0)
    oh2 = jnp.where(sub == i2, 1.0, 0.0)
    before1 = _dot(oh1.astype(BF16), tri_ref[...])
    before2 = _dot(oh2.astype(BF16), tri_ref[...])
    cnt1 = jnp.sum(oh1, axis=1, keepdims=True)
    cnt2 = jnp.sum(oh2, axis=1, keepdims=True)
    padded = jnp.floor((cnt1 + cnt2 + (MOE_CHUNK - 1.0)) * (1.0 / MOE_CHUNK)) * MOE_CHUNK
    run = jnp.broadcast_to(padded, (N_EXPERTS, LANES))
    sub8 = lax.broadcasted_iota(jnp.int32, (N_EXPERTS, LANES), 0)
    incl = run
    for k in (1, 2, 4):
        incl = incl + jnp.where(sub8 >= k, pltpu.roll(incl, k, axis=0), 0.0)
    start = (incl - run)[:, 0:1]
    d1 = jnp.sum(oh1 * (start + before1), axis=0, keepdims=True)
    d2 = jnp.sum(oh2 * (start + cnt1 + before2), axis=0, keepdims=True)
    row = lax.broadcasted_iota(jnp.int32, (MOE_TILE_CAP, tm), 0).astype(F32)
    perm = jnp.where(row == d1, 1.0, jnp.where(row == d2, 1.0, 0.0)).astype(BF16)
    hs_ref[...] = _dot(perm, h_hi).astype(BF16)
    info = jnp.concatenate([d1, d2, w1, w2, jnp.zeros((LANES - 4, tm), F32)], axis=0)
    route_ref[...] = jnp.transpose(info)
    lane8 = lax.broadcasted_iota(jnp.int32, (N_EXPERTS, LANES), 1)
    meta_ref[...] = jnp.where(lane8 == 0, start * (1.0 / MOE_CHUNK),
                              jnp.where(lane8 == 1, padded * (1.0 / MOE_CHUNK), 0.0))


def _post_mix_kernel(*refs, mode):
    (x_ref, oa_ref, ob_ref, oc_ref, wout_ref, gmem_ref, wq_ref, mk_ref, mv_ref, wo_ref, gffn_ref) = refs[:11]
    if mode == "dense":
        wg_ref, wu_ref, wd_ref, x3_ref = refs[11:]
    elif mode == "dense_split":
        x2_ref, h_ref = refs[11:]
    elif mode == "experts":
        router_ref, x2_ref, h_ref, route_ref = refs[11:]
    else:
        router_ref, tri_ref, x2_ref, h_ref, route_ref, meta_ref = refs[11:]
    y = (_dot(oa_ref[...], wout_ref[0:D_A, :])
         + _dot(ob_ref[...], wout_ref[D_A:D_A + D_B, :])
         + _dot(oc_ref[...], wout_ref[D_A + D_B:D_MODEL, :]))
    x1 = x_ref[...] + y
    q = _dot(_rms(x1, gmem_ref[...]).astype(BF16), wq_ref[...]).astype(BF16)
    outs = []
    for h in range(X_HEADS):
        cols = slice(h * X_HEAD_DIM, (h + 1) * X_HEAD_DIM)
        s = _dot_nt(q[:, cols], mk_ref[:, cols].astype(BF16)) * (X_HEAD_DIM ** -0.5)
        e = jnp.exp(s - jnp.max(s, axis=-1, keepdims=True))
        l = jnp.sum(e, axis=-1, keepdims=True)
        outs.append((_dot(e.astype(BF16), mv_ref[:, cols].astype(BF16)) / l).astype(BF16))
    x2 = x1 + _dot(jnp.concatenate(outs, axis=1), wo_ref[...])
    hf = _rms(x2, gffn_ref[...])
    h_hi = hf.astype(BF16)
    if mode == "dense":
        acc = x2
        for c in range(D_FF // FF_CHUNK):
            cols = slice(c * FF_CHUNK, (c + 1) * FF_CHUNK)
            g = _dot(h_hi, wg_ref[:, cols])
            u = _dot(h_hi, wu_ref[:, cols])
            acc = acc + _dot((g * jax.nn.sigmoid(g) * u).astype(BF16), wd_ref[cols, :])
        x3_ref[...] = acc
        return
    x2_ref[...] = x2
    if mode == "dense_split":
        h_ref[...] = h_hi
        return
    h_lo = (hf - h_hi.astype(F32)).astype(BF16)
    logits = _dot(h_hi, router_ref[...]) + _dot(h_lo, router_ref[...])
    if mode == "experts":
        h_ref[...] = h_hi
        lane, i1, i2, w1, w2 = _route(logits + pltpu.roll(logits, LANES - N_EXPERTS, axis=1))
        route_ref[...] = jnp.where(lane == i1, w1, jnp.where(lane == i2, w2, 0.0))
    else:
        _sort_tile_by_expert(logits, h_hi, tri_ref, h_ref, route_ref, meta_ref)


def _post_mix(x, oa, ob, oc, w_out, g_mem, wq, mk, mv, mem_base, wo, g_ffn, tail, mode):
    n, t, _ = x.shape
    tm = min(t, 512)
    tile = lambda w: pl.BlockSpec((None, tm, w), lambda b, i: (b, i, 0))
    mem = pl.BlockSpec((None,) + mk.shape[1:], lambda b, i: (mem_base + b, 0, 0))
    sq = _const_spec((D_MODEL, D_MODEL))
    vec = _const_spec((1, D_MODEL))
    in_specs = [tile(D_MODEL), tile(D_A), tile(D_B), tile(C_CHANNELS), sq, vec, sq, mem, mem, sq, vec]
    args = [x, oa, ob, oc, w_out, g_mem, wq, mk, mv, wo, g_ffn]
    out_specs = [tile(D_MODEL)]
    out_shape = [jax.ShapeDtypeStruct((n, t, D_MODEL), F32)]
    if mode == "dense":
        in_specs += [_const_spec((D_MODEL, D_FF)), _const_spec((D_MODEL, D_FF)), _const_spec((D_FF, D_MODEL))]
        args += list(tail)
        return pl.pallas_call(
            functools.partial(_post_mix_kernel, mode=mode),
            grid=(n, t // tm),
            in_specs=in_specs,
            out_specs=out_specs[0],
            out_shape=out_shape[0],
            compiler_params=_params(("parallel", "parallel")),
            name="post_mix_ffn",
        )(*args)
    if mode == "dense_split":
        return pl.pallas_call(
            functools.partial(_post_mix_kernel, mode=mode),
            grid=(n, t // tm),
            in_specs=in_specs,
            out_specs=[tile(D_MODEL), tile(D_MODEL)],
            out_shape=[out_shape[0], jax.ShapeDtypeStruct((n, t, D_MODEL), BF16)],
            compiler_params=_params(("parallel", "parallel")),
            name="post_mix",
        )(*args)
    in_specs.append(_const_spec((D_MODEL, LANES)))
    args.append(tail)
    if mode == "grouped":
        assert tm == MOE_TILE
        in_specs.append(_const_spec((tm, tm)))
        args.append(jnp.triu(jnp.ones((tm, tm), BF16), k=1))
        out_specs.append(pl.BlockSpec((None, MOE_TILE_CAP, D_MODEL), lambda b, i: (b, i, 0)))
        out_shape.append(jax.ShapeDtypeStruct((n, t // tm * MOE_TILE_CAP, D_MODEL), BF16))
    else:
        out_specs.append(tile(D_MODEL))
        out_shape.append(jax.ShapeDtypeStruct((n, t, D_MODEL), BF16))
    out_specs.append(tile(LANES))
    out_shape.append(jax.ShapeDtypeStruct((n, t, LANES), F32))
    if mode == "grouped":
        out_specs.append(pl.BlockSpec((None, N_EXPERTS, LANES), lambda b, i: (b, i, 0)))
        out_shape.append(jax.ShapeDtypeStruct((n, t // tm * N_EXPERTS, LANES), F32))
    return pl.pallas_call(
        functools.partial(_post_mix_kernel, mode=mode),
        grid=(n, t // tm),
        in_specs=in_specs,
        out_specs=out_specs,
        out_shape=out_shape,
        compiler_params=_params(("parallel", "parallel")),
        name="post_mix",
    )(*args)


def _ffn_kernel(x_ref, h_ref, wg_ref, wu_ref, wd_ref, o_ref):
    h = h_ref[...]
    acc = x_ref[...]
    for c in range(D_FF // FF_CHUNK):
        cols = slice(c * FF_CHUNK, (c + 1) * FF_CHUNK)
        g = _dot(h, wg_ref[:, cols])
        u = _dot(h, wu_ref[:, cols])
        acc = acc + _dot((g * jax.nn.sigmoid(g) * u).astype(BF16), wd_ref[cols, :])
    o_ref[...] = acc


def _ffn(x2d, h2d, wg, wu, wd):
    m = x2d.shape[0]
    tm = min(m, 512)
    row = pl.BlockSpec((tm, D_MODEL), lambda i: (i, 0))
    return pl.pallas_call(
        _ffn_kernel,
        grid=(m // tm,),
        in_specs=[row, row, _const_spec((D_MODEL, D_FF)), _const_spec((D_MODEL, D_FF)),
                  _const_spec((D_FF, D_MODEL))],
        out_specs=row,
        out_shape=jax.ShapeDtypeStruct((m, D_MODEL), F32),
        compiler_params=_params(("parallel",)),
        name="ffn_dense",
    )(x2d, h2d, wg, wu, wd)


def _moe_kernel(x_ref, h_ref, comb_ref, wg_ref, wu_ref, wd_ref, gfin_ref, o_ref, acc):
    e = pl.program_id(1)

    @pl.when(e == 0)
    def _():
        acc[...] = x_ref[...]

    h = h_ref[...]
    g = _dot(h, wg_ref[...])
    u = _dot(h, wu_ref[...])
    y = _dot((g * jax.nn.sigmoid(g) * u).astype(BF16), wd_ref[...])
    comb = comb_ref[...]
    lane = lax.broadcasted_iota(jnp.int32, comb.shape, 1)
    ce = jnp.sum(jnp.where(lane == e, comb, 0.0), axis=-1, keepdims=True)
    acc[...] += ce * y

    @pl.when(e == N_EXPERTS - 1)
    def _():
        o_ref[...] = _rms(acc[...], gfin_ref[...])


def _moe(x2d, h2d, comb2d, wg, wu, wd, g_fin):
    m = x2d.shape[0]
    tm = min(m, 512)
    row = lambda w: pl.BlockSpec((tm, w), lambda i, e: (i, 0))
    return pl.pallas_call(
        _moe_kernel,
        grid=(m // tm, N_EXPERTS),
        in_specs=[row(D_MODEL), row(D_MODEL), row(LANES),
                  pl.BlockSpec((None, D_MODEL, D_FF_EXPERT), lambda i, e: (e, 0, 0)),
                  pl.BlockSpec((None, D_MODEL, D_FF_EXPERT), lambda i, e: (e, 0, 0)),
                  pl.BlockSpec((None, D_FF_EXPERT, D_MODEL), lambda i, e: (e, 0, 0)),
                  _const_spec((1, D_MODEL))],
        out_specs=row(D_MODEL),
        out_shape=jax.ShapeDtypeStruct((m, D_MODEL), F32),
        scratch_shapes=[pltpu.VMEM((tm, D_MODEL), F32)],
        compiler_params=_params(("parallel", "arbitrary")),
        name="moe_experts",
    )(x2d, h2d, comb2d, wg, wu, wd, g_fin)


def _moe_plan(meta, n_tiles):
    cpb = MOE_BLOCK // MOE_CHUNK
    tile_chunks = MOE_TILE_CAP // MOE_CHUNK
    nb = -(-(n_tiles * tile_chunks + N_EXPERTS * (cpb - 1)) // cpb)
    first = meta[:, :, 0].astype(jnp.int32)
    cnt = meta[:, :, 1].astype(jnp.int32)
    tot = jnp.sum(cnt, axis=0)
    eend = jnp.cumsum((tot + cpb - 1) // cpb * cpb)
    q = jnp.arange(nb * cpb, dtype=jnp.int32)
    e_q = jnp.minimum(jnp.sum((q[:, None] >= eend[None, :]).astype(jnp.int32), axis=1), N_EXPERTS - 1)
    oh_e = (e_q[:, None] == jnp.arange(N_EXPERTS, dtype=jnp.int32)[None, :]).astype(jnp.int32)
    pick = lambda table: jnp.sum(oh_e[:, :, None] * table.T[None, :, :], axis=1)
    j = q - jnp.sum(oh_e * (eend - (tot + cpb - 1) // cpb * cpb)[None, :], axis=1)
    valid = (j < jnp.sum(oh_e * tot[None, :], axis=1)) & (q < eend[N_EXPERTS - 1])
    incl = jnp.cumsum(cnt, axis=0)
    tile_q = jnp.minimum(jnp.sum((j[:, None] >= pick(incl)).astype(jnp.int32), axis=1), n_tiles - 1)
    oh_t = (tile_q[:, None] == jnp.arange(n_tiles, dtype=jnp.int32)[None, :]).astype(jnp.int32)
    chunk = (tile_q * tile_chunks + jnp.sum(oh_t * pick(first), axis=1)
             + j - jnp.sum(oh_t * pick(incl - cnt), axis=1))
    k = jnp.cumsum(jnp.logical_not(valid).astype(jnp.int32)) - 1
    used = jnp.sum(cnt, axis=1)
    free_incl = jnp.cumsum(tile_chunks - used)
    tile_k = jnp.minimum(jnp.sum((k[:, None] >= free_incl[None, :]).astype(jnp.int32), axis=1), n_tiles - 1)
    oh_k = (tile_k[:, None] == jnp.arange(n_tiles, dtype=jnp.int32)[None, :]).astype(jnp.int32)
    in_tile = (tile_k * tile_chunks + jnp.sum(oh_k * used[None, :], axis=1)
               + k - jnp.sum(oh_k * (free_incl - (tile_chunks - used))[None, :], axis=1))
    n_free = free_incl[n_tiles - 1]
    spare = jnp.where(k < n_free, in_tile, n_tiles * tile_chunks + k - n_free)
    src = jnp.where(valid, chunk, 0)
    dst = jnp.where(valid, chunk, spare)
    blk_expert = e_q[::cpb]
    n_active = (eend[N_EXPERTS - 1] // cpb).reshape(1)
    return blk_expert, n_active, src.reshape(nb, 1, cpb), dst.reshape(nb, 1, cpb)


def _moe_grouped_kernel(be_ref, nact_ref, src_ref, src_next_ref, dst_ref, hs_hbm, wg_ref, wu_ref, wd_ref,
                        y_hbm, xbuf, ybuf, sem_in, sem_out):
    b = pl.program_id(0)
    nb = pl.num_programs(0)
    slot = b % 2
    cpb = MOE_BLOCK // MOE_CHUNK

    def chunk_rows(i):
        return pl.ds(pl.multiple_of(i * MOE_CHUNK, MOE_CHUNK), MOE_CHUNK)

    def gather(idx_ref, s):
        for c in range(cpb):
            pltpu.make_async_copy(hs_hbm.at[chunk_rows(idx_ref[0, c]), :],
                                  xbuf.at[s, c * MOE_CHUNK:(c + 1) * MOE_CHUNK, :],
                                  sem_in.at[s]).start(priority=c % 2)

    def wait_gather(s):
        pltpu.make_async_copy(hs_hbm.at[pl.ds(0, MOE_BLOCK), :], xbuf.at[s], sem_in.at[s]).wait()

    def wait_scatter(s):
        pltpu.make_async_copy(ybuf.at[s], y_hbm.at[pl.ds(0, MOE_BLOCK), :], sem_out.at[s]).wait()

    @pl.when(b == 0)
    def _():
        gather(src_ref, 0)

    @pl.when(b + 1 < nb)
    def _():
        gather(src_next_ref, 1 - slot)

    wait_gather(slot)

    @pl.when(b >= 2)
    def _():
        wait_scatter(slot)

    @pl.when(b < nact_ref[0])
    def _():
        x = xbuf[slot]
        g = _dot(x, wg_ref[...])
        u = _dot(x, wu_ref[...])
        ybuf[slot] = _dot((g * jax.nn.sigmoid(g) * u).astype(BF16), wd_ref[...]).astype(BF16)

    @pl.when(b >= nact_ref[0])
    def _():
        ybuf[slot] = jnp.zeros((MOE_BLOCK, D_MODEL), BF16)

    for c in range(cpb):
        pltpu.make_async_copy(ybuf.at[slot, c * MOE_CHUNK:(c + 1) * MOE_CHUNK, :],
                              y_hbm.at[chunk_rows(dst_ref[0, c]), :],
                              sem_out.at[slot]).start(priority=c % 2)

    @pl.when(b == nb - 1)
    def _():
        wait_scatter(slot)
        wait_scatter(1 - slot)


def _moe_grouped(hs, blk_expert, n_active, src, dst, wg, wu, wd):
    nb, _, cpb = src.shape
    assert nb >= 2
    idx_spec = lambda f: pl.BlockSpec((None, 1, cpb), f, memory_space=pltpu.SMEM)
    wspec = lambda shape: pl.BlockSpec((None,) + shape, lambda b, be, na: (be[b], 0, 0))
    return pl.pallas_call(
        _moe_grouped_kernel,
        grid_spec=pltpu.PrefetchScalarGridSpec(
            num_scalar_prefetch=2,
            grid=(nb,),
            in_specs=[idx_spec(lambda b, be, na: (b, 0, 0)),
                      idx_spec(lambda b, be, na: (jnp.minimum(b + 1, nb - 1), 0, 0)),
                      idx_spec(lambda b, be, na: (b, 0, 0)),
                      pl.BlockSpec(memory_space=pl.ANY),
                      wspec((D_MODEL, D_FF_EXPERT)), wspec((D_MODEL, D_FF_EXPERT)),
                      wspec((D_FF_EXPERT, D_MODEL))],
            out_specs=pl.BlockSpec(memory_space=pl.ANY),
            scratch_shapes=[pltpu.VMEM((2, MOE_BLOCK, D_MODEL), BF16), pltpu.VMEM((2, MOE_BLOCK, D_MODEL), BF16),
                            pltpu.SemaphoreType.DMA((2,)), pltpu.SemaphoreType.DMA((2,))]),
        out_shape=jax.ShapeDtypeStruct((nb * MOE_BLOCK, D_MODEL), BF16),
        compiler_params=_params(("arbitrary",)),
        name="moe_grouped",
    )(blk_expert, n_active, src, src, dst, hs, wg, wu, wd)


def _moe_combine_kernel(x_ref, route_ref, y_ref, gfin_ref, o_ref):
    tm = x_ref.shape[0]
    route = route_ref[...]
    d1, d2, w1, w2 = route[:, 0:1], route[:, 1:2], route[:, 2:3], route[:, 3:4]
    col = lax.broadcasted_iota(jnp.int32, (tm, MOE_TILE_CAP), 1).astype(F32)
    wm = jnp.where(col == d1, w1, jnp.where(col == d2, w2, 0.0)).astype(BF16)
    acc = x_ref[...] + _dot(wm, y_ref[...])
    o_ref[...] = _rms(acc, gfin_ref[...])


def _moe_combine(x2d, route2d, ys, g_fin):
    m = x2d.shape[0]
    tm = MOE_TILE
    return pl.pallas_call(
        _moe_combine_kernel,
        grid=(m // tm,),
        in_specs=[pl.BlockSpec((tm, D_MODEL), lambda i: (i, 0)),
                  pl.BlockSpec((tm, LANES), lambda i: (i, 0)),
                  pl.BlockSpec((MOE_TILE_CAP, D_MODEL), lambda i: (i, 0)),
                  _const_spec((1, D_MODEL))],
        out_specs=pl.BlockSpec((tm, D_MODEL), lambda i: (i, 0)),
        out_shape=jax.ShapeDtypeStruct((m, D_MODEL), F32),
        compiler_params=_params(("parallel",)),
        name="moe_combine",
    )(x2d, route2d, ys, g_fin)


def _layer(x, pos0, attn_hist, ret_state, conv_state, mk, mv, mem_base, w):
    n, t, _ = x.shape
    conv_w = (w["conv_w"], w["conv_b"], w["conv_ln_g"], w["conv_ln_b"])
    keep = min(A_REACH, t)
    if conv_state is None:
        aq, akv, bqk, bv, bg, oc, new_conv, k_keep, v_keep = _in_proj_conv(
            x, w["norm_mix_g"], w["w_in"], *conv_w)
    else:
        aq, akv, bqk, bv, bg, cab = _in_proj(x.reshape(n * t, D_MODEL), w["norm_mix_g"], w["w_in"])
        r3 = lambda a: a.reshape(n, t, a.shape[-1])
        aq, akv, bqk, bv, bg, cab = map(r3, (aq, akv, bqk, bv, bg, cab))
        oc, new_conv = _conv_module(cab, conv_state, *conv_w)
        k_keep, v_keep = akv[:, t - keep:, :D_A], akv[:, t - keep:, D_A:]
    oa = _band_attn(aq, akv, attn_hist, w["rel_bias"])
    ob, new_s = _retention(bqk, bv, bg, ret_state, w["ret_gn_g"], pos0)
    m = n * t
    if "router" not in w:
        mode = "dense" if t >= MOE_TILE else "dense_split"
    elif m >= N_EXPERTS * MOE_BLOCK and t % MOE_TILE == 0:
        mode = "grouped"
    else:
        mode = "experts"
    ffn_w = (w.get("ffn_g"), w.get("ffn_u"), w.get("ffn_d"))
    tail = ffn_w if mode == "dense" else w.get("router")
    post = _post_mix(x, oa, ob, oc, w["w_out"], w["norm_mem_g"], w["wx_q"], mk, mv, mem_base, w["wx_o"],
                     w["norm_ffn_g"], tail, mode)
    if mode == "dense":
        x3 = post
    elif mode == "dense_split":
        x3 = _ffn(post[0].reshape(m, D_MODEL), post[1].reshape(m, D_MODEL), *ffn_w)
    elif mode == "experts":
        x2 = post[0].reshape(m, D_MODEL)
        x3 = _moe(x2, post[1].reshape(m, D_MODEL), post[2].reshape(m, LANES),
                  w["moe_g"], w["moe_u"], w["moe_d"], w["final_g"])
    else:
        x2 = post[0].reshape(m, D_MODEL)
        n_tiles = m // MOE_TILE
        blk_expert, n_active, src, dst = _moe_plan(post[3].reshape(n_tiles, N_EXPERTS, LANES), n_tiles)
        ys = _moe_grouped(post[1].reshape(n_tiles * MOE_TILE_CAP, D_MODEL), blk_expert, n_active, src, dst,
                          w["moe_g"], w["moe_u"], w["moe_d"])
        x3 = _moe_combine(x2, post[2].reshape(m, LANES), ys, w["final_g"])
    new_k = k_keep.reshape(n, keep, A_HEADS, A_HEAD_DIM)
    new_v = v_keep.reshape(n, keep, A_HEADS, A_HEAD_DIM)
    return x3.reshape(n, t, D_MODEL), new_k, new_v, new_s, new_conv


def kernel(x_prompt, x_sample, cache_attn_k, cache_attn_v, state_ret, state_conv, cache_mem_k, cache_mem_v,
           mem_prompt, norm_mix_g, w_in, rel_bias, ret_gn_g, conv_w, conv_b, conv_ln_g, conv_ln_b, w_out,
           norm_mem_g, wx_q, wx_k, wx_v, wx_o, norm_ffn_g, ffn_w_gate, ffn_w_up, ffn_w_down,
           router_w, moe_w_gate, moe_w_up, moe_w_down, final_norm_g):
    depth = w_in.shape[0]
    assert depth == 2, "layer 0 dense FFN, layer 1 experts + closing norm"
    n_p, _, _ = x_prompt.shape
    n_s, t_s, _ = x_sample.shape
    n_mem = mem_prompt.shape[1]
    row = lambda a: a.reshape(1, -1).astype(F32)
    xp, xs = x_prompt, x_sample
    mem2d = mem_prompt.reshape(n_p * n_mem, D_MODEL)
    outs_p = [[] for _ in range(4)]
    outs_s = [[] for _ in range(4)]
    mk_all, mv_all, mk_heads, mv_heads = _mem_kv(mem2d, wx_k.astype(BF16), wx_v.astype(BF16), n_mem)
    mk_p = mk_all.reshape(depth * n_p, n_mem, D_MODEL)
    mv_p = mv_all.reshape(depth * n_p, n_mem, D_MODEL)
    mk_s = cache_mem_k.reshape(depth * n_s, n_mem, D_MODEL)
    mv_s = cache_mem_v.reshape(depth * n_s, n_mem, D_MODEL)
    for l in range(depth):
        w = dict(norm_mix_g=row(norm_mix_g[l]), w_in=w_in[l].astype(BF16), rel_bias=rel_bias[l],
                 ret_gn_g=row(ret_gn_g[l]), conv_w=conv_w[l].astype(F32), conv_b=row(conv_b[l]),
                 conv_ln_g=row(conv_ln_g[l]), conv_ln_b=row(conv_ln_b[l]), w_out=w_out[l].astype(BF16),
                 norm_mem_g=row(norm_mem_g[l]), wx_q=wx_q[l].astype(BF16), wx_o=wx_o[l].astype(BF16),
                 norm_ffn_g=row(norm_ffn_g[l]))
        if l % 2 == 0:
            i = l // 2
            w.update(ffn_g=ffn_w_gate[i].astype(BF16), ffn_u=ffn_w_up[i].astype(BF16),
                     ffn_d=ffn_w_down[i].astype(BF16))
        else:
            i = l // 2
            r = router_w[i].astype(F32)
            r_hi = r.astype(BF16)
            r_lo = (r - r_hi.astype(F32)).astype(BF16)
            w.update(router=jnp.pad(jnp.concatenate([r_hi, r_lo], axis=1),
                                    ((0, 0), (0, LANES - 2 * N_EXPERTS))),
                     moe_g=moe_w_gate[i].astype(BF16), moe_u=moe_w_up[i].astype(BF16),
                     moe_d=moe_w_down[i].astype(BF16),
                     final_g=row(final_norm_g) if l == depth - 1 else None)
        xp, k_new, v_new, s_new, c_new = _layer(
            xp, 0.0, None, jnp.zeros((n_p, B_HEADS, B_QK_DIM, B_V_DIM), F32), None, mk_p, mv_p, l * n_p, w)
        for lst, a in zip(outs_p, (k_new, v_new, s_new, c_new)):
            lst.append(a)
        hist = (cache_attn_k[l].reshape(n_s, A_REACH, D_A), cache_attn_v[l].reshape(n_s, A_REACH, D_A))
        xs, k_new, v_new, s_new, c_new = _layer(
            xs, float(PAST_LEN), hist, state_ret[l], state_conv[l], mk_s, mv_s, l * n_s, w)
        for lst, a in zip(outs_s, (k_new, v_new, s_new, c_new)):
            lst.append(a)
    st = lambda lst: jnp.stack(lst)
    return (xp, xs, st(outs_p[0]), st(outs_p[1]), st(outs_p[2]), st(outs_p[3]), mk_heads, mv_heads,
            st(outs_s[0]), st(outs_s[1]), st(outs_s[2]), st(outs_s[3]))
```
